```python
import math
import jax, jax.numpy as jnp
from jax import lax
import numpy as np

D_MODEL = 1024
BATCH = 4
SEQ = 8192
DEPTH = 2

N_MIXERS = 2
N_HEADS = 16
HEAD_DIM = 64
N_KV_HEADS = 4
GROUP = N_HEADS // N_KV_HEADS
IDX_HEADS = 8
IDX_DIM = 64
TOPK_MAX = 256
Q_BLOCK = 128
ROPE_THETA = 10000.0
CONV_WIDTH = 3
FFN_HIDDEN = int(math.ceil(8 * D_MODEL / 3 / 256) * 256)
NORM_EPS = 1e-6
N_A = (DEPTH + 1) // 2
N_B = DEPTH // 2

Q_COLS = N_HEADS * HEAD_DIM
K_COLS = N_KV_HEADS * HEAD_DIM
V_COLS = N_KV_HEADS * HEAD_DIM
QI_COLS = IDX_HEADS * IDX_DIM
KI_COLS = IDX_DIM
WI_COLS = IDX_HEADS
ATTN_IN_COLS = Q_COLS + K_COLS + V_COLS + QI_COLS + KI_COLS + WI_COLS

kernel_name = "hybrid_dsa_shortconv_adaln_block"


def _rmsnorm(x, g):
    xf = x.astype(jnp.float32)
    y = xf * lax.rsqrt(jnp.mean(xf * xf, axis=-1, keepdims=True) + NORM_EPS)
    return (y * g.astype(jnp.float32)).astype(x.dtype)


def _layernorm(x, g, b):
    xf = x.astype(jnp.float32)
    mu = jnp.mean(xf, axis=-1, keepdims=True)
    var = jnp.mean(jnp.square(xf - mu), axis=-1, keepdims=True)
    y = (xf - mu) * lax.rsqrt(var + NORM_EPS)
    return (y * g.astype(jnp.float32) + b.astype(jnp.float32)).astype(x.dtype)


def _rope(x, pos):
    d = x.shape[-1]
    half = d // 2
    inv_freq = ROPE_THETA ** (-(jnp.arange(half, dtype=jnp.float32) * 2.0 / d))
    ang = pos.astype(jnp.float32)[..., None] * inv_freq
    cos = jnp.cos(ang)[:, :, None, :]
    sin = jnp.sin(ang)[:, :, None, :]
    xf = x.astype(jnp.float32)
    x1, x2 = xf[..., :half], xf[..., half:]
    out = jnp.concatenate([x1 * cos - x2 * sin, x2 * cos + x1 * sin], axis=-1)
    return out.astype(x.dtype)


def _dsa_mixer(h, positions, w_in, q_norm_g, k_norm_g, idx_k_ln_g, idx_k_ln_b, w_out):
    Bn, T, _ = h.shape
    proj = h @ w_in
    o0 = 0
    q = proj[..., o0:o0 + Q_COLS].reshape(Bn, T, N_HEADS, HEAD_DIM); o0 += Q_COLS
    k = proj[..., o0:o0 + K_COLS].reshape(Bn, T, N_KV_HEADS, HEAD_DIM); o0 += K_COLS
    v = proj[..., o0:o0 + V_COLS].reshape(Bn, T, N_KV_HEADS, HEAD_DIM); o0 += V_COLS
    qi = proj[..., o0:o0 + QI_COLS].reshape(Bn, T, IDX_HEADS, IDX_DIM); o0 += QI_COLS
    ki = proj[..., o0:o0 + KI_COLS]; o0 += KI_COLS
    wi = proj[..., o0:o0 + WI_COLS]

    q = _rope(_rmsnorm(q, q_norm_g), positions)
    k = _rope(_rmsnorm(k, k_norm_g), positions)
    qi = _rope(qi, positions)
    ki = _rope(_layernorm(ki, idx_k_ln_g, idx_k_ln_b)[:, :, None, :], positions)[:, :, 0, :]
    wi = wi * (IDX_HEADS ** -0.5 * IDX_DIM ** -0.5)

    topk = min(TOPK_MAX, T // 4)
    n_blk = T // Q_BLOCK
    key_idx = jnp.arange(T)
    scale = HEAD_DIM ** -0.5

    def to_blocks(a):
        return jnp.moveaxis(a.reshape(Bn, n_blk, Q_BLOCK, *a.shape[2:]), 1, 0)

    def block_fn(args):
        qb, qib, wib, start = args
        t_idx = start + jnp.arange(Q_BLOCK)
        s = jnp.einsum('bqhd,bsd->bqhs', qib, ki)
        score = jnp.einsum('bqhs,bqh->bqs', jax.nn.relu(s), wib).astype(jnp.float32)
        causal = key_idx[None, :] <= t_idx[:, None]
        score = jnp.where(causal[None], score, -jnp.inf)
        _, sel = lax.top_k(score, topk)
        sel_ok = sel <= t_idx[None, :, None]
        kg = jax.vmap(lambda a, i: a[i])(k, sel)
        vg = jax.vmap(lambda a, i: a[i])(v, sel)
        qg = qb.reshape(Bn, Q_BLOCK, N_KV_HEADS, GROUP, HEAD_DIM)
        logits = jnp.einsum('bqgrd,bqkgd->bqgrk', qg, kg).astype(jnp.float32) * scale
        logits = jnp.where(sel_ok[:, :, None, None, :], logits, -1e30)
        p = jax.nn.softmax(logits, axis=-1).astype(vg.dtype)
        o = jnp.einsum('bqgrk,bqkgd->bqgrd', p, vg)
        return o.reshape(Bn, Q_BLOCK, N_HEADS * HEAD_DIM)

    starts = jnp.arange(n_blk) * Q_BLOCK
    o = lax.map(block_fn, (to_blocks(q), to_blocks(qi), to_blocks(wi), starts))
    o = jnp.moveaxis(o, 0, 1).reshape(Bn, T, N_HEADS * HEAD_DIM)
    return o @ w_out


def _short_conv_mixer(h, w_in, conv_w, w_out):
    proj = h @ w_in
    b_gate, c_gate, u = jnp.split(proj, 3, axis=-1)
    z = c_gate * u
    z = lax.conv_general_dilated(
        z, conv_w[:, None, :].astype(z.dtype),
        window_strides=(1,), padding=[(CONV_WIDTH - 1, 0)],
        dimension_numbers=('NWC', 'WIO', 'NWC'),
        feature_group_count=z.shape[-1])
    return (b_gate * z) @ w_out


def _swiglu(h, w_gate, w_up, w_down):
    return (jax.nn.silu(h @ w_gate) * (h @ w_up)) @ w_down


def setup_inputs(seed: int = 0) -> dict:
    key = jax.random.key(seed)
    ks = jax.random.split(key, 20)
    D = D_MODEL

    def nrm(k, shape, s):
        return jax.random.normal(k, shape, dtype=jnp.float32) * s

    x = nrm(ks[0], (BATCH, SEQ, D), 1.0)
    c = nrm(ks[1], (BATCH, D), 1.0)
    positions = (jnp.arange(SEQ, dtype=jnp.int32)[None, :]
                 + jax.random.randint(ks[2], (BATCH, 1), 0, 1024, dtype=jnp.int32))
    ada_w = nrm(ks[3], (DEPTH, D, 6 * D), 0.5 * D ** -0.5)
    ada_b = nrm(ks[4], (DEPTH, 6 * D), 0.02)
    norm1_g = 1.0 + nrm(ks[5], (DEPTH, D), 0.02)
    norm2_g = 1.0 + nrm(ks[6], (DEPTH, D), 0.02)
    attn_w_in = nrm(ks[7], (N_A, D, ATTN_IN_COLS), D ** -0.5)
    attn_q_norm_g = 1.0 + nrm(ks[8], (N_A, HEAD_DIM), 0.02)
    attn_k_norm_g = 1.0 + nrm(ks[9], (N_A, HEAD_DIM), 0.02)
    idx_k_ln_g = 1.0 + nrm(ks[10], (N_A, IDX_DIM), 0.02)
    idx_k_ln_b = nrm(ks[11], (N_A, IDX_DIM), 0.02)
    attn_w_out = nrm(ks[12], (N_A, N_HEADS * HEAD_DIM, D), (N_HEADS * HEAD_DIM) ** -0.5)
    conv_w_in = nrm(ks[13], (N_B, D, 3 * D), D ** -0.5)
    conv_w = nrm(ks[14], (N_B, CONV_WIDTH, D), CONV_WIDTH ** -0.5)
    conv_w_out = nrm(ks[15], (N_B, D, D), D ** -0.5)
    ffn_w_gate = nrm(ks[16], (DEPTH, D, FFN_HIDDEN), D ** -0.5)
    ffn_w_up = nrm(ks[17], (DEPTH, D, FFN_HIDDEN), D ** -0.5)
    ffn_w_down = nrm(ks[18], (DEPTH, FFN_HIDDEN, D), FFN_HIDDEN ** -0.5)
    return {"x": x, "c": c, "positions": positions,
            "ada_w": ada_w, "ada_b": ada_b, "norm1_g": norm1_g, "norm2_g": norm2_g,
            "attn_w_in": attn_w_in, "attn_q_norm_g": attn_q_norm_g,
            "attn_k_norm_g": attn_k_norm_g, "idx_k_ln_g": idx_k_ln_g,
            "idx_k_ln_b": idx_k_ln_b, "attn_w_out": attn_w_out,
            "conv_w_in": conv_w_in, "conv_w": conv_w, "conv_w_out": conv_w_out,
            "ffn_w_gate": ffn_w_gate, "ffn_w_up": ffn_w_up, "ffn_w_down": ffn_w_down}


def reference(x, c, positions, ada_w, ada_b, norm1_g, norm2_g,
              attn_w_in, attn_q_norm_g, attn_k_norm_g, idx_k_ln_g, idx_k_ln_b, attn_w_out,
              conv_w_in, conv_w, conv_w_out, ffn_w_gate, ffn_w_up, ffn_w_down):
    c_act = jax.nn.silu(c)
    for i in range(DEPTH):
        mod = c_act @ ada_w[i] + ada_b[i]
        sh1, sc1, g1, sh2, sc2, g2 = [m[:, None, :] for m in jnp.split(mod, 6, axis=-1)]
        h = _rmsnorm(x, norm1_g[i]) * (1.0 + sc1) + sh1
        if i % N_MIXERS == 0:
            j = i // N_MIXERS
            y = _dsa_mixer(h, positions, attn_w_in[j], attn_q_norm_g[j], attn_k_norm_g[j],
                           idx_k_ln_g[j], idx_k_ln_b[j], attn_w_out[j])
        else:
            j = i // N_MIXERS
            y = _short_conv_mixer(h, conv_w_in[j], conv_w[j], conv_w_out[j])
        x = x + g1 * y
        h = _rmsnorm(x, norm2_g[i]) * (1.0 + sc2) + sh2
        x = x + g2 * _swiglu(h, ffn_w_gate[i], ffn_w_up[i], ffn_w_down[i])
    return x
```

```python
import functools
import math

import numpy as np
import jax
import jax.numpy as jnp
from jax import lax
from jax.experimental import pallas as pl
from jax.experimental.pallas import tpu as pltpu

F32 = jnp.float32
BF16 = jnp.bfloat16
I32 = jnp.int32

N_HEADS = 16
HEAD_DIM = 64
N_KV_HEADS = 4
GROUP = N_HEADS // N_KV_HEADS
IDX_HEADS = 8
IDX_DIM = 64
TOPK_MAX = 256
ROPE_THETA = 10000.0
CONV_WIDTH = 3
NORM_EPS = 1e-6

LANES = 128
SUBLANES = 8
VMEM_LIMIT = 56 * 1024 * 1024

Q_COLS = N_HEADS * HEAD_DIM
KV_COLS = N_KV_HEADS * HEAD_DIM
QI_COLS = IDX_HEADS * IDX_DIM
V_AUG = HEAD_DIM + 16

INT_MIN = np.int32(-2**31)
KEY_NEG_INF = np.int32(np.uint32(0x807FFFFF).astype(np.int64) - 2**32)


def _const_spec(shape):
    nd = len(shape)
    return pl.BlockSpec(shape, lambda *_: (0,) * nd, pipeline_mode=pl.Buffered(1))


def _params(n_axes):
    return pltpu.CompilerParams(dimension_semantics=("arbitrary",) * n_axes,
                                vmem_limit_bytes=VMEM_LIMIT)


def _mod_kernel(c_ref, w_ref, b_ref, o_ref):
    c = c_ref[...]
    ca = c * jax.nn.sigmoid(c)
    o_ref[0] = jnp.dot(ca.astype(BF16), w_ref[0].astype(BF16), preferred_element_type=F32) + b_ref[0]


def _modulation(c, ada_w, ada_b):
    depth, d, six_d = ada_w.shape
    bn = c.shape[0]
    rows = 16
    cp = jnp.zeros((rows, d), F32).at[:bn].set(c)
    tn = 1536
    out = pl.pallas_call(
        _mod_kernel,
        grid=(depth, six_d // tn),
        in_specs=[pl.BlockSpec((rows, d), lambda i, j: (0, 0)),
                  pl.BlockSpec((1, d, tn), lambda i, j: (i, 0, j)),
                  pl.BlockSpec((1, 1, tn), lambda i, j: (i, 0, j))],
        out_specs=pl.BlockSpec((1, rows, tn), lambda i, j: (i, 0, j)),
        out_shape=jax.ShapeDtypeStruct((depth, rows, six_d), F32),
        compiler_params=_params(2),
        name="adaln_mod",
    )(cp, ada_w, ada_b.reshape(depth, 1, six_d))
    return out[:, :bn].reshape(depth, bn, 6, d)


def _segmean64(v, gmat):
    hi = v.astype(BF16)
    lo = (v - hi.astype(F32)).astype(BF16)
    s = jnp.dot(hi, gmat, preferred_element_type=F32) + jnp.dot(lo, gmat, preferred_element_type=F32)
    return s * (1.0 / HEAD_DIM)


def _attn_pre_kernel(x_ref, mod_ref, g1_ref, wta_ref, wb_ref, posr_ref, posc_ref, invft_ref, invfr_ref,
                     gq_ref, gk_ref, lng_ref, lnb_ref,
                     qT_ref, qiT_ref, wiT_ref, vT_ref, k2_ref, kiP_ref):
    tm = x_ref.shape[0]
    x = x_ref[...]
    sh = mod_ref[0, 0:1, :]
    sc = mod_ref[0, 1:2, :]
    ms = jnp.mean(x * x, axis=-1, keepdims=True)
    h = (x * lax.rsqrt(ms + NORM_EPS) * g1_ref[...]) * (1.0 + sc) + sh
    hb = h.astype(BF16)

    pT = lax.dot_general(wta_ref[...], hb, (((1,), (1,)), ((), ())), preferred_element_type=F32)
    angT = invft_ref[...] * posr_ref[0].astype(F32)
    cT = jnp.cos(angT)
    sT = jnp.sin(angT)
    half = HEAD_DIM // 2

    def rope_t(y):
        x1 = y[:half]
        x2 = y[half:]
        return jnp.concatenate([x1 * cT - x2 * sT, x2 * cT + x1 * sT], axis=0)

    gq = gq_ref[...]
    for hh in range(N_HEADS):
        xq = pT[HEAD_DIM * hh:HEAD_DIM * (hh + 1)]
        inv = lax.rsqrt(jnp.mean(xq * xq, axis=0, keepdims=True) + NORM_EPS)
        y = xq * inv * gq
        qT_ref[0, hh] = (rope_t(y) * (HEAD_DIM ** -0.5)).astype(BF16)
    o0 = Q_COLS
    ones = jnp.ones((V_AUG - HEAD_DIM, tm), BF16)
    for g in range(N_KV_HEADS):
        vT_ref[0, g, 0, 0:HEAD_DIM, :] = pT[o0 + HEAD_DIM * g:o0 + HEAD_DIM * (g + 1)].astype(BF16)
        vT_ref[0, g, 0, HEAD_DIM:V_AUG, :] = ones
    o0 += KV_COLS
    for hh in range(IDX_HEADS):
        qiT_ref[0, hh] = rope_t(pT[o0 + IDX_DIM * hh:o0 + IDX_DIM * (hh + 1)]).astype(BF16)
    o0 += QI_COLS
    wiT_ref[0] = pT[o0:o0 + IDX_HEADS] * (IDX_HEADS ** -0.5 * IDX_DIM ** -0.5)

    pk = jnp.dot(hb, wb_ref[...], preferred_element_type=F32)
    ang = posc_ref[...].astype(F32) * invfr_ref[...]
    cR = jnp.cos(ang)
    sR = jnp.sin(ang)
    lane = lax.broadcasted_iota(I32, (tm, LANES), 1)
    first = (lane % HEAD_DIM) < half
    s_signed = jnp.where(first, -sR, sR)
    ri = lax.broadcasted_iota(I32, (LANES, LANES), 0) // HEAD_DIM
    ci = lax.broadcasted_iota(I32, (LANES, LANES), 1) // HEAD_DIM
    gmat = jnp.where(ri == ci, 1.0, 0.0).astype(BF16)

    def rope_r(y):
        rot = jnp.where(first, pltpu.roll(y, LANES - half, 1), pltpu.roll(y, half, 1))
        return y * cR + rot * s_signed

    for j in range(KV_COLS // LANES):
        xs = pk[:, LANES * j:LANES * (j + 1)]
        inv = lax.rsqrt(_segmean64(xs * xs, gmat) + NORM_EPS)
        k2_ref[:, LANES * j:LANES * (j + 1)] = rope_r(xs * inv * gk_ref[...]).astype(BF16)
    t = pk[:, KV_COLS:KV_COLS + LANES]
    mu = _segmean64(t, gmat)
    dlt = t - mu
    var = _segmean64(dlt * dlt, gmat)
    kin = dlt * lax.rsqrt(var + NORM_EPS) * lng_ref[...] + lnb_ref[...]
    kiP_ref[...] = rope_r(kin).astype(BF16)


def _attn_pre(x2d, mod, g1, w_in, gq, gk, lng, lnb, positions, bn, t_len, tm):
    n, d = x2d.shape
    tpb = t_len // tm
    half = HEAD_DIM // 2
    wq = w_in[:, :Q_COLS]
    wk = w_in[:, Q_COLS:Q_COLS + KV_COLS]
    wv = w_in[:, Q_COLS + KV_COLS:Q_COLS + 2 * KV_COLS]
    o0 = Q_COLS + 2 * KV_COLS
    wqi = w_in[:, o0:o0 + QI_COLS]
    wki = w_in[:, o0 + QI_COLS:o0 + QI_COLS + IDX_DIM]
    wwi = w_in[:, o0 + QI_COLS + IDX_DIM:]
    ma = Q_COLS + KV_COLS + QI_COLS + IDX_HEADS
    ma_pad = -(-ma // 16) * 16
    wta = jnp.concatenate([wq, wv, wqi, wwi, jnp.zeros((d, ma_pad - ma), F32)], axis=1).T.astype(BF16)
    wb = jnp.concatenate([wk, wki, jnp.zeros((d, LANES - IDX_DIM), F32)], axis=1).astype(BF16)
    nb = wb.shape[1]
    inv_freq = ROPE_THETA ** (-(jnp.arange(half, dtype=F32) * 2.0 / HEAD_DIM))
    invft = jnp.broadcast_to(inv_freq[:, None], (half, tm))
    invfr = jnp.tile(inv_freq, LANES // half).reshape(1, LANES)
    gq_b = jnp.broadcast_to(gq.astype(F32)[:, None], (HEAD_DIM, tm))
    gk_r = jnp.tile(gk.astype(F32), LANES // HEAD_DIM).reshape(1, LANES)
    zpad = jnp.zeros((LANES - IDX_DIM,), F32)
    lng_r = jnp.concatenate([lng.astype(F32), zpad]).reshape(1, LANES)
    lnb_r = jnp.concatenate([lnb.astype(F32), zpad]).reshape(1, LANES)
    posr = positions.reshape(bn, 1, t_len)
    posc = positions.reshape(n, 1)

    row = lambda i: (i, 0)
    outs = pl.pallas_call(
        _attn_pre_kernel,
        grid=(n // tm,),
        in_specs=[pl.BlockSpec((tm, d), row),
                  pl.BlockSpec((1, 6, d), lambda i: (i // tpb, 0, 0)),
                  _const_spec((1, d)),
                  _const_spec((ma_pad, d)),
                  _const_spec((d, nb)),
                  pl.BlockSpec((1, 1, tm), lambda i: (i // tpb, 0, i % tpb)),
                  pl.BlockSpec((tm, 1), row),
                  _const_spec((half, tm)),
                  _const_spec((1, LANES)),
                  _const_spec((HEAD_DIM, tm)),
                  _const_spec((1, LANES)),
                  _const_spec((1, LANES)),
                  _const_spec((1, LANES))],
        out_specs=[pl.BlockSpec((1, N_HEADS, HEAD_DIM, tm), lambda i: (i // tpb, 0, 0, i % tpb)),
                   pl.BlockSpec((1, IDX_HEADS, IDX_DIM, tm), lambda i: (i // tpb, 0, 0, i % tpb)),
                   pl.BlockSpec((1, IDX_HEADS, tm), lambda i: (i // tpb, 0, i % tpb)),
                   pl.BlockSpec((1, N_KV_HEADS, 1, V_AUG, tm), lambda i: (i // tpb, 0, i % tpb, 0, 0)),
                   pl.BlockSpec((tm, KV_COLS), row),
                   pl.BlockSpec((tm, LANES), row)],
        out_shape=[jax.ShapeDtypeStruct((bn, N_HEADS, HEAD_DIM, t_len), BF16),
                   jax.ShapeDtypeStruct((bn, IDX_HEADS, IDX_DIM, t_len), BF16),
                   jax.ShapeDtypeStruct((bn, IDX_HEADS, t_len), F32),
                   jax.ShapeDtypeStruct((bn, N_KV_HEADS, tpb, V_AUG, tm), BF16),
                   jax.ShapeDtypeStruct((n, KV_COLS), BF16),
                   jax.ShapeDtypeStruct((n, LANES), BF16)],
        compiler_params=_params(1),
        name="attn_pre",
    )(x2d, mod, g1.reshape(1, d), wta, wb, posr, posc, invft, invfr, gq_b, gk_r, lng_r, lnb_r)
    return outs


def _dsa_kernel(qT_ref, qiT_ref, wiT_ref, k2_ref, kiP_ref, vT_ref, o_ref,
                st_ref, qpad_ref, qipad_ref, m_ref, acc_ref, *, topk, rb):
    tq = o_ref.shape[1]
    kc = tq
    i = pl.program_id(1)
    n_c = i + 1
    q0 = i * tq

    qpad_ref[...] = jnp.zeros(qpad_ref.shape, BF16)
    qipad_ref[...] = jnp.zeros(qipad_ref.shape, BF16)
    for hh in range(N_HEADS):
        e = (hh // GROUP) % 2
        qpad_ref[hh, HEAD_DIM * e:HEAD_DIM * (e + 1), :] = qT_ref[0, hh]
    for hh in range(IDX_HEADS):
        qipad_ref[hh, 0:IDX_DIM, :] = qiT_ref[0, hh]

    tcol = q0 + lax.broadcasted_iota(I32, (kc, tq), 1)
    srow0 = lax.broadcasted_iota(I32, (kc, tq), 0)
    wi = wiT_ref[0]

    def score_body(c, carry):
        r0 = pl.multiple_of(c * kc, kc)
        ki = kiP_ref[0, pl.ds(r0, kc), :]
        acc = jnp.zeros((kc, tq), F32)
        for hh in range(IDX_HEADS):
            s = jnp.dot(ki, qipad_ref[hh], preferred_element_type=F32)
            acc = acc + jnp.maximum(s, 0.0) * wi[hh:hh + 1, :]
        causal = (srow0 + r0) <= tcol
        st_ref[pl.ds(r0, kc), :] = jnp.where(causal, acc, -jnp.inf)
        return carry

    lax.fori_loop(0, n_c, score_body, 0)

    n_rb = n_c * (kc // rb)

    def count(pred):
        def body(r, acc):
            r0 = pl.multiple_of(r * rb, rb)
            blk = st_ref[pl.ds(r0, rb), :].reshape(rb // SUBLANES, SUBLANES, tq)
            return acc + jnp.sum(pred(blk, r0).astype(I32), axis=0)
        acc = lax.fori_loop(0, n_rb, body, jnp.zeros((SUBLANES, tq), I32))
        return jnp.broadcast_to(jnp.sum(acc, axis=0, keepdims=True), (SUBLANES, tq))

    def key_to_float(u):
        ks = u ^ INT_MIN
        bits = jnp.where(ks < 0, ks ^ np.int32(0x7FFFFFFF), ks)
        return ks, lax.bitcast_convert_type(bits, F32)

    def bit_body(it, carry):
        tau_u, cnt_tau = carry
        cand_u = tau_u | lax.shift_left(np.int32(1), np.int32(31) - it)
        ks, cand_f = key_to_float(cand_u)
        cnt = count(lambda blk, r0: blk >= cand_f[None])
        take = (cnt >= topk) | (ks < KEY_NEG_INF)
        return jnp.where(take, cand_u, tau_u), jnp.where(take, cnt, cnt_tau)

    zero8 = jnp.zeros((SUBLANES, tq), I32)
    tau_u, cnt_ge = lax.fori_loop(0, 32, bit_body, (zero8, zero8))
    _, tau8 = key_to_float(tau_u)
    cnt_gt = count(lambda blk, r0: blk > tau8[None])
    need = topk - cnt_gt
    excess = jnp.max(((cnt_ge - cnt_gt) > need).astype(I32))

    t_total = st_ref.shape[0]
    n_bits = int(math.log2(t_total))
    sub_iota = lax.broadcasted_iota(I32, (rb // SUBLANES, SUBLANES, tq), 0) * SUBLANES + \
        lax.broadcasted_iota(I32, (rb // SUBLANES, SUBLANES, tq), 1)

    def tie_search():
        def jbit(it, j0):
            cand = j0 | lax.shift_left(np.int32(1), np.int32(n_bits - 1) - it)
            c = count(lambda blk, r0: (blk == tau8[None]) & ((sub_iota + r0) < cand[None]))
            return jnp.where(c < need, cand, j0)
        return lax.fori_loop(0, n_bits, jbit, zero8)

    jmax8 = lax.cond(excess > 0, tie_search, lambda: jnp.full((SUBLANES, tq), t_total, I32))
    tau1 = tau8[0:1]
    jmax1 = jmax8[0:1]

    m_ref[...] = jnp.full(m_ref.shape, -1e30, F32)
    acc_ref[...] = jnp.zeros(acc_ref.shape, F32)

    def attn_body(c, carry):
        r0 = pl.multiple_of(c * kc, kc)
        sc = st_ref[pl.ds(r0, kc), :]
        srow = srow0 + r0
        mask = ((sc > tau1) | ((sc == tau1) & (srow <= jmax1))) & (srow <= tcol)
        for g in range(N_KV_HEADS):
            j = g // 2
            kslab = k2_ref[0, pl.ds(r0, kc), LANES * j:LANES * (j + 1)]
            vt = vT_ref[0, g, c]
            for hl in range(GROUP):
                hh = g * GROUP + hl
                lg = jnp.dot(kslab, qpad_ref[hh], preferred_element_type=F32)
                lg = jnp.where(mask, lg, -jnp.inf)
                m_old = m_ref[hh:hh + 1, :]
                m_new = jnp.maximum(m_old, jnp.max(lg, axis=0, keepdims=True))
                alpha = jnp.exp(m_old - m_new)
                p = jnp.exp(lg - m_new).astype(BF16)
                acc_ref[hh] = acc_ref[hh] * alpha + jnp.dot(vt, p, preferred_element_type=F32)
                m_ref[hh:hh + 1, :] = m_new
        return carry

    lax.fori_loop(0, n_c, attn_body, 0)

    for hp in range(N_HEADS // 2):
        parts = []
        for hh in (2 * hp, 2 * hp + 1):
            a = acc_ref[hh]
            parts.append(a[0:HEAD_DIM] / a[HEAD_DIM:HEAD_DIM + 1])
        pair = jnp.concatenate(parts, axis=0)
        o_ref[0, :, LANES * hp:LANES * (hp + 1)] = pair.T.astype(BF16)


def _dsa_attention(qT, qiT, wiT, k2, kiP, vT, bn, t_len, topk):
    tq = topk
    rb = 64
    kern = functools.partial(_dsa_kernel, topk=topk, rb=rb)
    return pl.pallas_call(
        kern,
        grid=(bn, t_len // tq),
        in_specs=[pl.BlockSpec((1, N_HEADS, HEAD_DIM, tq), lambda b, i: (b, 0, 0, i)),
                  pl.BlockSpec((1, IDX_HEADS, IDX_DIM, tq), lambda b, i: (b, 0, 0, i)),
                  pl.BlockSpec((1, IDX_HEADS, tq), lambda b, i: (b, 0, i)),
                  pl.BlockSpec((1, t_len, KV_COLS), lambda b, i: (b, 0, 0), pipeline_mode=pl.Buffered(1)),
                  pl.BlockSpec((1, t_len, LANES), lambda b, i: (b, 0, 0), pipeline_mode=pl.Buffered(1)),
                  pl.BlockSpec((1, N_KV_HEADS, t_len // tq, V_AUG, tq), lambda b, i: (b, 0, 0, 0, 0),
                               pipeline_mode=pl.Buffered(1))],
        out_specs=pl.BlockSpec((1, tq, Q_COLS), lambda b, i: (b, i, 0)),
        out_shape=jax.ShapeDtypeStruct((bn, t_len, Q_COLS), BF16),
        scratch_shapes=[pltpu.VMEM((t_len, tq), F32),
                        pltpu.VMEM((N_HEADS, LANES, tq), BF16),
                        pltpu.VMEM((IDX_HEADS, LANES, tq), BF16),
                        pltpu.VMEM((N_HEADS, tq), F32),
                        pltpu.VMEM((N_HEADS, V_AUG, tq), F32)],
        compiler_params=_params(2),
        name="dsa_attention",
    )(qT, qiT, wiT, k2.reshape(bn, t_len, KV_COLS), kiP.reshape(bn, t_len, LANES), vT)


def _conv_pre_kernel(x_ref, mod_ref, g1_ref, w_ref, cw_ref, o_ref, zbuf_ref, *, tpb):
    tm, d = x_ref.shape
    i = pl.program_id(0)

    @pl.when(i % tpb == 0)
    def _():
        zbuf_ref[0:SUBLANES, :] = jnp.zeros((SUBLANES, d), F32)

    x = x_ref[...]
    sh = mod_ref[0, 0:1, :]
    sc = mod_ref[0, 1:2, :]
    ms = jnp.mean(x * x, axis=-1, keepdims=True)
    h = (x * lax.rsqrt(ms + NORM_EPS) * g1_ref[...]) * (1.0 + sc) + sh
    proj = jnp.dot(h.astype(BF16), w_ref[...], preferred_element_type=F32)
    b_gate = proj[:, 0:d]
    z = proj[:, d:2 * d] * proj[:, 2 * d:3 * d]
    zbuf_ref[SUBLANES:SUBLANES + tm, :] = z
    z1 = zbuf_ref[SUBLANES - 1:SUBLANES - 1 + tm, :]
    z2 = zbuf_ref[SUBLANES - 2:SUBLANES - 2 + tm, :]
    zc = cw_ref[0:1, :] * z2 + cw_ref[1:2, :] * z1 + cw_ref[2:3, :] * z
    o_ref[...] = (b_gate * zc).astype(BF16)
    zbuf_ref[0:SUBLANES, :] = z[tm - SUBLANES:tm, :]


def _conv_pre(x2d, mod, g1, w_in, conv_w, t_len, tm):
    n, d = x2d.shape
    tpb = t_len // tm
    row = lambda i: (i, 0)
    return pl.pallas_call(
        functools.partial(_conv_pre_kernel, tpb=tpb),
        grid=(n // tm,),
        in_specs=[pl.BlockSpec((tm, d), row),
                  pl.BlockSpec((1, 6, d), lambda i: (i // tpb, 0, 0)),
                  _const_spec((1, d)),
                  _const_spec((d, 3 * d)),
                  _const_spec((CONV_WIDTH, d))],
        out_specs=pl.BlockSpec((tm, d), row),
        out_shape=jax.ShapeDtypeStruct((n, d), BF16),
        scratch_shapes=[pltpu.VMEM((tm + SUBLANES, d), F32)],
        compiler_params=_params(1),
        name="conv_pre",
    )(x2d, mod, g1.reshape(1, d), w_in.astype(BF16), conv_w.astype(F32))


def _post_kernel(x_ref, mix_ref, mod_ref, g2_ref, wo_ref, wg_ref, wu_ref, wd_ref, o_ref, *, th):
    gate1 = mod_ref[0, 2:3, :]
    sh2 = mod_ref[0, 3:4, :]
    sc2 = mod_ref[0, 4:5, :]
    gate2 = mod_ref[0, 5:6, :]
    y = jnp.dot(mix_ref[...], wo_ref[...], preferred_element_type=F32)
    x1 = x_ref[...] + gate1 * y
    ms = jnp.mean(x1 * x1, axis=-1, keepdims=True)
    h = ((x1 * lax.rsqrt(ms + NORM_EPS) * g2_ref[...]) * (1.0 + sc2) + sh2).astype(BF16)
    hidden = wg_ref.shape[1]
    acc = jnp.zeros(x1.shape, F32)
    for j in range(hidden // th):
        gt = jnp.dot(h, wg_ref[:, th * j:th * (j + 1)], preferred_element_type=F32)
        up = jnp.dot(h, wu_ref[:, th * j:th * (j + 1)], preferred_element_type=F32)
        a = (gt * jax.nn.sigmoid(gt)) * up
        acc = acc + jnp.dot(a.astype(BF16), wd_ref[th * j:th * (j + 1), :], preferred_element_type=F32)
    o_ref[...] = x1 + gate2 * acc


def _post(x2d, mix, mod, g2, w_out, w_gate, w_up, w_down, t_len, tm):
    n, d = x2d.shape
    hidden = w_gate.shape[1]
    tpb = t_len // tm
    row = lambda i: (i, 0)
    return pl.pallas_call(
        functools.partial(_post_kernel, th=256),
        grid=(n // tm,),
        in_specs=[pl.BlockSpec((tm, d), row),
                  pl.BlockSpec((tm, d), row),
                  pl.BlockSpec((1, 6, d), lambda i: (i // tpb, 0, 0)),
                  _const_spec((1, d)),
                  _const_spec((d, d)),
                  _const_spec((d, hidden)),
                  _const_spec((d, hidden)),
                  _const_spec((hidden, d))],
        out_specs=pl.BlockSpec((tm, d), row),
        out_shape=jax.ShapeDtypeStruct((n, d), F32),
        compiler_params=_params(1),
        name="mixer_out_ffn",
    )(x2d, mix, mod, g2.reshape(1, d), w_out.astype(BF16), w_gate.astype(BF16), w_up.astype(BF16),
      w_down.astype(BF16))


def kernel(x, c, positions, ada_w, ada_b, norm1_g, norm2_g, attn_w_in, attn_q_norm_g, attn_k_norm_g,
           idx_k_ln_g, idx_k_ln_b, attn_w_out, conv_w_in, conv_w, conv_w_out, ffn_w_gate, ffn_w_up,
           ffn_w_down):
    bn, t_len, d = x.shape
    depth = ada_w.shape[0]
    topk = min(TOPK_MAX, t_len // 4)
    assert topk == TOPK_MAX and t_len % topk == 0 and (t_len & (t_len - 1)) == 0
    n = bn * t_len
    tm_pre = 256
    tm_post = 512
    mod = _modulation(c, ada_w, ada_b)
    x2d = x.reshape(n, d)
    for i in range(depth):
        j = i // 2
        if i % 2 == 0:
            qT, qiT, wiT, vT, k2, kiP = _attn_pre(
                x2d, mod[i], norm1_g[i], attn_w_in[j], attn_q_norm_g[j], attn_k_norm_g[j],
                idx_k_ln_g[j], idx_k_ln_b[j], positions, bn, t_len, tm_pre)
            mix = _dsa_attention(qT, qiT, wiT, k2, kiP, vT, bn, t_len, topk).reshape(n, d)
            w_mix_out = attn_w_out[j]
        else:
            mix = _conv_pre(x2d, mod[i], norm1_g[i], conv_w_in[j], conv_w[j], t_len, tm_post)
            w_mix_out = conv_w_out[j]
        x2d = _post(x2d, mix, mod[i], norm2_g[i], w_mix_out, ffn_w_gate[i], ffn_w_up[i], ffn_w_down[i],
                    t_len, tm_post)
    return x2d.reshape(bn, t_len, d)
```

```python
import functools
import math

import numpy as np
import jax
import jax.numpy as jnp
from jax import lax
from jax.experimental import pallas as pl
from jax.experimental.pallas import tpu as pltpu

F32 = jnp.float32
BF16 = jnp.bfloat16
I32 = jnp.int32

N_HEADS = 16
HEAD_DIM = 64
N_KV_HEADS = 4
GROUP = N_HEADS // N_KV_HEADS
IDX_HEADS = 8
IDX_DIM = 64
TOPK_MAX = 256
ROPE_THETA = 10000.0
CONV_WIDTH = 3
NORM_EPS = 1e-6

LANES = 128
SUBLANES = 8
VMEM_LIMIT = 56 * 1024 * 1024

Q_COLS = N_HEADS * HEAD_DIM
KV_COLS = N_KV_HEADS * HEAD_DIM
QI_COLS = IDX_HEADS * IDX_DIM
V_AUG = HEAD_DIM + 16

INT_MIN = np.int32(-2**31)
KEY_NEG_INF = np.int32(np.uint32(0x807FFFFF).astype(np.int64) - 2**32)


def _const_spec(shape):
    nd = len(shape)
    return pl.BlockSpec(shape, lambda *_: (0,) * nd, pipeline_mode=pl.Buffered(1))


def _params(n_axes):
    return pltpu.CompilerParams(dimension_semantics=("arbitrary",) * n_axes,
                                vmem_limit_bytes=VMEM_LIMIT)


def _mod_kernel(c_ref, w_ref, b_ref, o_ref):
    c = c_ref[...]
    ca = c * jax.nn.sigmoid(c)
    o_ref[0] = jnp.dot(ca.astype(BF16), w_ref[0].astype(BF16), preferred_element_type=F32) + b_ref[0]


def _modulation(c, ada_w, ada_b):
    depth, d, six_d = ada_w.shape
    bn = c.shape[0]
    rows = 16
    cp = jnp.zeros((rows, d), F32).at[:bn].set(c)
    tn = 1536
    out = pl.pallas_call(
        _mod_kernel,
        grid=(depth, six_d // tn),
        in_specs=[pl.BlockSpec((rows, d), lambda i, j: (0, 0)),
                  pl.BlockSpec((1, d, tn), lambda i, j: (i, 0, j)),
                  pl.BlockSpec((1, 1, tn), lambda i, j: (i, 0, j))],
        out_specs=pl.BlockSpec((1, rows, tn), lambda i, j: (i, 0, j)),
        out_shape=jax.ShapeDtypeStruct((depth, rows, six_d), F32),
        compiler_params=_params(2),
        name="adaln_mod",
    )(cp, ada_w, ada_b.reshape(depth, 1, six_d))
    return out[:, :bn].reshape(depth, bn, 6, d)


def _segmean64(v, gmat):
    hi = v.astype(BF16)
    lo = (v - hi.astype(F32)).astype(BF16)
    s = jnp.dot(hi, gmat, preferred_element_type=F32) + jnp.dot(lo, gmat, preferred_element_type=F32)
    return s * (1.0 / HEAD_DIM)


def _attn_pre_kernel(x_ref, mod_ref, g1_ref, wta_ref, wb_ref, posr_ref, posc_ref, invft_ref, invfr_ref,
                     gq_ref, gk_ref, lng_ref, lnb_ref,
                     qT_ref, qiT_ref, wiT_ref, vT_ref, k4_ref, ki_ref):
    tm = x_ref.shape[0]
    x = x_ref[...]
    sh = mod_ref[0, 0:1, :]
    sc = mod_ref[0, 1:2, :]
    ms = jnp.mean(x * x, axis=-1, keepdims=True)
    h = (x * lax.rsqrt(ms + NORM_EPS) * g1_ref[...]) * (1.0 + sc) + sh
    hb = h.astype(BF16)

    pT = lax.dot_general(wta_ref[...], hb, (((1,), (1,)), ((), ())), preferred_element_type=F32)
    angT = invft_ref[...] * posr_ref[0].astype(F32)
    cT = jnp.cos(angT)
    sT = jnp.sin(angT)
    half = HEAD_DIM // 2

    def rope_t(y):
        x1 = y[:half]
        x2 = y[half:]
        return jnp.concatenate([x1 * cT - x2 * sT, x2 * cT + x1 * sT], axis=0)

    gq = gq_ref[...]
    for hh in range(N_HEADS):
        xq = pT[HEAD_DIM * hh:HEAD_DIM * (hh + 1)]
        inv = lax.rsqrt(jnp.mean(xq * xq, axis=0, keepdims=True) + NORM_EPS)
        y = xq * inv * gq
        qT_ref[0, hh] = (rope_t(y) * (HEAD_DIM ** -0.5)).astype(BF16)
    o0 = Q_COLS
    ones = jnp.ones((V_AUG - HEAD_DIM, tm), BF16)
    for g in range(N_KV_HEADS):
        vT_ref[0, g, 0, 0:HEAD_DIM, :] = pT[o0 + HEAD_DIM * g:o0 + HEAD_DIM * (g + 1)].astype(BF16)
        vT_ref[0, g, 0, HEAD_DIM:V_AUG, :] = ones
    o0 += KV_COLS
    for hh in range(IDX_HEADS):
        qiT_ref[0, hh] = rope_t(pT[o0 + IDX_DIM * hh:o0 + IDX_DIM * (hh + 1)]).astype(BF16)
    o0 += QI_COLS
    wiT_ref[0] = pT[o0:o0 + IDX_HEADS] * (IDX_HEADS ** -0.5 * IDX_DIM ** -0.5)

    pk = jnp.dot(hb, wb_ref[...], preferred_element_type=F32)
    ang = posc_ref[...].astype(F32) * invfr_ref[...]
    cR = jnp.cos(ang)
    sR = jnp.sin(ang)
    lane = lax.broadcasted_iota(I32, (tm, LANES), 1)
    first = (lane % HEAD_DIM) < half
    s_signed = jnp.where(first, -sR, sR)
    ri = lax.broadcasted_iota(I32, (LANES, LANES), 0) // HEAD_DIM
    ci = lax.broadcasted_iota(I32, (LANES, LANES), 1) // HEAD_DIM
    gmat = jnp.where(ri == ci, 1.0, 0.0).astype(BF16)

    def rope_r(y):
        rot = jnp.where(first, pltpu.roll(y, LANES - half, 1), pltpu.roll(y, half, 1))
        return y * cR + rot * s_signed

    for j in range(KV_COLS // LANES):
        xs = pk[:, LANES * j:LANES * (j + 1)]
        inv = lax.rsqrt(_segmean64(xs * xs, gmat) + NORM_EPS)
        kr = rope_r(xs * inv * gk_ref[...])
        k4_ref[0, 2 * j] = kr[:, 0:HEAD_DIM].astype(BF16)
        k4_ref[0, 2 * j + 1] = kr[:, HEAD_DIM:LANES].astype(BF16)
    t = pk[:, KV_COLS:KV_COLS + LANES]
    mu = _segmean64(t, gmat)
    dlt = t - mu
    var = _segmean64(dlt * dlt, gmat)
    kin = dlt * lax.rsqrt(var + NORM_EPS) * lng_ref[...] + lnb_ref[...]
    ki_ref[...] = rope_r(kin)[:, 0:IDX_DIM].astype(BF16)


def _attn_pre(x2d, mod, g1, w_in, gq, gk, lng, lnb, positions, bn, t_len, tm):
    n, d = x2d.shape
    tpb = t_len // tm
    half = HEAD_DIM // 2
    wq = w_in[:, :Q_COLS]
    wk = w_in[:, Q_COLS:Q_COLS + KV_COLS]
    wv = w_in[:, Q_COLS + KV_COLS:Q_COLS + 2 * KV_COLS]
    o0 = Q_COLS + 2 * KV_COLS
    wqi = w_in[:, o0:o0 + QI_COLS]
    wki = w_in[:, o0 + QI_COLS:o0 + QI_COLS + IDX_DIM]
    wwi = w_in[:, o0 + QI_COLS + IDX_DIM:]
    ma = Q_COLS + KV_COLS + QI_COLS + IDX_HEADS
    ma_pad = -(-ma // 16) * 16
    wta = jnp.concatenate([wq, wv, wqi, wwi, jnp.zeros((d, ma_pad - ma), F32)], axis=1).T.astype(BF16)
    wb = jnp.concatenate([wk, wki, jnp.zeros((d, LANES - IDX_DIM), F32)], axis=1).astype(BF16)
    nb = wb.shape[1]
    inv_freq = ROPE_THETA ** (-(jnp.arange(half, dtype=F32) * 2.0 / HEAD_DIM))
    invft = jnp.broadcast_to(inv_freq[:, None], (half, tm))
    invfr = jnp.tile(inv_freq, LANES // half).reshape(1, LANES)
    gq_b = jnp.broadcast_to(gq.astype(F32)[:, None], (HEAD_DIM, tm))
    gk_r = jnp.tile(gk.astype(F32), LANES // HEAD_DIM).reshape(1, LANES)
    zpad = jnp.zeros((LANES - IDX_DIM,), F32)
    lng_r = jnp.concatenate([lng.astype(F32), zpad]).reshape(1, LANES)
    lnb_r = jnp.concatenate([lnb.astype(F32), zpad]).reshape(1, LANES)
    posr = positions.reshape(bn, 1, t_len)
    posc = positions.reshape(n, 1)

    row = lambda i: (i, 0)
    outs = pl.pallas_call(
        _attn_pre_kernel,
        grid=(n // tm,),
        in_specs=[pl.BlockSpec((tm, d), row),
                  pl.BlockSpec((1, 6, d), lambda i: (i // tpb, 0, 0)),
                  _const_spec((1, d)),
                  _const_spec((ma_pad, d)),
                  _const_spec((d, nb)),
                  pl.BlockSpec((1, 1, tm), lambda i: (i // tpb, 0, i % tpb)),
                  pl.BlockSpec((tm, 1), row),
                  _const_spec((half, tm)),
                  _const_spec((1, LANES)),
                  _const_spec((HEAD_DIM, tm)),
                  _const_spec((1, LANES)),
                  _const_spec((1, LANES)),
                  _const_spec((1, LANES))],
        out_specs=[pl.BlockSpec((1, N_HEADS, HEAD_DIM, tm), lambda i: (i // tpb, 0, 0, i % tpb)),
                   pl.BlockSpec((1, IDX_HEADS, IDX_DIM, tm), lambda i: (i // tpb, 0, 0, i % tpb)),
                   pl.BlockSpec((1, IDX_HEADS, tm), lambda i: (i // tpb, 0, i % tpb)),
                   pl.BlockSpec((1, N_KV_HEADS, 1, V_AUG, tm), lambda i: (i // tpb, 0, i % tpb, 0, 0)),
                   pl.BlockSpec((1, N_KV_HEADS, tm, HEAD_DIM), lambda i: (i // tpb, 0, i % tpb, 0)),
                   pl.BlockSpec((tm, IDX_DIM), row)],
        out_shape=[jax.ShapeDtypeStruct((bn, N_HEADS, HEAD_DIM, t_len), BF16),
                   jax.ShapeDtypeStruct((bn, IDX_HEADS, IDX_DIM, t_len), BF16),
                   jax.ShapeDtypeStruct((bn, IDX_HEADS, t_len), F32),
                   jax.ShapeDtypeStruct((bn, N_KV_HEADS, tpb, V_AUG, tm), BF16),
                   jax.ShapeDtypeStruct((bn, N_KV_HEADS, t_len, HEAD_DIM), BF16),
                   jax.ShapeDtypeStruct((n, IDX_DIM), BF16)],
        compiler_params=_params(1),
        name="attn_pre",
    )(x2d, mod, g1.reshape(1, d), wta, wb, posr, posc, invft, invfr, gq_b, gk_r, lng_r, lnb_r)
    return outs


def _dsa_kernel(qT_ref, qiT_ref, wiT_ref, k4_ref, ki_ref, vT_ref, o_ref,
                st_ref, m_ref, al_ref, acc_ref, bias_ref, lg_ref, *, topk, rb):
    tq = o_ref.shape[1]
    kc = tq
    i = pl.program_id(1)
    n_c = i + 1
    q0 = i * tq

    tcol = q0 + lax.broadcasted_iota(I32, (kc, tq), 1)
    srow0 = lax.broadcasted_iota(I32, (kc, tq), 0)
    wi = wiT_ref[0]

    def score_body(c, carry):
        r0 = pl.multiple_of(c * kc, kc)
        ki = ki_ref[0, pl.ds(r0, kc), :]
        acc = jnp.zeros((kc, tq), F32)
        for hh in range(IDX_HEADS):
            s = jnp.dot(ki, qiT_ref[0, hh], preferred_element_type=F32)
            acc = acc + jnp.maximum(s, 0.0) * wi[hh:hh + 1, :]
        causal = (srow0 + r0) <= tcol
        st_ref[pl.ds(r0, kc), :] = jnp.where(causal, acc, -jnp.inf)
        return carry

    lax.fori_loop(0, n_c, score_body, 0)

    n_rb = n_c * (kc // rb)
    n_par = 4

    def count(pred):
        def body(r, acc):
            r0 = pl.multiple_of(r * rb, rb)
            blk = st_ref[pl.ds(r0, rb), :].reshape(rb // SUBLANES, SUBLANES, tq)
            hit = pred(blk, r0).astype(I32).reshape(rb // (SUBLANES * n_par), n_par, SUBLANES, tq)
            return acc + jnp.sum(hit, axis=0)
        acc = lax.fori_loop(0, n_rb, body, jnp.zeros((n_par, SUBLANES, tq), I32))
        tot = jnp.sum(jnp.sum(acc, axis=0), axis=0, keepdims=True)
        return jnp.broadcast_to(tot, (SUBLANES, tq))

    def key_to_float(u):
        ks = u ^ INT_MIN
        bits = jnp.where(ks < 0, ks ^ np.int32(0x7FFFFFFF), ks)
        return ks, lax.bitcast_convert_type(bits, F32)

    def bit_cond(carry):
        it, _, cnt_tau = carry
        return (it < 32) & (jnp.min((cnt_tau == topk).astype(I32)) == 0)

    def bit_body(carry):
        it, tau_u, cnt_tau = carry
        cand_u = tau_u | lax.shift_left(np.int32(1), np.int32(31) - it)
        ks, cand_f = key_to_float(cand_u)
        cnt = count(lambda blk, r0: blk >= cand_f[None])
        take = ((cnt >= topk) | (ks < KEY_NEG_INF)) & (cnt_tau != topk)
        return it + 1, jnp.where(take, cand_u, tau_u), jnp.where(take, cnt, cnt_tau)

    zero8 = jnp.zeros((SUBLANES, tq), I32)
    _, tau_u, cnt_ge = lax.while_loop(bit_cond, bit_body, (np.int32(0), zero8, zero8))
    _, tau8 = key_to_float(tau_u)
    cnt_gt = count(lambda blk, r0: blk > tau8[None])
    need = topk - cnt_gt
    excess = jnp.max(((cnt_ge - cnt_gt) > need).astype(I32))

    t_total = st_ref.shape[0]
    n_bits = int(math.log2(t_total))
    sub_iota = lax.broadcasted_iota(I32, (rb // SUBLANES, SUBLANES, tq), 0) * SUBLANES + \
        lax.broadcasted_iota(I32, (rb // SUBLANES, SUBLANES, tq), 1)

    def tie_search():
        def jbit(it, j0):
            cand = j0 | lax.shift_left(np.int32(1), np.int32(n_bits - 1) - it)
            c = count(lambda blk, r0: (blk == tau8[None]) & ((sub_iota + r0) < cand[None]))
            return jnp.where(c < need, cand, j0)
        return lax.fori_loop(0, n_bits, jbit, zero8)

    jmax8 = lax.cond(excess > 0, tie_search, lambda: jnp.full((SUBLANES, tq), t_total, I32))
    tau1 = tau8[0:1]
    jmax1 = jmax8[0:1]

    m_ref[...] = jnp.full(m_ref.shape, -1e30, F32)
    acc_ref[...] = jnp.zeros(acc_ref.shape, F32)

    def set_bias(c):
        r0 = pl.multiple_of(c * kc, kc)
        sc = st_ref[pl.ds(r0, kc), :]
        srow = srow0 + r0
        mask = ((sc > tau1) | ((sc == tau1) & (srow <= jmax1))) & (srow <= tcol)
        bias_ref[...] = jnp.where(mask, 0.0, -jnp.inf)

    def logits_stage(c, hh):
        r0 = pl.multiple_of(c * kc, kc)
        kg = k4_ref[0, hh // GROUP, pl.ds(r0, kc), :]
        lg = jnp.dot(kg, qT_ref[0, hh], preferred_element_type=F32) + bias_ref[...]
        lg_ref[hh] = lg
        mx = jnp.max(lg.reshape(kc // SUBLANES, SUBLANES, tq), axis=0)
        m_old = m_ref[hh:hh + 1, :]
        m_new = jnp.maximum(m_old, jnp.max(mx, axis=0, keepdims=True))
        m_ref[hh:hh + 1, :] = m_new
        al_ref[hh:hh + 1, :] = jnp.exp(m_old - m_new)

    def value_stage(c, hh):
        vt = vT_ref[0, hh // GROUP, c]
        p = jnp.exp(lg_ref[hh] - m_ref[hh:hh + 1, :]).astype(BF16)
        acc_ref[hh] = acc_ref[hh] * al_ref[hh:hh + 1, :] + jnp.dot(vt, p, preferred_element_type=F32)

    set_bias(0)
    for hh in range(N_HEADS):
        logits_stage(0, hh)

    def attn_body(c, carry):
        set_bias(c)
        for hh in range(N_HEADS):
            value_stage(c - 1, hh)
            logits_stage(c, hh)
        return carry

    lax.fori_loop(1, n_c, attn_body, 0)
    for hh in range(N_HEADS):
        value_stage(n_c - 1, hh)

    for hp in range(N_HEADS // 2):
        parts = []
        for hh in (2 * hp, 2 * hp + 1):
            a = acc_ref[hh]
            parts.append(a[0:HEAD_DIM] / a[HEAD_DIM:HEAD_DIM + 1])
        pair = jnp.concatenate(parts, axis=0)
        o_ref[0, :, LANES * hp:LANES * (hp + 1)] = pair.T.astype(BF16)


def _dsa_attention(qT, qiT, wiT, k4, ki, vT, bn, t_len, topk):
    tq = topk
    rb = 256
    kern = functools.partial(_dsa_kernel, topk=topk, rb=rb)
    return pl.pallas_call(
        kern,
        grid=(bn, t_len // tq),
        in_specs=[pl.BlockSpec((1, N_HEADS, HEAD_DIM, tq), lambda b, i: (b, 0, 0, i)),
                  pl.BlockSpec((1, IDX_HEADS, IDX_DIM, tq), lambda b, i: (b, 0, 0, i)),
                  pl.BlockSpec((1, IDX_HEADS, tq), lambda b, i: (b, 0, i)),
                  pl.BlockSpec((1, N_KV_HEADS, t_len, HEAD_DIM), lambda b, i: (b, 0, 0, 0),
                               pipeline_mode=pl.Buffered(1)),
                  pl.BlockSpec((1, t_len, IDX_DIM), lambda b, i: (b, 0, 0), pipeline_mode=pl.Buffered(1)),
                  pl.BlockSpec((1, N_KV_HEADS, t_len // tq, V_AUG, tq), lambda b, i: (b, 0, 0, 0, 0),
                               pipeline_mode=pl.Buffered(1))],
        out_specs=pl.BlockSpec((1, tq, Q_COLS), lambda b, i: (b, i, 0)),
        out_shape=jax.ShapeDtypeStruct((bn, t_len, Q_COLS), BF16),
        scratch_shapes=[pltpu.VMEM((t_len, tq), F32),
                        pltpu.VMEM((N_HEADS, tq), F32),
                        pltpu.VMEM((N_HEADS, tq), F32),
                        pltpu.VMEM((N_HEADS, V_AUG, tq), F32),
                        pltpu.VMEM((tq, tq), F32),
                        pltpu.VMEM((N_HEADS, tq, tq), F32)],
        compiler_params=_params(2),
        name="dsa_attention",
    )(qT, qiT, wiT, k4, ki.reshape(bn, t_len, IDX_DIM), vT)


def _conv_pre_kernel(x_ref, mod_ref, g1_ref, w_ref, cw_ref, o_ref, zbuf_ref, *, tpb):
    tm, d = x_ref.shape
    i = pl.program_id(0)

    @pl.when(i % tpb == 0)
    def _():
        zbuf_ref[0:SUBLANES, :] = jnp.zeros((SUBLANES, d), F32)

    x = x_ref[...]
    sh = mod_ref[0, 0:1, :]
    sc = mod_ref[0, 1:2, :]
    ms = jnp.mean(x * x, axis=-1, keepdims=True)
    h = (x * lax.rsqrt(ms + NORM_EPS) * g1_ref[...]) * (1.0 + sc) + sh
    proj = jnp.dot(h.astype(BF16), w_ref[...], preferred_element_type=F32)
    b_gate = proj[:, 0:d]
    z = proj[:, d:2 * d] * proj[:, 2 * d:3 * d]
    zbuf_ref[SUBLANES:SUBLANES + tm, :] = z
    z1 = zbuf_ref[SUBLANES - 1:SUBLANES - 1 + tm, :]
    z2 = zbuf_ref[SUBLANES - 2:SUBLANES - 2 + tm, :]
    zc = cw_ref[0:1, :] * z2 + cw_ref[1:2, :] * z1 + cw_ref[2:3, :] * z
    o_ref[...] = (b_gate * zc).astype(BF16)
    zbuf_ref[0:SUBLANES, :] = z[tm - SUBLANES:tm, :]


def _conv_pre(x2d, mod, g1, w_in, conv_w, t_len, tm):
    n, d = x2d.shape
    tpb = t_len // tm
    row = lambda i: (i, 0)
    return pl.pallas_call(
        functools.partial(_conv_pre_kernel, tpb=tpb),
        grid=(n // tm,),
        in_specs=[pl.BlockSpec((tm, d), row),
                  pl.BlockSpec((1, 6, d), lambda i: (i // tpb, 0, 0)),
                  _const_spec((1, d)),
                  _const_spec((d, 3 * d)),
                  _const_spec((CONV_WIDTH, d))],
        out_specs=pl.BlockSpec((tm, d), row),
        out_shape=jax.ShapeDtypeStruct((n, d), BF16),
        scratch_shapes=[pltpu.VMEM((tm + SUBLANES, d), F32)],
        compiler_params=_params(1),
        name="conv_pre",
    )(x2d, mod, g1.reshape(1, d), w_in.astype(BF16), conv_w.astype(F32))


def _post_kernel(x_ref, mix_ref, mod_ref, g2_ref, wo_ref, wg_ref, wu_ref, wd_ref, o_ref, *, th):
    gate1 = mod_ref[0, 2:3, :]
    sh2 = mod_ref[0, 3:4, :]
    sc2 = mod_ref[0, 4:5, :]
    gate2 = mod_ref[0, 5:6, :]
    y = jnp.dot(mix_ref[...], wo_ref[...], preferred_element_type=F32)
    x1 = x_ref[...] + gate1 * y
    ms = jnp.mean(x1 * x1, axis=-1, keepdims=True)
    h = ((x1 * lax.rsqrt(ms + NORM_EPS) * g2_ref[...]) * (1.0 + sc2) + sh2).astype(BF16)
    hidden = wg_ref.shape[1]
    acc = jnp.zeros(x1.shape, F32)
    for j in range(hidden // th):
        gt = jnp.dot(h, wg_ref[:, th * j:th * (j + 1)], preferred_element_type=F32)
        up = jnp.dot(h, wu_ref[:, th * j:th * (j + 1)], preferred_element_type=F32)
        a = (gt * jax.nn.sigmoid(gt)) * up
        acc = acc + jnp.dot(a.astype(BF16), wd_ref[th * j:th * (j + 1), :], preferred_element_type=F32)
    o_ref[...] = x1 + gate2 * acc


def _post(x2d, mix, mod, g2, w_out, w_gate, w_up, w_down, t_len, tm):
    n, d = x2d.shape
    hidden = w_gate.shape[1]
    tpb = t_len // tm
    row = lambda i: (i, 0)
    return pl.pallas_call(
        functools.partial(_post_kernel, th=256),
        grid=(n // tm,),
        in_specs=[pl.BlockSpec((tm, d), row),
                  pl.BlockSpec((tm, d), row),
                  pl.BlockSpec((1, 6, d), lambda i: (i // tpb, 0, 0)),
                  _const_spec((1, d)),
                  _const_spec((d, d)),
                  _const_spec((d, hidden)),
                  _const_spec((d, hidden)),
                  _const_spec((hidden, d))],
        out_specs=pl.BlockSpec((tm, d), row),
        out_shape=jax.ShapeDtypeStruct((n, d), F32),
        compiler_params=_params(1),
        name="mixer_out_ffn",
    )(x2d, mix, mod, g2.reshape(1, d), w_out.astype(BF16), w_gate.astype(BF16), w_up.astype(BF16),
      w_down.astype(BF16))


def kernel(x, c, positions, ada_w, ada_b, norm1_g, norm2_g, attn_w_in, attn_q_norm_g, attn_k_norm_g,
           idx_k_ln_g, idx_k_ln_b, attn_w_out, conv_w_in, conv_w, conv_w_out, ffn_w_gate, ffn_w_up,
           ffn_w_down):
    bn, t_len, d = x.shape
    depth = ada_w.shape[0]
    topk = min(TOPK_MAX, t_len // 4)
    assert topk == TOPK_MAX and t_len % topk == 0 and (t_len & (t_len - 1)) == 0
    n = bn * t_len
    tm_pre = 256
    tm_post = 512
    mod = _modulation(c, ada_w, ada_b)
    x2d = x.reshape(n, d)
    for i in range(depth):
        j = i // 2
        if i % 2 == 0:
            qT, qiT, wiT, vT, k2, kiP = _attn_pre(
                x2d, mod[i], norm1_g[i], attn_w_in[j], attn_q_norm_g[j], attn_k_norm_g[j],
                idx_k_ln_g[j], idx_k_ln_b[j], positions, bn, t_len, tm_pre)
            mix = _dsa_attention(qT, qiT, wiT, k2, kiP, vT, bn, t_len, topk).reshape(n, d)
            w_mix_out = attn_w_out[j]
        else:
            mix = _conv_pre(x2d, mod[i], norm1_g[i], conv_w_in[j], conv_w[j], t_len, tm_post)
            w_mix_out = conv_w_out[j]
        x2d = _post(x2d, mix, mod[i], norm2_g[i], w_mix_out, ffn_w_gate[i], ffn_w_up[i], ffn_w_down[i],
                    t_len, tm_post)
    return x2d.reshape(bn, t_len, d)
```

```python
import functools
import math

import numpy as np
import jax
import jax.numpy as jnp
from jax import lax
from jax.experimental import pallas as pl
from jax.experimental.pallas import tpu as pltpu

F32 = jnp.float32
BF16 = jnp.bfloat16
I32 = jnp.int32

N_HEADS = 16
HEAD_DIM = 64
N_KV_HEADS = 4
GROUP = N_HEADS // N_KV_HEADS
IDX_HEADS = 8
IDX_DIM = 64
TOPK_MAX = 256
ROPE_THETA = 10000.0
CONV_WIDTH = 3
NORM_EPS = 1e-6
LOG2E = math.log2(math.e)

LANES = 128
SUBLANES = 8
VMEM_LIMIT = 56 * 1024 * 1024

Q_COLS = N_HEADS * HEAD_DIM
KV_COLS = N_KV_HEADS * HEAD_DIM
QI_COLS = IDX_HEADS * IDX_DIM
V_AUG = HEAD_DIM + 16

INT_MIN = np.int32(-2**31)
KEY_NEG_INF = np.int32(np.uint32(0x807FFFFF).astype(np.int64) - 2**32)


def _const_spec(shape):
    nd = len(shape)
    return pl.BlockSpec(shape, lambda *_: (0,) * nd, pipeline_mode=pl.Buffered(1))


def _params(n_axes):
    return pltpu.CompilerParams(dimension_semantics=("arbitrary",) * n_axes,
                                vmem_limit_bytes=VMEM_LIMIT)


def _mod_kernel(c_ref, w_ref, b_ref, o_ref):
    c = c_ref[...]
    ca = c * jax.nn.sigmoid(c)
    o_ref[0] = jnp.dot(ca.astype(BF16), w_ref[0].astype(BF16), preferred_element_type=F32) + b_ref[0]


def _modulation(c, ada_w, ada_b):
    depth, d, six_d = ada_w.shape
    bn = c.shape[0]
    rows = 16
    cp = jnp.zeros((rows, d), F32).at[:bn].set(c)
    tn = 1536
    out = pl.pallas_call(
        _mod_kernel,
        grid=(depth, six_d // tn),
        in_specs=[pl.BlockSpec((rows, d), lambda i, j: (0, 0)),
                  pl.BlockSpec((1, d, tn), lambda i, j: (i, 0, j)),
                  pl.BlockSpec((1, 1, tn), lambda i, j: (i, 0, j))],
        out_specs=pl.BlockSpec((1, rows, tn), lambda i, j: (i, 0, j)),
        out_shape=jax.ShapeDtypeStruct((depth, rows, six_d), F32),
        compiler_params=_params(2),
        name="adaln_mod",
    )(cp, ada_w, ada_b.reshape(depth, 1, six_d))
    return out[:, :bn].reshape(depth, bn, 6, d)


def _segmean64(v, gmat):
    hi = v.astype(BF16)
    lo = (v - hi.astype(F32)).astype(BF16)
    s = jnp.dot(hi, gmat, preferred_element_type=F32) + jnp.dot(lo, gmat, preferred_element_type=F32)
    return s * (1.0 / HEAD_DIM)


def _attn_pre_kernel(x_ref, mod_ref, g1_ref, wta_ref, wb_ref, posr_ref, posc_ref, invft_ref, invfr_ref,
                     gq_ref, gk_ref, lng_ref, lnb_ref,
                     qT_ref, qiT_ref, wiT_ref, vT_ref, k4_ref, ki_ref):
    tm = x_ref.shape[0]
    x = x_ref[...]
    sh = mod_ref[0, 0:1, :]
    sc = mod_ref[0, 1:2, :]
    ms = jnp.mean(x * x, axis=-1, keepdims=True)
    h = (x * lax.rsqrt(ms + NORM_EPS) * g1_ref[...]) * (1.0 + sc) + sh
    hb = h.astype(BF16)

    pT = lax.dot_general(wta_ref[...], hb, (((1,), (1,)), ((), ())), preferred_element_type=F32)
    angT = invft_ref[...] * posr_ref[0].astype(F32)
    cT = jnp.cos(angT)
    sT = jnp.sin(angT)
    half = HEAD_DIM // 2

    def rope_t(y):
        x1 = y[:half]
        x2 = y[half:]
        return jnp.concatenate([x1 * cT - x2 * sT, x2 * cT + x1 * sT], axis=0)

    gq = gq_ref[...]
    for hh in range(N_HEADS):
        xq = pT[HEAD_DIM * hh:HEAD_DIM * (hh + 1)]
        inv = lax.rsqrt(jnp.mean(xq * xq, axis=0, keepdims=True) + NORM_EPS)
        y = xq * inv * gq
        qT_ref[0, hh] = (rope_t(y) * (HEAD_DIM ** -0.5 * LOG2E)).astype(BF16)
    o0 = Q_COLS
    ones = jnp.ones((V_AUG - HEAD_DIM, tm), BF16)
    for g in range(N_KV_HEADS):
        vT_ref[0, g, 0, 0:HEAD_DIM, :] = pT[o0 + HEAD_DIM * g:o0 + HEAD_DIM * (g + 1)].astype(BF16)
        vT_ref[0, g, 0, HEAD_DIM:V_AUG, :] = ones
    o0 += KV_COLS
    for hh in range(IDX_HEADS):
        qiT_ref[0, hh] = rope_t(pT[o0 + IDX_DIM * hh:o0 + IDX_DIM * (hh + 1)]).astype(BF16)
    o0 += QI_COLS
    wiT_ref[0] = pT[o0:o0 + IDX_HEADS] * (IDX_HEADS ** -0.5 * IDX_DIM ** -0.5)

    pk = jnp.dot(hb, wb_ref[...], preferred_element_type=F32)
    ang = posc_ref[...].astype(F32) * invfr_ref[...]
    cR = jnp.cos(ang)
    sR = jnp.sin(ang)
    lane = lax.broadcasted_iota(I32, (tm, LANES), 1)
    first = (lane % HEAD_DIM) < half
    s_signed = jnp.where(first, -sR, sR)
    ri = lax.broadcasted_iota(I32, (LANES, LANES), 0) // HEAD_DIM
    ci = lax.broadcasted_iota(I32, (LANES, LANES), 1) // HEAD_DIM
    gmat = jnp.where(ri == ci, 1.0, 0.0).astype(BF16)

    def rope_r(y):
        rot = jnp.where(first, pltpu.roll(y, LANES - half, 1), pltpu.roll(y, half, 1))
        return y * cR + rot * s_signed

    for j in range(KV_COLS // LANES):
        xs = pk[:, LANES * j:LANES * (j + 1)]
        inv = lax.rsqrt(_segmean64(xs * xs, gmat) + NORM_EPS)
        kr = rope_r(xs * inv * gk_ref[...])
        k4_ref[0, 2 * j] = kr[:, 0:HEAD_DIM].astype(BF16)
        k4_ref[0, 2 * j + 1] = kr[:, HEAD_DIM:LANES].astype(BF16)
    t = pk[:, KV_COLS:KV_COLS + LANES]
    mu = _segmean64(t, gmat)
    dlt = t - mu
    var = _segmean64(dlt * dlt, gmat)
    kin = dlt * lax.rsqrt(var + NORM_EPS) * lng_ref[...] + lnb_ref[...]
    ki_ref[...] = rope_r(kin)[:, 0:IDX_DIM].astype(BF16)


def _attn_pre(x2d, mod, g1, w_in, gq, gk, lng, lnb, positions, bn, t_len, tm):
    n, d = x2d.shape
    tpb = t_len // tm
    half = HEAD_DIM // 2
    wq = w_in[:, :Q_COLS]
    wk = w_in[:, Q_COLS:Q_COLS + KV_COLS]
    wv = w_in[:, Q_COLS + KV_COLS:Q_COLS + 2 * KV_COLS]
    o0 = Q_COLS + 2 * KV_COLS
    wqi = w_in[:, o0:o0 + QI_COLS]
    wki = w_in[:, o0 + QI_COLS:o0 + QI_COLS + IDX_DIM]
    wwi = w_in[:, o0 + QI_COLS + IDX_DIM:]
    ma = Q_COLS + KV_COLS + QI_COLS + IDX_HEADS
    ma_pad = -(-ma // 16) * 16
    wta = jnp.concatenate([wq, wv, wqi, wwi, jnp.zeros((d, ma_pad - ma), F32)], axis=1).T.astype(BF16)
    wb = jnp.concatenate([wk, wki, jnp.zeros((d, LANES - IDX_DIM), F32)], axis=1).astype(BF16)
    nb = wb.shape[1]
    inv_freq = ROPE_THETA ** (-(jnp.arange(half, dtype=F32) * 2.0 / HEAD_DIM))
    invft = jnp.broadcast_to(inv_freq[:, None], (half, tm))
    invfr = jnp.tile(inv_freq, LANES // half).reshape(1, LANES)
    gq_b = jnp.broadcast_to(gq.astype(F32)[:, None], (HEAD_DIM, tm))
    gk_r = jnp.tile(gk.astype(F32), LANES // HEAD_DIM).reshape(1, LANES)
    zpad = jnp.zeros((LANES - IDX_DIM,), F32)
    lng_r = jnp.concatenate([lng.astype(F32), zpad]).reshape(1, LANES)
    lnb_r = jnp.concatenate([lnb.astype(F32), zpad]).reshape(1, LANES)
    posr = positions.reshape(bn, 1, t_len)
    posc = positions.reshape(n, 1)

    row = lambda i: (i, 0)
    outs = pl.pallas_call(
        _attn_pre_kernel,
        grid=(n // tm,),
        in_specs=[pl.BlockSpec((tm, d), row),
                  pl.BlockSpec((1, 6, d), lambda i: (i // tpb, 0, 0)),
                  _const_spec((1, d)),
                  _const_spec((ma_pad, d)),
                  _const_spec((d, nb)),
                  pl.BlockSpec((1, 1, tm), lambda i: (i // tpb, 0, i % tpb)),
                  pl.BlockSpec((tm, 1), row),
                  _const_spec((half, tm)),
                  _const_spec((1, LANES)),
                  _const_spec((HEAD_DIM, tm)),
                  _const_spec((1, LANES)),
                  _const_spec((1, LANES)),
                  _const_spec((1, LANES))],
        out_specs=[pl.BlockSpec((1, N_HEADS, HEAD_DIM, tm), lambda i: (i // tpb, 0, 0, i % tpb)),
                   pl.BlockSpec((1, IDX_HEADS, IDX_DIM, tm), lambda i: (i // tpb, 0, 0, i % tpb)),
                   pl.BlockSpec((1, IDX_HEADS, tm), lambda i: (i // tpb, 0, i % tpb)),
                   pl.BlockSpec((1, N_KV_HEADS, 1, V_AUG, tm), lambda i: (i // tpb, 0, i % tpb, 0, 0)),
                   pl.BlockSpec((1, N_KV_HEADS, tm, HEAD_DIM), lambda i: (i // tpb, 0, i % tpb, 0)),
                   pl.BlockSpec((tm, IDX_DIM), row)],
        out_shape=[jax.ShapeDtypeStruct((bn, N_HEADS, HEAD_DIM, t_len), BF16),
                   jax.ShapeDtypeStruct((bn, IDX_HEADS, IDX_DIM, t_len), BF16),
                   jax.ShapeDtypeStruct((bn, IDX_HEADS, t_len), F32),
                   jax.ShapeDtypeStruct((bn, N_KV_HEADS, tpb, V_AUG, tm), BF16),
                   jax.ShapeDtypeStruct((bn, N_KV_HEADS, t_len, HEAD_DIM), BF16),
                   jax.ShapeDtypeStruct((n, IDX_DIM), BF16)],
        compiler_params=_params(1),
        name="attn_pre",
    )(x2d, mod, g1.reshape(1, d), wta, wb, posr, posc, invft, invfr, gq_b, gk_r, lng_r, lnb_r)
    return outs


def _dsa_kernel(qT_ref, qiT_ref, wiT_ref, k4_ref, ki_ref, vT_ref, o_ref,
                st_ref, m_ref, al_ref, acc_ref, bias_ref, lg_ref, *, topk, rb):
    tq = o_ref.shape[1]
    kc = tq
    i = pl.program_id(1)
    n_c = i + 1
    q0 = i * tq

    tcol = q0 + lax.broadcasted_iota(I32, (kc, tq), 1)
    srow0 = lax.broadcasted_iota(I32, (kc, tq), 0)
    wi = wiT_ref[0]

    def score_body(c, carry):
        r0 = pl.multiple_of(c * kc, kc)
        ki = ki_ref[0, pl.ds(r0, kc), :]
        acc = jnp.zeros((kc, tq), F32)
        for hh in range(IDX_HEADS):
            s = jnp.dot(ki, qiT_ref[0, hh], preferred_element_type=F32)
            acc = acc + jnp.maximum(s, 0.0) * wi[hh:hh + 1, :]
        causal = (srow0 + r0) <= tcol
        st_ref[pl.ds(r0, kc), :] = jnp.where(causal, acc, -jnp.inf)
        return carry

    lax.fori_loop(0, n_c, score_body, 0)

    n_rb = n_c * (kc // rb)
    n_par = 4

    def count(pred):
        def body(r, acc):
            r0 = pl.multiple_of(r * rb, rb)
            blk = st_ref[pl.ds(r0, rb), :].reshape(rb // SUBLANES, SUBLANES, tq)
            hit = pred(blk, r0).astype(I32).reshape(rb // (SUBLANES * n_par), n_par, SUBLANES, tq)
            return acc + jnp.sum(hit, axis=0)
        acc = lax.fori_loop(0, n_rb, body, jnp.zeros((n_par, SUBLANES, tq), I32))
        tot = jnp.sum(acc, axis=0)
        for sh in (4, 2, 1):
            tot = tot + pltpu.roll(tot, sh, 0)
        return tot

    def key_to_float(u):
        ks = u ^ INT_MIN
        bits = jnp.where(ks < 0, ks ^ np.int32(0x7FFFFFFF), ks)
        return ks, lax.bitcast_convert_type(bits, F32)

    def bit_body(it, carry):
        tau_u, cnt_tau = carry
        cand_u = tau_u | lax.shift_left(np.int32(1), np.int32(31) - it)
        ks, cand_f = key_to_float(cand_u)
        cnt = count(lambda blk, r0: blk >= cand_f[None])
        take = (cnt >= topk) | (ks < KEY_NEG_INF)
        return jnp.where(take, cand_u, tau_u), jnp.where(take, cnt, cnt_tau)

    zero8 = jnp.zeros((SUBLANES, tq), I32)
    tau_u, cnt_ge = lax.fori_loop(0, 32, bit_body, (zero8, zero8))
    _, tau8 = key_to_float(tau_u)
    cnt_gt = count(lambda blk, r0: blk > tau8[None])
    need = topk - cnt_gt
    excess = jnp.max(((cnt_ge - cnt_gt) > need).astype(I32))

    t_total = st_ref.shape[0]
    n_bits = int(math.log2(t_total))
    sub_iota = lax.broadcasted_iota(I32, (rb // SUBLANES, SUBLANES, tq), 0) * SUBLANES + \
        lax.broadcasted_iota(I32, (rb // SUBLANES, SUBLANES, tq), 1)

    def tie_search():
        def jbit(it, j0):
            cand = j0 | lax.shift_left(np.int32(1), np.int32(n_bits - 1) - it)
            c = count(lambda blk, r0: (blk == tau8[None]) & ((sub_iota + r0) < cand[None]))
            return jnp.where(c < need, cand, j0)
        return lax.fori_loop(0, n_bits, jbit, zero8)

    jmax8 = lax.cond(excess > 0, tie_search, lambda: jnp.full((SUBLANES, tq), t_total, I32))
    tau1 = tau8[0:1]
    jmax1 = jmax8[0:1]

    m_ref[...] = jnp.full(m_ref.shape, -1e30, F32)
    acc_ref[...] = jnp.zeros(acc_ref.shape, F32)

    def set_bias(c):
        r0 = pl.multiple_of(c * kc, kc)
        sc = st_ref[pl.ds(r0, kc), :]
        srow = srow0 + r0
        mask = ((sc > tau1) | ((sc == tau1) & (srow <= jmax1))) & (srow <= tcol)
        bias_ref[...] = jnp.where(mask, 0.0, -jnp.inf)

    def logits_stage(c, hh):
        r0 = pl.multiple_of(c * kc, kc)
        kg = k4_ref[0, hh // GROUP, pl.ds(r0, kc), :]
        lg = jnp.dot(kg, qT_ref[0, hh], preferred_element_type=F32) + bias_ref[...]
        lg_ref[hh] = lg
        mx = jnp.max(lg.reshape(kc // SUBLANES, SUBLANES, tq), axis=0)
        m_old = m_ref[hh:hh + 1, :]
        m_new = jnp.maximum(m_old, jnp.max(mx, axis=0, keepdims=True))
        m_ref[hh:hh + 1, :] = m_new
        al_ref[hh:hh + 1, :] = jnp.exp2(m_old - m_new)

    def value_stage(c, hh):
        vt = vT_ref[0, hh // GROUP, c]
        p = jnp.exp2(lg_ref[hh] - m_ref[hh:hh + 1, :]).astype(BF16)
        acc_ref[hh] = acc_ref[hh] * al_ref[hh:hh + 1, :] + jnp.dot(vt, p, preferred_element_type=F32)

    set_bias(0)
    for hh in range(N_HEADS):
        logits_stage(0, hh)

    def attn_body(c, carry):
        set_bias(c)
        for hh in range(N_HEADS):
            value_stage(c - 1, hh)
            logits_stage(c, hh)
        return carry

    lax.fori_loop(1, n_c, attn_body, 0)
    for hh in range(N_HEADS):
        value_stage(n_c - 1, hh)

    for hp in range(N_HEADS // 2):
        parts = []
        for hh in (2 * hp, 2 * hp + 1):
            a = acc_ref[hh]
            parts.append(a[0:HEAD_DIM] / a[HEAD_DIM:HEAD_DIM + 1])
        pair = jnp.concatenate(parts, axis=0)
        o_ref[0, :, LANES * hp:LANES * (hp + 1)] = pair.T.astype(BF16)


def _dsa_attention(qT, qiT, wiT, k4, ki, vT, bn, t_len, topk):
    tq = topk
    rb = 256
    kern = functools.partial(_dsa_kernel, topk=topk, rb=rb)
    return pl.pallas_call(
        kern,
        grid=(bn, t_len // tq),
        in_specs=[pl.BlockSpec((1, N_HEADS, HEAD_DIM, tq), lambda b, i: (b, 0, 0, i)),
                  pl.BlockSpec((1, IDX_HEADS, IDX_DIM, tq), lambda b, i: (b, 0, 0, i)),
                  pl.BlockSpec((1, IDX_HEADS, tq), lambda b, i: (b, 0, i)),
                  pl.BlockSpec((1, N_KV_HEADS, t_len, HEAD_DIM), lambda b, i: (b, 0, 0, 0),
                               pipeline_mode=pl.Buffered(1)),
                  pl.BlockSpec((1, t_len, IDX_DIM), lambda b, i: (b, 0, 0), pipeline_mode=pl.Buffered(1)),
                  pl.BlockSpec((1, N_KV_HEADS, t_len // tq, V_AUG, tq), lambda b, i: (b, 0, 0, 0, 0),
                               pipeline_mode=pl.Buffered(1))],
        out_specs=pl.BlockSpec((1, tq, Q_COLS), lambda b, i: (b, i, 0)),
        out_shape=jax.ShapeDtypeStruct((bn, t_len, Q_COLS), BF16),
        scratch_shapes=[pltpu.VMEM((t_len, tq), F32),
                        pltpu.VMEM((N_HEADS, tq), F32),
                        pltpu.VMEM((N_HEADS, tq), F32),
                        pltpu.VMEM((N_HEADS, V_AUG, tq), F32),
                        pltpu.VMEM((tq, tq), F32),
                        pltpu.VMEM((N_HEADS, tq, tq), F32)],
        compiler_params=_params(2),
        name="dsa_attention",
    )(qT, qiT, wiT, k4, ki.reshape(bn, t_len, IDX_DIM), vT)


def _conv_pre_kernel(x_ref, mod_ref, g1_ref, w_ref, cw_ref, o_ref, zbuf_ref, *, tpb):
    tm, d = x_ref.shape
    i = pl.program_id(0)

    @pl.when(i % tpb == 0)
    def _():
        zbuf_ref[0:SUBLANES, :] = jnp.zeros((SUBLANES, d), F32)

    x = x_ref[...]
    sh = mod_ref[0, 0:1, :]
    sc = mod_ref[0, 1:2, :]
    ms = jnp.mean(x * x, axis=-1, keepdims=True)
    h = (x * lax.rsqrt(ms + NORM_EPS) * g1_ref[...]) * (1.0 + sc) + sh
    proj = jnp.dot(h.astype(BF16), w_ref[...], preferred_element_type=F32)
    b_gate = proj[:, 0:d]
    z = proj[:, d:2 * d] * proj[:, 2 * d:3 * d]
    zbuf_ref[SUBLANES:SUBLANES + tm, :] = z
    z1 = zbuf_ref[SUBLANES - 1:SUBLANES - 1 + tm, :]
    z2 = zbuf_ref[SUBLANES - 2:SUBLANES - 2 + tm, :]
    zc = cw_ref[0:1, :] * z2 + cw_ref[1:2, :] * z1 + cw_ref[2:3, :] * z
    o_ref[...] = (b_gate * zc).astype(BF16)
    zbuf_ref[0:SUBLANES, :] = z[tm - SUBLANES:tm, :]


def _conv_pre(x2d, mod, g1, w_in, conv_w, t_len, tm):
    n, d = x2d.shape
    tpb = t_len // tm
    row = lambda i: (i, 0)
    return pl.pallas_call(
        functools.partial(_conv_pre_kernel, tpb=tpb),
        grid=(n // tm,),
        in_specs=[pl.BlockSpec((tm, d), row),
                  pl.BlockSpec((1, 6, d), lambda i: (i // tpb, 0, 0)),
                  _const_spec((1, d)),
                  _const_spec((d, 3 * d)),
                  _const_spec((CONV_WIDTH, d))],
        out_specs=pl.BlockSpec((tm, d), row),
        out_shape=jax.ShapeDtypeStruct((n, d), BF16),
        scratch_shapes=[pltpu.VMEM((tm + SUBLANES, d), F32)],
        compiler_params=_params(1),
        name="conv_pre",
    )(x2d, mod, g1.reshape(1, d), w_in.astype(BF16), conv_w.astype(F32))


def _post_kernel(x_ref, mix_ref, mod_ref, g2_ref, wo_ref, wg_ref, wu_ref, wd_ref, o_ref, *, th):
    gate1 = mod_ref[0, 2:3, :]
    sh2 = mod_ref[0, 3:4, :]
    sc2 = mod_ref[0, 4:5, :]
    gate2 = mod_ref[0, 5:6, :]
    y = jnp.dot(mix_ref[...], wo_ref[...], preferred_element_type=F32)
    x1 = x_ref[...] + gate1 * y
    ms = jnp.mean(x1 * x1, axis=-1, keepdims=True)
    h = ((x1 * lax.rsqrt(ms + NORM_EPS) * g2_ref[...]) * (1.0 + sc2) + sh2).astype(BF16)
    hidden = wg_ref.shape[1]
    acc = jnp.zeros(x1.shape, F32)
    for j in range(hidden // th):
        gt = jnp.dot(h, wg_ref[:, th * j:th * (j + 1)], preferred_element_type=F32)
        up = jnp.dot(h, wu_ref[:, th * j:th * (j + 1)], preferred_element_type=F32)
        a = (gt * jax.nn.sigmoid(gt)) * up
        acc = acc + jnp.dot(a.astype(BF16), wd_ref[th * j:th * (j + 1), :], preferred_element_type=F32)
    o_ref[...] = x1 + gate2 * acc


def _post(x2d, mix, mod, g2, w_out, w_gate, w_up, w_down, t_len, tm):
    n, d = x2d.shape
    hidden = w_gate.shape[1]
    tpb = t_len // tm
    row = lambda i: (i, 0)
    return pl.pallas_call(
        functools.partial(_post_kernel, th=256),
        grid=(n // tm,),
        in_specs=[pl.BlockSpec((tm, d), row),
                  pl.BlockSpec((tm, d), row),
                  pl.BlockSpec((1, 6, d), lambda i: (i // tpb, 0, 0)),
                  _const_spec((1, d)),
                  _const_spec((d, d)),
                  _const_spec((d, hidden)),
                  _const_spec((d, hidden)),
                  _const_spec((hidden, d))],
        out_specs=pl.BlockSpec((tm, d), row),
        out_shape=jax.ShapeDtypeStruct((n, d), F32),
        compiler_params=_params(1),
        name="mixer_out_ffn",
    )(x2d, mix, mod, g2.reshape(1, d), w_out.astype(BF16), w_gate.astype(BF16), w_up.astype(BF16),
      w_down.astype(BF16))


def kernel(x, c, positions, ada_w, ada_b, norm1_g, norm2_g, attn_w_in, attn_q_norm_g, attn_k_norm_g,
           idx_k_ln_g, idx_k_ln_b, attn_w_out, conv_w_in, conv_w, conv_w_out, ffn_w_gate, ffn_w_up,
           ffn_w_down):
    bn, t_len, d = x.shape
    depth = ada_w.shape[0]
    topk = min(TOPK_MAX, t_len // 4)
    assert topk == TOPK_MAX and t_len % topk == 0 and (t_len & (t_len - 1)) == 0
    n = bn * t_len
    tm_pre = 256
    tm_post = 512
    mod = _modulation(c, ada_w, ada_b)
    x2d = x.reshape(n, d)
    for i in range(depth):
        j = i // 2
        if i % 2 == 0:
            qT, qiT, wiT, vT, k2, kiP = _attn_pre(
                x2d, mod[i], norm1_g[i], attn_w_in[j], attn_q_norm_g[j], attn_k_norm_g[j],
                idx_k_ln_g[j], idx_k_ln_b[j], positions, bn, t_len, tm_pre)
            mix = _dsa_attention(qT, qiT, wiT, k2, kiP, vT, bn, t_len, topk).reshape(n, d)
            w_mix_out = attn_w_out[j]
        else:
            mix = _conv_pre(x2d, mod[i], norm1_g[i], conv_w_in[j], conv_w[j], t_len, tm_post)
            w_mix_out = conv_w_out[j]
        x2d = _post(x2d, mix, mod[i], norm2_g[i], w_mix_out, ffn_w_gate[i], ffn_w_up[i], ffn_w_down[i],
                    t_len, tm_post)
    return x2d.reshape(bn, t_len, d)
```

```python
import functools
import math

import numpy as np
import jax
import jax.numpy as jnp
from jax import lax
from jax.experimental import pallas as pl
from jax.experimental.pallas import tpu as pltpu

F32 = jnp.float32
BF16 = jnp.bfloat16
I32 = jnp.int32

N_HEADS = 16
HEAD_DIM = 64
N_KV_HEADS = 4
GROUP = N_HEADS // N_KV_HEADS
IDX_HEADS = 8
IDX_DIM = 64
TOPK_MAX = 256
ROPE_THETA = 10000.0
CONV_WIDTH = 3
NORM_EPS = 1e-6
LOG2E = math.log2(math.e)

LANES = 128
SUBLANES = 8
VMEM_LIMIT = 56 * 1024 * 1024

Q_COLS = N_HEADS * HEAD_DIM
KV_COLS = N_KV_HEADS * HEAD_DIM
QI_COLS = IDX_HEADS * IDX_DIM
V_AUG = HEAD_DIM + 16

INT_MIN = np.int32(-2**31)
KEY_NEG_INF = np.int32(np.uint32(0x807FFFFF).astype(np.int64) - 2**32)
KEY_HI_NEG_INF = np.int32(np.uint32(0x807F0000).astype(np.int64) - 2**32)


def _const_spec(shape):
    nd = len(shape)
    return pl.BlockSpec(shape, lambda *_: (0,) * nd, pipeline_mode=pl.Buffered(1))


def _params(n_axes):
    return pltpu.CompilerParams(dimension_semantics=("arbitrary",) * n_axes,
                                vmem_limit_bytes=VMEM_LIMIT)


def _mod_kernel(c_ref, w_ref, b_ref, o_ref):
    c = c_ref[...]
    ca = c * jax.nn.sigmoid(c)
    o_ref[0] = jnp.dot(ca.astype(BF16), w_ref[0].astype(BF16), preferred_element_type=F32) + b_ref[0]


def _modulation(c, ada_w, ada_b):
    depth, d, six_d = ada_w.shape
    bn = c.shape[0]
    rows = 16
    cp = jnp.zeros((rows, d), F32).at[:bn].set(c)
    tn = 1536
    out = pl.pallas_call(
        _mod_kernel,
        grid=(depth, six_d // tn),
        in_specs=[pl.BlockSpec((rows, d), lambda i, j: (0, 0)),
                  pl.BlockSpec((1, d, tn), lambda i, j: (i, 0, j)),
                  pl.BlockSpec((1, 1, tn), lambda i, j: (i, 0, j))],
        out_specs=pl.BlockSpec((1, rows, tn), lambda i, j: (i, 0, j)),
        out_shape=jax.ShapeDtypeStruct((depth, rows, six_d), F32),
        compiler_params=_params(2),
        name="adaln_mod",
    )(cp, ada_w, ada_b.reshape(depth, 1, six_d))
    return out[:, :bn].reshape(depth, bn, 6, d)


def _segmean64(v, gmat):
    hi = v.astype(BF16)
    lo = (v - hi.astype(F32)).astype(BF16)
    s = jnp.dot(hi, gmat, preferred_element_type=F32) + jnp.dot(lo, gmat, preferred_element_type=F32)
    return s * (1.0 / HEAD_DIM)


def _attn_pre_kernel(x_ref, mod_ref, g1_ref, wta_ref, wb_ref, posr_ref, posc_ref, invft_ref, invfr_ref,
                     gq_ref, gk_ref, lng_ref, lnb_ref,
                     qT_ref, qiT_ref, wiT_ref, vT_ref, k4_ref, ki_ref):
    tm = x_ref.shape[0]
    x = x_ref[...]
    sh = mod_ref[0, 0:1, :]
    sc = mod_ref[0, 1:2, :]
    ms = jnp.mean(x * x, axis=-1, keepdims=True)
    h = (x * lax.rsqrt(ms + NORM_EPS) * g1_ref[...]) * (1.0 + sc) + sh
    hb = h.astype(BF16)

    pT = lax.dot_general(wta_ref[...], hb, (((1,), (1,)), ((), ())), preferred_element_type=F32)
    angT = invft_ref[...] * posr_ref[0].astype(F32)
    cT = jnp.cos(angT)
    sT = jnp.sin(angT)
    half = HEAD_DIM // 2

    def rope_t(y):
        x1 = y[:half]
        x2 = y[half:]
        return jnp.concatenate([x1 * cT - x2 * sT, x2 * cT + x1 * sT], axis=0)

    gq = gq_ref[...]
    for hh in range(N_HEADS):
        xq = pT[HEAD_DIM * hh:HEAD_DIM * (hh + 1)]
        inv = lax.rsqrt(jnp.mean(xq * xq, axis=0, keepdims=True) + NORM_EPS)
        y = xq * inv * gq
        qT_ref[0, hh] = (rope_t(y) * (HEAD_DIM ** -0.5 * LOG2E)).astype(BF16)
    o0 = Q_COLS
    ones = jnp.ones((V_AUG - HEAD_DIM, tm), BF16)
    for g in range(N_KV_HEADS):
        vT_ref[0, g, 0, 0:HEAD_DIM, :] = pT[o0 + HEAD_DIM * g:o0 + HEAD_DIM * (g + 1)].astype(BF16)
        vT_ref[0, g, 0, HEAD_DIM:V_AUG, :] = ones
    o0 += KV_COLS
    for hh in range(IDX_HEADS):
        qiT_ref[0, hh] = rope_t(pT[o0 + IDX_DIM * hh:o0 + IDX_DIM * (hh + 1)]).astype(BF16)
    o0 += QI_COLS
    wiT_ref[0] = pT[o0:o0 + IDX_HEADS] * (IDX_HEADS ** -0.5 * IDX_DIM ** -0.5)

    pk = jnp.dot(hb, wb_ref[...], preferred_element_type=F32)
    ang = posc_ref[...].astype(F32) * invfr_ref[...]
    cR = jnp.cos(ang)
    sR = jnp.sin(ang)
    lane = lax.broadcasted_iota(I32, (tm, LANES), 1)
    first = (lane % HEAD_DIM) < half
    s_signed = jnp.where(first, -sR, sR)
    ri = lax.broadcasted_iota(I32, (LANES, LANES), 0) // HEAD_DIM
    ci = lax.broadcasted_iota(I32, (LANES, LANES), 1) // HEAD_DIM
    gmat = jnp.where(ri == ci, 1.0, 0.0).astype(BF16)

    def rope_r(y):
        rot = jnp.where(first, pltpu.roll(y, LANES - half, 1), pltpu.roll(y, half, 1))
        return y * cR + rot * s_signed

    for j in range(KV_COLS // LANES):
        xs = pk[:, LANES * j:LANES * (j + 1)]
        inv = lax.rsqrt(_segmean64(xs * xs, gmat) + NORM_EPS)
        kr = rope_r(xs * inv * gk_ref[...])
        k4_ref[0, 2 * j] = kr[:, 0:HEAD_DIM].astype(BF16)
        k4_ref[0, 2 * j + 1] = kr[:, HEAD_DIM:LANES].astype(BF16)
    t = pk[:, KV_COLS:KV_COLS + LANES]
    mu = _segmean64(t, gmat)
    dlt = t - mu
    var = _segmean64(dlt * dlt, gmat)
    kin = dlt * lax.rsqrt(var + NORM_EPS) * lng_ref[...] + lnb_ref[...]
    ki_ref[...] = rope_r(kin)[:, 0:IDX_DIM].astype(BF16)


def _attn_pre(x2d, mod, g1, w_in, gq, gk, lng, lnb, positions, bn, t_len, tm):
    n, d = x2d.shape
    tpb = t_len // tm
    half = HEAD_DIM // 2
    wq = w_in[:, :Q_COLS]
    wk = w_in[:, Q_COLS:Q_COLS + KV_COLS]
    wv = w_in[:, Q_COLS + KV_COLS:Q_COLS + 2 * KV_COLS]
    o0 = Q_COLS + 2 * KV_COLS
    wqi = w_in[:, o0:o0 + QI_COLS]
    wki = w_in[:, o0 + QI_COLS:o0 + QI_COLS + IDX_DIM]
    wwi = w_in[:, o0 + QI_COLS + IDX_DIM:]
    ma = Q_COLS + KV_COLS + QI_COLS + IDX_HEADS
    ma_pad = -(-ma // 16) * 16
    wta = jnp.concatenate([wq, wv, wqi, wwi, jnp.zeros((d, ma_pad - ma), F32)], axis=1).T.astype(BF16)
    wb = jnp.concatenate([wk, wki, jnp.zeros((d, LANES - IDX_DIM), F32)], axis=1).astype(BF16)
    nb = wb.shape[1]
    inv_freq = ROPE_THETA ** (-(jnp.arange(half, dtype=F32) * 2.0 / HEAD_DIM))
    invft = jnp.broadcast_to(inv_freq[:, None], (half, tm))
    invfr = jnp.tile(inv_freq, LANES // half).reshape(1, LANES)
    gq_b = jnp.broadcast_to(gq.astype(F32)[:, None], (HEAD_DIM, tm))
    gk_r = jnp.tile(gk.astype(F32), LANES // HEAD_DIM).reshape(1, LANES)
    zpad = jnp.zeros((LANES - IDX_DIM,), F32)
    lng_r = jnp.concatenate([lng.astype(F32), zpad]).reshape(1, LANES)
    lnb_r = jnp.concatenate([lnb.astype(F32), zpad]).reshape(1, LANES)
    posr = positions.reshape(bn, 1, t_len)
    posc = positions.reshape(n, 1)

    row = lambda i: (i, 0)
    outs = pl.pallas_call(
        _attn_pre_kernel,
        grid=(n // tm,),
        in_specs=[pl.BlockSpec((tm, d), row),
                  pl.BlockSpec((1, 6, d), lambda i: (i // tpb, 0, 0)),
                  _const_spec((1, d)),
                  _const_spec((ma_pad, d)),
                  _const_spec((d, nb)),
                  pl.BlockSpec((1, 1, tm), lambda i: (i // tpb, 0, i % tpb)),
                  pl.BlockSpec((tm, 1), row),
                  _const_spec((half, tm)),
                  _const_spec((1, LANES)),
                  _const_spec((HEAD_DIM, tm)),
                  _const_spec((1, LANES)),
                  _const_spec((1, LANES)),
                  _const_spec((1, LANES))],
        out_specs=[pl.BlockSpec((1, N_HEADS, HEAD_DIM, tm), lambda i: (i // tpb, 0, 0, i % tpb)),
                   pl.BlockSpec((1, IDX_HEADS, IDX_DIM, tm), lambda i: (i // tpb, 0, 0, i % tpb)),
                   pl.BlockSpec((1, IDX_HEADS, tm), lambda i: (i // tpb, 0, i % tpb)),
                   pl.BlockSpec((1, N_KV_HEADS, 1, V_AUG, tm), lambda i: (i // tpb, 0, i % tpb, 0, 0)),
                   pl.BlockSpec((1, N_KV_HEADS, tm, HEAD_DIM), lambda i: (i // tpb, 0, i % tpb, 0)),
                   pl.BlockSpec((tm, IDX_DIM), row)],
        out_shape=[jax.ShapeDtypeStruct((bn, N_HEADS, HEAD_DIM, t_len), BF16),
                   jax.ShapeDtypeStruct((bn, IDX_HEADS, IDX_DIM, t_len), BF16),
                   jax.ShapeDtypeStruct((bn, IDX_HEADS, t_len), F32),
                   jax.ShapeDtypeStruct((bn, N_KV_HEADS, tpb, V_AUG, tm), BF16),
                   jax.ShapeDtypeStruct((bn, N_KV_HEADS, t_len, HEAD_DIM), BF16),
                   jax.ShapeDtypeStruct((n, IDX_DIM), BF16)],
        compiler_params=_params(1),
        name="attn_pre",
    )(x2d, mod, g1.reshape(1, d), wta, wb, posr, posc, invft, invfr, gq_b, gk_r, lng_r, lnb_r)
    return outs


def _dsa_kernel(qT_ref, qiT_ref, wiT_ref, k4_ref, ki_ref, vT_ref, o_ref,
                st_ref, hi_ref, m_ref, al_ref, acc_ref, bias_ref, lg_ref, *, topk, rb):
    tq = o_ref.shape[1]
    kc = tq
    i = pl.program_id(1)
    n_c = i + 1
    q0 = i * tq

    tcol = q0 + lax.broadcasted_iota(I32, (kc, tq), 1)
    srow0 = lax.broadcasted_iota(I32, (kc, tq), 0)
    wi = wiT_ref[0]

    def score_body(c, carry):
        r0 = pl.multiple_of(c * kc, kc)
        ki = ki_ref[0, pl.ds(r0, kc), :]
        acc = jnp.zeros((kc, tq), F32)
        for hh in range(IDX_HEADS):
            s = jnp.dot(ki, qiT_ref[0, hh], preferred_element_type=F32)
            acc = acc + jnp.maximum(s, 0.0) * wi[hh:hh + 1, :]
        causal = (srow0 + r0) <= tcol
        sc = jnp.where(causal, acc, -jnp.inf)
        st_ref[pl.ds(r0, kc), :] = sc
        top = lax.bitcast_convert_type(sc, I32) & np.int32(-65536)
        hi_ref[pl.ds(r0, kc), :] = lax.bitcast_convert_type(top, F32).astype(BF16)
        return carry

    lax.fori_loop(0, n_c, score_body, 0)

    n_rb = n_c * (kc // rb)
    n_par = 4
    pack = 2 * SUBLANES

    def all_sublanes(tot):
        for sh in (4, 2, 1):
            tot = tot + pltpu.roll(tot, sh, 0)
        return tot

    def count(pred):
        def body(r, acc):
            r0 = pl.multiple_of(r * rb, rb)
            blk = st_ref[pl.ds(r0, rb), :].reshape(rb // SUBLANES, SUBLANES, tq)
            hit = pred(blk, r0).astype(I32).reshape(rb // (SUBLANES * n_par), n_par, SUBLANES, tq)
            return acc + jnp.sum(hit, axis=0)
        acc = lax.fori_loop(0, n_rb, body, jnp.zeros((n_par, SUBLANES, tq), I32))
        return all_sublanes(jnp.sum(acc, axis=0))

    def count_hi(cand16):
        def body(r, acc):
            r0 = pl.multiple_of(r * rb, rb)
            blk = hi_ref[pl.ds(r0, rb), :].reshape(rb // pack, pack, tq)
            hit = jnp.where(blk >= cand16[None], jnp.ones((), BF16), jnp.zeros((), BF16))
            parts = [hit[g] for g in range(rb // pack)]
            while len(parts) > 1:
                parts = [parts[g] + parts[g + 1] for g in range(0, len(parts), 2)]
            return acc + parts[0].astype(F32)
        acc = lax.fori_loop(0, n_rb, body, jnp.zeros((pack, tq), F32))
        return all_sublanes((acc[0:SUBLANES] + acc[SUBLANES:pack]).astype(I32))

    def key_to_float(u):
        ks = u ^ INT_MIN
        bits = jnp.where(ks < 0, ks ^ np.int32(0x7FFFFFFF), ks)
        return ks, bits

    def hi_body(it, carry):
        tau_u, cnt_tau = carry
        cand_u = tau_u | lax.shift_left(np.int32(1), np.int32(31) - it)
        ks, bits = key_to_float(cand_u)
        cand_top = lax.bitcast_convert_type(bits & np.int32(-65536), F32)
        cand16 = jnp.concatenate([cand_top, cand_top], axis=0).astype(BF16)
        cnt = count_hi(cand16)
        take = (cnt >= topk) | (ks < KEY_HI_NEG_INF)
        return jnp.where(take, cand_u, tau_u), jnp.where(take, cnt, cnt_tau)

    def lo_body(it, carry):
        tau_u, cnt_tau = carry
        cand_u = tau_u | lax.shift_left(np.int32(1), np.int32(15) - it)
        ks, bits = key_to_float(cand_u)
        cand_f = lax.bitcast_convert_type(bits, F32)
        cnt = count(lambda blk, r0: blk >= cand_f[None])
        take = (cnt >= topk) | (ks < KEY_NEG_INF)
        return jnp.where(take, cand_u, tau_u), jnp.where(take, cnt, cnt_tau)

    zero8 = jnp.zeros((SUBLANES, tq), I32)
    carry = lax.fori_loop(0, 16, hi_body, (zero8, zero8))
    tau_u, cnt_ge = lax.fori_loop(0, 16, lo_body, carry)
    tau8 = lax.bitcast_convert_type(key_to_float(tau_u)[1], F32)
    cnt_gt = count(lambda blk, r0: blk > tau8[None])
    need = topk - cnt_gt
    excess = jnp.max(((cnt_ge - cnt_gt) > need).astype(I32))

    t_total = st_ref.shape[0]
    n_bits = int(math.log2(t_total))
    sub_iota = lax.broadcasted_iota(I32, (rb // SUBLANES, SUBLANES, tq), 0) * SUBLANES + \
        lax.broadcasted_iota(I32, (rb // SUBLANES, SUBLANES, tq), 1)

    def tie_search():
        def jbit(it, j0):
            cand = j0 | lax.shift_left(np.int32(1), np.int32(n_bits - 1) - it)
            c = count(lambda blk, r0: (blk == tau8[None]) & ((sub_iota + r0) < cand[None]))
            return jnp.where(c < need, cand, j0)
        return lax.fori_loop(0, n_bits, jbit, zero8)

    jmax8 = lax.cond(excess > 0, tie_search, lambda: jnp.full((SUBLANES, tq), t_total, I32))
    tau1 = tau8[0:1]
    jmax1 = jmax8[0:1]

    m_ref[...] = jnp.full(m_ref.shape, -1e30, F32)
    acc_ref[...] = jnp.zeros(acc_ref.shape, F32)

    def set_bias(c):
        r0 = pl.multiple_of(c * kc, kc)
        sc = st_ref[pl.ds(r0, kc), :]
        srow = srow0 + r0
        mask = ((sc > tau1) | ((sc == tau1) & (srow <= jmax1))) & (srow <= tcol)
        bias_ref[...] = jnp.where(mask, 0.0, -jnp.inf)

    def logits_stage(c, hh):
        r0 = pl.multiple_of(c * kc, kc)
        kg = k4_ref[0, hh // GROUP, pl.ds(r0, kc), :]
        lg = jnp.dot(kg, qT_ref[0, hh], preferred_element_type=F32) + bias_ref[...]
        lg_ref[hh] = lg
        mx = jnp.max(lg.reshape(kc // SUBLANES, SUBLANES, tq), axis=0)
        m_old = m_ref[hh:hh + 1, :]
        m_new = jnp.maximum(m_old, jnp.max(mx, axis=0, keepdims=True))
        m_ref[hh:hh + 1, :] = m_new
        al_ref[hh:hh + 1, :] = jnp.exp2(m_old - m_new)

    def value_stage(c, hh):
        vt = vT_ref[0, hh // GROUP, c]
        p = jnp.exp2(lg_ref[hh] - m_ref[hh:hh + 1, :]).astype(BF16)
        acc_ref[hh] = acc_ref[hh] * al_ref[hh:hh + 1, :] + jnp.dot(vt, p, preferred_element_type=F32)

    set_bias(0)
    for hh in range(N_HEADS):
        logits_stage(0, hh)

    def attn_body(c, carry):
        set_bias(c)
        for hh in range(N_HEADS):
            value_stage(c - 1, hh)
            logits_stage(c, hh)
        return carry

    lax.fori_loop(1, n_c, attn_body, 0)
    for hh in range(N_HEADS):
        value_stage(n_c - 1, hh)

    for hp in range(N_HEADS // 2):
        parts = []
        for hh in (2 * hp, 2 * hp + 1):
            a = acc_ref[hh]
            parts.append(a[0:HEAD_DIM] / a[HEAD_DIM:HEAD_DIM + 1])
        pair = jnp.concatenate(parts, axis=0)
        o_ref[0, :, LANES * hp:LANES * (hp + 1)] = pair.T.astype(BF16)


def _dsa_attention(qT, qiT, wiT, k4, ki, vT, bn, t_len, topk):
    tq = topk
    rb = 256
    kern = functools.partial(_dsa_kernel, topk=topk, rb=rb)
    return pl.pallas_call(
        kern,
        grid=(bn, t_len // tq),
        in_specs=[pl.BlockSpec((1, N_HEADS, HEAD_DIM, tq), lambda b, i: (b, 0, 0, i)),
                  pl.BlockSpec((1, IDX_HEADS, IDX_DIM, tq), lambda b, i: (b, 0, 0, i)),
                  pl.BlockSpec((1, IDX_HEADS, tq), lambda b, i: (b, 0, i)),
                  pl.BlockSpec((1, N_KV_HEADS, t_len, HEAD_DIM), lambda b, i: (b, 0, 0, 0),
                               pipeline_mode=pl.Buffered(1)),
                  pl.BlockSpec((1, t_len, IDX_DIM), lambda b, i: (b, 0, 0), pipeline_mode=pl.Buffered(1)),
                  pl.BlockSpec((1, N_KV_HEADS, t_len // tq, V_AUG, tq), lambda b, i: (b, 0, 0, 0, 0),
                               pipeline_mode=pl.Buffered(1))],
        out_specs=pl.BlockSpec((1, tq, Q_COLS), lambda b, i: (b, i, 0)),
        out_shape=jax.ShapeDtypeStruct((bn, t_len, Q_COLS), BF16),
        scratch_shapes=[pltpu.VMEM((t_len, tq), F32),
                        pltpu.VMEM((t_len, tq), BF16),
                        pltpu.VMEM((N_HEADS, tq), F32),
                        pltpu.VMEM((N_HEADS, tq), F32),
                        pltpu.VMEM((N_HEADS, V_AUG, tq), F32),
                        pltpu.VMEM((tq, tq), F32),
                        pltpu.VMEM((N_HEADS, tq, tq), F32)],
        compiler_params=_params(2),
        name="dsa_attention",
    )(qT, qiT, wiT, k4, ki.reshape(bn, t_len, IDX_DIM), vT)


def _conv_pre_kernel(x_ref, mod_ref, g1_ref, w_ref, cw_ref, o_ref, zbuf_ref, *, tpb):
    tm, d = x_ref.shape
    i = pl.program_id(0)

    @pl.when(i % tpb == 0)
    def _():
        zbuf_ref[0:SUBLANES, :] = jnp.zeros((SUBLANES, d), F32)

    x = x_ref[...]
    sh = mod_ref[0, 0:1, :]
    sc = mod_ref[0, 1:2, :]
    ms = jnp.mean(x * x, axis=-1, keepdims=True)
    h = (x * lax.rsqrt(ms + NORM_EPS) * g1_ref[...]) * (1.0 + sc) + sh
    proj = jnp.dot(h.astype(BF16), w_ref[...], preferred_element_type=F32)
    b_gate = proj[:, 0:d]
    z = proj[:, d:2 * d] * proj[:, 2 * d:3 * d]
    zbuf_ref[SUBLANES:SUBLANES + tm, :] = z
    z1 = zbuf_ref[SUBLANES - 1:SUBLANES - 1 + tm, :]
    z2 = zbuf_ref[SUBLANES - 2:SUBLANES - 2 + tm, :]
    zc = cw_ref[0:1, :] * z2 + cw_ref[1:2, :] * z1 + cw_ref[2:3, :] * z
    o_ref[...] = (b_gate * zc).astype(BF16)
    zbuf_ref[0:SUBLANES, :] = z[tm - SUBLANES:tm, :]


def _conv_pre(x2d, mod, g1, w_in, conv_w, t_len, tm):
    n, d = x2d.shape
    tpb = t_len // tm
    row = lambda i: (i, 0)
    return pl.pallas_call(
        functools.partial(_conv_pre_kernel, tpb=tpb),
        grid=(n // tm,),
        in_specs=[pl.BlockSpec((tm, d), row),
                  pl.BlockSpec((1, 6, d), lambda i: (i // tpb, 0, 0)),
                  _const_spec((1, d)),
                  _const_spec((d, 3 * d)),
                  _const_spec((CONV_WIDTH, d))],
        out_specs=pl.BlockSpec((tm, d), row),
        out_shape=jax.ShapeDtypeStruct((n, d), BF16),
        scratch_shapes=[pltpu.VMEM((tm + SUBLANES, d), F32)],
        compiler_params=_params(1),
        name="conv_pre",
    )(x2d, mod, g1.reshape(1, d), w_in.astype(BF16), conv_w.astype(F32))


def _post_kernel(x_ref, mix_ref, mod_ref, g2_ref, wo_ref, wg_ref, wu_ref, wd_ref, o_ref, *, th):
    gate1 = mod_ref[0, 2:3, :]
    sh2 = mod_ref[0, 3:4, :]
    sc2 = mod_ref[0, 4:5, :]
    gate2 = mod_ref[0, 5:6, :]
    y = jnp.dot(mix_ref[...], wo_ref[...], preferred_element_type=F32)
    x1 = x_ref[...] + gate1 * y
    ms = jnp.mean(x1 * x1, axis=-1, keepdims=True)
    h = ((x1 * lax.rsqrt(ms + NORM_EPS) * g2_ref[...]) * (1.0 + sc2) + sh2).astype(BF16)
    hidden = wg_ref.shape[1]
    acc = jnp.zeros(x1.shape, F32)
    for j in range(hidden // th):
        gt = jnp.dot(h, wg_ref[:, th * j:th * (j + 1)], preferred_element_type=F32)
        up = jnp.dot(h, wu_ref[:, th * j:th * (j + 1)], preferred_element_type=F32)
        a = (gt * jax.nn.sigmoid(gt)) * up
        acc = acc + jnp.dot(a.astype(BF16), wd_ref[th * j:th * (j + 1), :], preferred_element_type=F32)
    o_ref[...] = x1 + gate2 * acc


def _post(x2d, mix, mod, g2, w_out, w_gate, w_up, w_down, t_len, tm):
    n, d = x2d.shape
    hidden = w_gate.shape[1]
    tpb = t_len // tm
    row = lambda i: (i, 0)
    return pl.pallas_call(
        functools.partial(_post_kernel, th=256),
        grid=(n // tm,),
        in_specs=[pl.BlockSpec((tm, d), row),
                  pl.BlockSpec((tm, d), row),
                  pl.BlockSpec((1, 6, d), lambda i: (i // tpb, 0, 0)),
                  _const_spec((1, d)),
                  _const_spec((d, d)),
                  _const_spec((d, hidden)),
                  _const_spec((d, hidden)),
                  _const_spec((hidden, d))],
        out_specs=pl.BlockSpec((tm, d), row),
        out_shape=jax.ShapeDtypeStruct((n, d), F32),
        compiler_params=_params(1),
        name="mixer_out_ffn",
    )(x2d, mix, mod, g2.reshape(1, d), w_out.astype(BF16), w_gate.astype(BF16), w_up.astype(BF16),
      w_down.astype(BF16))


def kernel(x, c, positions, ada_w, ada_b, norm1_g, norm2_g, attn_w_in, attn_q_norm_g, attn_k_norm_g,
           idx_k_ln_g, idx_k_ln_b, attn_w_out, conv_w_in, conv_w, conv_w_out, ffn_w_gate, ffn_w_up,
           ffn_w_down):
    bn, t_len, d = x.shape
    depth = ada_w.shape[0]
    topk = min(TOPK_MAX, t_len // 4)
    assert topk == TOPK_MAX and t_len % topk == 0 and (t_len & (t_len - 1)) == 0
    n = bn * t_len
    tm_pre = 256
    tm_post = 512
    mod = _modulation(c, ada_w, ada_b)
    x2d = x.reshape(n, d)
    for i in range(depth):
        j = i // 2
        if i % 2 == 0:
            qT, qiT, wiT, vT, k2, kiP = _attn_pre(
                x2d, mod[i], norm1_g[i], attn_w_in[j], attn_q_norm_g[j], attn_k_norm_g[j],
                idx_k_ln_g[j], idx_k_ln_b[j], positions, bn, t_len, tm_pre)
            mix = _dsa_attention(qT, qiT, wiT, k2, kiP, vT, bn, t_len, topk).reshape(n, d)
            w_mix_out = attn_w_out[j]
        else:
            mix = _conv_pre(x2d, mod[i], norm1_g[i], conv_w_in[j], conv_w[j], t_len, tm_post)
            w_mix_out = conv_w_out[j]
        x2d = _post(x2d, mix, mod[i], norm2_g[i], w_mix_out, ffn_w_gate[i], ffn_w_up[i], ffn_w_down[i],
                    t_len, tm_post)
    return x2d.reshape(bn, t_len, d)
```

```python
import functools
import math

import numpy as np
import jax
import jax.numpy as jnp
from jax import lax
from jax.experimental import pallas as pl
from jax.experimental.pallas import tpu as pltpu

F32 = jnp.float32
BF16 = jnp.bfloat16
I32 = jnp.int32

N_HEADS = 16
HEAD_DIM = 64
N_KV_HEADS = 4
GROUP = N_HEADS // N_KV_HEADS
IDX_HEADS = 8
IDX_DIM = 64
TOPK_MAX = 256
ROPE_THETA = 10000.0
CONV_WIDTH = 3
NORM_EPS = 1e-6
LOG2E = math.log2(math.e)

LANES = 128
SUBLANES = 8
VMEM_LIMIT = 56 * 1024 * 1024

Q_COLS = N_HEADS * HEAD_DIM
KV_COLS = N_KV_HEADS * HEAD_DIM
QI_COLS = IDX_HEADS * IDX_DIM
V_AUG = HEAD_DIM + 16

INT_MIN = np.int32(-2**31)
KEY_NEG_INF = np.int32(np.uint32(0x807FFFFF).astype(np.int64) - 2**32)
KEY_HI_NEG_INF = np.int32(np.uint32(0x807F0000).astype(np.int64) - 2**32)


def _const_spec(shape):
    nd = len(shape)
    return pl.BlockSpec(shape, lambda *_: (0,) * nd, pipeline_mode=pl.Buffered(1))


def _params(n_axes):
    return pltpu.CompilerParams(dimension_semantics=("arbitrary",) * n_axes,
                                vmem_limit_bytes=VMEM_LIMIT)


def _mod_kernel(c_ref, w_ref, b_ref, o_ref):
    c = c_ref[...]
    ca = c * jax.nn.sigmoid(c)
    o_ref[0] = jnp.dot(ca.astype(BF16), w_ref[0].astype(BF16), preferred_element_type=F32) + b_ref[0]


def _modulation(c, ada_w, ada_b):
    depth, d, six_d = ada_w.shape
    bn = c.shape[0]
    rows = 16
    cp = jnp.zeros((rows, d), F32).at[:bn].set(c)
    tn = 1536
    out = pl.pallas_call(
        _mod_kernel,
        grid=(depth, six_d // tn),
        in_specs=[pl.BlockSpec((rows, d), lambda i, j: (0, 0)),
                  pl.BlockSpec((1, d, tn), lambda i, j: (i, 0, j)),
                  pl.BlockSpec((1, 1, tn), lambda i, j: (i, 0, j))],
        out_specs=pl.BlockSpec((1, rows, tn), lambda i, j: (i, 0, j)),
        out_shape=jax.ShapeDtypeStruct((depth, rows, six_d), F32),
        compiler_params=_params(2),
        name="adaln_mod",
    )(cp, ada_w, ada_b.reshape(depth, 1, six_d))
    return out[:, :bn].reshape(depth, bn, 6, d)


def _segmean64(v, gmat):
    hi = v.astype(BF16)
    lo = (v - hi.astype(F32)).astype(BF16)
    s = jnp.dot(hi, gmat, preferred_element_type=F32) + jnp.dot(lo, gmat, preferred_element_type=F32)
    return s * (1.0 / HEAD_DIM)


def _attn_pre_kernel(x_ref, mod_ref, g1_ref, wta_ref, wb_ref, posr_ref, posc_ref, invft_ref, invfr_ref,
                     gq_ref, gk_ref, lng_ref, lnb_ref,
                     qT_ref, qiT_ref, wiT_ref, vT_ref, k4_ref, ki_ref):
    tm = x_ref.shape[0]
    x = x_ref[...]
    sh = mod_ref[0, 0:1, :]
    sc = mod_ref[0, 1:2, :]
    ms = jnp.mean(x * x, axis=-1, keepdims=True)
    h = (x * lax.rsqrt(ms + NORM_EPS) * g1_ref[...]) * (1.0 + sc) + sh
    hb = h.astype(BF16)

    pT = lax.dot_general(wta_ref[...], hb, (((1,), (1,)), ((), ())), preferred_element_type=F32)
    angT = invft_ref[...] * posr_ref[0].astype(F32)
    cT = jnp.cos(angT)
    sT = jnp.sin(angT)
    half = HEAD_DIM // 2

    def rope_t(y):
        x1 = y[:half]
        x2 = y[half:]
        return jnp.concatenate([x1 * cT - x2 * sT, x2 * cT + x1 * sT], axis=0)

    gq = gq_ref[...]
    for hh in range(N_HEADS):
        xq = pT[HEAD_DIM * hh:HEAD_DIM * (hh + 1)]
        inv = lax.rsqrt(jnp.mean(xq * xq, axis=0, keepdims=True) + NORM_EPS)
        y = xq * inv * gq
        qT_ref[0, hh] = (rope_t(y) * (HEAD_DIM ** -0.5 * LOG2E)).astype(BF16)
    o0 = Q_COLS
    ones = jnp.ones((V_AUG - HEAD_DIM, tm), BF16)
    for g in range(N_KV_HEADS):
        vT_ref[0, g, 0, 0:HEAD_DIM, :] = pT[o0 + HEAD_DIM * g:o0 + HEAD_DIM * (g + 1)].astype(BF16)
        vT_ref[0, g, 0, HEAD_DIM:V_AUG, :] = ones
    o0 += KV_COLS
    for hh in range(IDX_HEADS):
        qiT_ref[0, hh] = rope_t(pT[o0 + IDX_DIM * hh:o0 + IDX_DIM * (hh + 1)]).astype(BF16)
    o0 += QI_COLS
    wiT_ref[0] = pT[o0:o0 + IDX_HEADS] * (IDX_HEADS ** -0.5 * IDX_DIM ** -0.5)

    pk = jnp.dot(hb, wb_ref[...], preferred_element_type=F32)
    ang = posc_ref[...].astype(F32) * invfr_ref[...]
    cR = jnp.cos(ang)
    sR = jnp.sin(ang)
    lane = lax.broadcasted_iota(I32, (tm, LANES), 1)
    first = (lane % HEAD_DIM) < half
    s_signed = jnp.where(first, -sR, sR)
    ri = lax.broadcasted_iota(I32, (LANES, LANES), 0) // HEAD_DIM
    ci = lax.broadcasted_iota(I32, (LANES, LANES), 1) // HEAD_DIM
    gmat = jnp.where(ri == ci, 1.0, 0.0).astype(BF16)

    def rope_r(y):
        rot = jnp.where(first, pltpu.roll(y, LANES - half, 1), pltpu.roll(y, half, 1))
        return y * cR + rot * s_signed

    for j in range(KV_COLS // LANES):
        xs = pk[:, LANES * j:LANES * (j + 1)]
        inv = lax.rsqrt(_segmean64(xs * xs, gmat) + NORM_EPS)
        kr = rope_r(xs * inv * gk_ref[...])
        k4_ref[0, 2 * j] = kr[:, 0:HEAD_DIM].astype(BF16)
        k4_ref[0, 2 * j + 1] = kr[:, HEAD_DIM:LANES].astype(BF16)
    t = pk[:, KV_COLS:KV_COLS + LANES]
    mu = _segmean64(t, gmat)
    dlt = t - mu
    var = _segmean64(dlt * dlt, gmat)
    kin = dlt * lax.rsqrt(var + NORM_EPS) * lng_ref[...] + lnb_ref[...]
    ki_ref[...] = rope_r(kin)[:, 0:IDX_DIM].astype(BF16)


def _attn_pre(x2d, mod, g1, w_in, gq, gk, lng, lnb, positions, bn, t_len, tm):
    n, d = x2d.shape
    tpb = t_len // tm
    half = HEAD_DIM // 2
    wq = w_in[:, :Q_COLS]
    wk = w_in[:, Q_COLS:Q_COLS + KV_COLS]
    wv = w_in[:, Q_COLS + KV_COLS:Q_COLS + 2 * KV_COLS]
    o0 = Q_COLS + 2 * KV_COLS
    wqi = w_in[:, o0:o0 + QI_COLS]
    wki = w_in[:, o0 + QI_COLS:o0 + QI_COLS + IDX_DIM]
    wwi = w_in[:, o0 + QI_COLS + IDX_DIM:]
    ma = Q_COLS + KV_COLS + QI_COLS + IDX_HEADS
    ma_pad = -(-ma // 16) * 16
    wta = jnp.concatenate([wq, wv, wqi, wwi, jnp.zeros((d, ma_pad - ma), F32)], axis=1).T.astype(BF16)
    wb = jnp.concatenate([wk, wki, jnp.zeros((d, LANES - IDX_DIM), F32)], axis=1).astype(BF16)
    nb = wb.shape[1]
    inv_freq = ROPE_THETA ** (-(jnp.arange(half, dtype=F32) * 2.0 / HEAD_DIM))
    invft = jnp.broadcast_to(inv_freq[:, None], (half, tm))
    invfr = jnp.tile(inv_freq, LANES // half).reshape(1, LANES)
    gq_b = jnp.broadcast_to(gq.astype(F32)[:, None], (HEAD_DIM, tm))
    gk_r = jnp.tile(gk.astype(F32), LANES // HEAD_DIM).reshape(1, LANES)
    zpad = jnp.zeros((LANES - IDX_DIM,), F32)
    lng_r = jnp.concatenate([lng.astype(F32), zpad]).reshape(1, LANES)
    lnb_r = jnp.concatenate([lnb.astype(F32), zpad]).reshape(1, LANES)
    posr = positions.reshape(bn, 1, t_len)
    posc = positions.reshape(n, 1)

    row = lambda i: (i, 0)
    outs = pl.pallas_call(
        _attn_pre_kernel,
        grid=(n // tm,),
        in_specs=[pl.BlockSpec((tm, d), row),
                  pl.BlockSpec((1, 6, d), lambda i: (i // tpb, 0, 0)),
                  _const_spec((1, d)),
                  _const_spec((ma_pad, d)),
                  _const_spec((d, nb)),
                  pl.BlockSpec((1, 1, tm), lambda i: (i // tpb, 0, i % tpb)),
                  pl.BlockSpec((tm, 1), row),
                  _const_spec((half, tm)),
                  _const_spec((1, LANES)),
                  _const_spec((HEAD_DIM, tm)),
                  _const_spec((1, LANES)),
                  _const_spec((1, LANES)),
                  _const_spec((1, LANES))],
        out_specs=[pl.BlockSpec((1, N_HEADS, HEAD_DIM, tm), lambda i: (i // tpb, 0, 0, i % tpb)),
                   pl.BlockSpec((1, IDX_HEADS, IDX_DIM, tm), lambda i: (i // tpb, 0, 0, i % tpb)),
                   pl.BlockSpec((1, IDX_HEADS, tm), lambda i: (i // tpb, 0, i % tpb)),
                   pl.BlockSpec((1, N_KV_HEADS, 1, V_AUG, tm), lambda i: (i // tpb, 0, i % tpb, 0, 0)),
                   pl.BlockSpec((1, N_KV_HEADS, tm, HEAD_DIM), lambda i: (i // tpb, 0, i % tpb, 0)),
                   pl.BlockSpec((tm, IDX_DIM), row)],
        out_shape=[jax.ShapeDtypeStruct((bn, N_HEADS, HEAD_DIM, t_len), BF16),
                   jax.ShapeDtypeStruct((bn, IDX_HEADS, IDX_DIM, t_len), BF16),
                   jax.ShapeDtypeStruct((bn, IDX_HEADS, t_len), F32),
                   jax.ShapeDtypeStruct((bn, N_KV_HEADS, tpb, V_AUG, tm), BF16),
                   jax.ShapeDtypeStruct((bn, N_KV_HEADS, t_len, HEAD_DIM), BF16),
                   jax.ShapeDtypeStruct((n, IDX_DIM), BF16)],
        compiler_params=_params(1),
        name="attn_pre",
    )(x2d, mod, g1.reshape(1, d), wta, wb, posr, posc, invft, invfr, gq_b, gk_r, lng_r, lnb_r)
    return outs


def _dsa_kernel(qT_ref, qiT_ref, wiT_ref, k4_ref, ki_ref, vT_ref, o_ref,
                st_ref, hi_ref, m_ref, al_ref, acc_ref, bias_ref, lg_ref, left_ref, ltri_ref, *, topk, rb):
    tq = o_ref.shape[1]
    kc = tq
    i = pl.program_id(1)
    n_c = i + 1
    q0 = i * tq

    tcol = q0 + lax.broadcasted_iota(I32, (kc, tq), 1)
    srow0 = lax.broadcasted_iota(I32, (kc, tq), 0)
    wi = wiT_ref[0]

    def score_body(c, carry):
        r0 = pl.multiple_of(c * kc, kc)
        ki = ki_ref[0, pl.ds(r0, kc), :]
        acc = jnp.zeros((kc, tq), F32)
        for hh in range(IDX_HEADS):
            s = jnp.dot(ki, qiT_ref[0, hh], preferred_element_type=F32)
            acc = acc + jnp.maximum(s, 0.0) * wi[hh:hh + 1, :]
        causal = (srow0 + r0) <= tcol
        sc = jnp.where(causal, acc, -jnp.inf)
        st_ref[pl.ds(r0, kc), :] = sc
        top = lax.bitcast_convert_type(sc, I32) & np.int32(-65536)
        hi_ref[pl.ds(r0, kc), :] = lax.bitcast_convert_type(top, F32).astype(BF16)
        return carry

    lax.fori_loop(0, n_c, score_body, 0)

    n_rb = n_c * (kc // rb)
    n_par = 4
    pack = 2 * SUBLANES

    def all_sublanes(tot):
        for sh in (4, 2, 1):
            tot = tot + pltpu.roll(tot, sh, 0)
        return tot

    def count(pred):
        def body(r, acc):
            r0 = pl.multiple_of(r * rb, rb)
            blk = st_ref[pl.ds(r0, rb), :].reshape(rb // SUBLANES, SUBLANES, tq)
            hit = pred(blk, r0).astype(I32).reshape(rb // (SUBLANES * n_par), n_par, SUBLANES, tq)
            return acc + jnp.sum(hit, axis=0)
        acc = lax.fori_loop(0, n_rb, body, jnp.zeros((n_par, SUBLANES, tq), I32))
        return all_sublanes(jnp.sum(acc, axis=0))

    def count_hi(cand16):
        def body(r, acc):
            r0 = pl.multiple_of(r * rb, rb)
            blk = hi_ref[pl.ds(r0, rb), :].reshape(rb // pack, pack, tq)
            hit = jnp.where(blk >= cand16[None], jnp.ones((), BF16), jnp.zeros((), BF16))
            parts = [hit[g] for g in range(rb // pack)]
            while len(parts) > 1:
                parts = [parts[g] + parts[g + 1] for g in range(0, len(parts), 2)]
            return acc + parts[0].astype(F32)
        acc = lax.fori_loop(0, n_rb, body, jnp.zeros((pack, tq), F32))
        return all_sublanes((acc[0:SUBLANES] + acc[SUBLANES:pack]).astype(I32))

    def key_to_float(u):
        ks = u ^ INT_MIN
        bits = jnp.where(ks < 0, ks ^ np.int32(0x7FFFFFFF), ks)
        return ks, bits

    def hi_body(it, carry):
        tau_u, cnt_tau = carry
        cand_u = tau_u | lax.shift_left(np.int32(1), np.int32(31) - it)
        ks, bits = key_to_float(cand_u)
        cand_top = lax.bitcast_convert_type(bits & np.int32(-65536), F32)
        cand16 = jnp.concatenate([cand_top, cand_top], axis=0).astype(BF16)
        cnt = count_hi(cand16)
        take = (cnt >= topk) | (ks < KEY_HI_NEG_INF)
        return jnp.where(take, cand_u, tau_u), jnp.where(take, cnt, cnt_tau)

    def lo_body(it, carry):
        tau_u, cnt_tau = carry
        cand_u = tau_u | lax.shift_left(np.int32(1), np.int32(15) - it)
        ks, bits = key_to_float(cand_u)
        cand_f = lax.bitcast_convert_type(bits, F32)
        cnt = count(lambda blk, r0: blk >= cand_f[None])
        take = (cnt >= topk) | (ks < KEY_NEG_INF)
        return jnp.where(take, cand_u, tau_u), jnp.where(take, cnt, cnt_tau)

    zero8 = jnp.zeros((SUBLANES, tq), I32)
    carry = lax.fori_loop(0, 16, hi_body, (zero8, zero8))
    tau_u, _ = lax.fori_loop(0, 16, lo_body, carry)
    tau8 = lax.bitcast_convert_type(key_to_float(tau_u)[1], F32)
    cnt_gt = count(lambda blk, r0: blk > tau8[None])
    left_ref[...] = (topk - cnt_gt).astype(F32)
    tau1 = tau8[0:1]
    ri = lax.broadcasted_iota(I32, (kc, kc), 0)
    ci = lax.broadcasted_iota(I32, (kc, kc), 1)
    ltri_ref[...] = jnp.where(ci <= ri, 1.0, 0.0).astype(BF16)

    m_ref[...] = jnp.full(m_ref.shape, -1e30, F32)
    acc_ref[...] = jnp.zeros(acc_ref.shape, F32)

    def set_bias(c):
        r0 = pl.multiple_of(c * kc, kc)
        sc = st_ref[pl.ds(r0, kc), :]
        srow = srow0 + r0
        tie = sc == tau1
        seen = jnp.dot(ltri_ref[...], jnp.where(tie, 1.0, 0.0).astype(BF16), preferred_element_type=F32)
        left = left_ref[0:1, :]
        mask = ((sc > tau1) | (tie & (seen <= left))) & (srow <= tcol)
        bias_ref[...] = jnp.where(mask, 0.0, -jnp.inf)
        left_ref[...] = jnp.broadcast_to(left - seen[kc - 1:kc, :], left_ref.shape)

    def logits_stage(c, hh):
        r0 = pl.multiple_of(c * kc, kc)
        kg = k4_ref[0, hh // GROUP, pl.ds(r0, kc), :]
        lg = jnp.dot(kg, qT_ref[0, hh], preferred_element_type=F32) + bias_ref[...]
        lg_ref[hh] = lg
        mx = jnp.max(lg.reshape(kc // SUBLANES, SUBLANES, tq), axis=0)
        m_old = m_ref[hh:hh + 1, :]
        m_new = jnp.maximum(m_old, jnp.max(mx, axis=0, keepdims=True))
        m_ref[hh:hh + 1, :] = m_new
        al_ref[hh:hh + 1, :] = jnp.exp2(m_old - m_new)

    def value_stage(c, hh):
        vt = vT_ref[0, hh // GROUP, c]
        p = jnp.exp2(lg_ref[hh] - m_ref[hh:hh + 1, :]).astype(BF16)
        acc_ref[hh] = acc_ref[hh] * al_ref[hh:hh + 1, :] + jnp.dot(vt, p, preferred_element_type=F32)

    set_bias(0)
    for hh in range(N_HEADS):
        logits_stage(0, hh)

    def attn_body(c, carry):
        set_bias(c)
        for hh in range(N_HEADS):
            value_stage(c - 1, hh)
            logits_stage(c, hh)
        return carry

    lax.fori_loop(1, n_c, attn_body, 0)
    for hh in range(N_HEADS):
        value_stage(n_c - 1, hh)

    for hp in range(N_HEADS // 2):
        parts = []
        for hh in (2 * hp, 2 * hp + 1):
            a = acc_ref[hh]
            parts.append(a[0:HEAD_DIM] / a[HEAD_DIM:HEAD_DIM + 1])
        pair = jnp.concatenate(parts, axis=0)
        o_ref[0, :, LANES * hp:LANES * (hp + 1)] = pair.T.astype(BF16)


def _dsa_attention(qT, qiT, wiT, k4, ki, vT, bn, t_len, topk):
    tq = topk
    rb = 256
    kern = functools.partial(_dsa_kernel, topk=topk, rb=rb)
    return pl.pallas_call(
        kern,
        grid=(bn, t_len // tq),
        in_specs=[pl.BlockSpec((1, N_HEADS, HEAD_DIM, tq), lambda b, i: (b, 0, 0, i)),
                  pl.BlockSpec((1, IDX_HEADS, IDX_DIM, tq), lambda b, i: (b, 0, 0, i)),
                  pl.BlockSpec((1, IDX_HEADS, tq), lambda b, i: (b, 0, i)),
                  pl.BlockSpec((1, N_KV_HEADS, t_len, HEAD_DIM), lambda b, i: (b, 0, 0, 0),
                               pipeline_mode=pl.Buffered(1)),
                  pl.BlockSpec((1, t_len, IDX_DIM), lambda b, i: (b, 0, 0), pipeline_mode=pl.Buffered(1)),
                  pl.BlockSpec((1, N_KV_HEADS, t_len // tq, V_AUG, tq), lambda b, i: (b, 0, 0, 0, 0),
                               pipeline_mode=pl.Buffered(1))],
        out_specs=pl.BlockSpec((1, tq, Q_COLS), lambda b, i: (b, i, 0)),
        out_shape=jax.ShapeDtypeStruct((bn, t_len, Q_COLS), BF16),
        scratch_shapes=[pltpu.VMEM((t_len, tq), F32),
                        pltpu.VMEM((t_len, tq), BF16),
                        pltpu.VMEM((N_HEADS, tq), F32),
                        pltpu.VMEM((N_HEADS, tq), F32),
                        pltpu.VMEM((N_HEADS, V_AUG, tq), F32),
                        pltpu.VMEM((tq, tq), F32),
                        pltpu.VMEM((N_HEADS, tq, tq), F32),
                        pltpu.VMEM((SUBLANES, tq), F32),
                        pltpu.VMEM((tq, tq), BF16)],
        compiler_params=_params(2),
        name="dsa_attention",
    )(qT, qiT, wiT, k4, ki.reshape(bn, t_len, IDX_DIM), vT)


def _conv_pre_kernel(x_ref, mod_ref, g1_ref, w_ref, cw_ref, o_ref, zbuf_ref, *, tpb):
    tm, d = x_ref.shape
    i = pl.program_id(0)

    @pl.when(i % tpb == 0)
    def _():
        zbuf_ref[0:SUBLANES, :] = jnp.zeros((SUBLANES, d), F32)

    x = x_ref[...]
    sh = mod_ref[0, 0:1, :]
    sc = mod_ref[0, 1:2, :]
    ms = jnp.mean(x * x, axis=-1, keepdims=True)
    h = (x * lax.rsqrt(ms + NORM_EPS) * g1_ref[...]) * (1.0 + sc) + sh
    proj = jnp.dot(h.astype(BF16), w_ref[...], preferred_element_type=F32)
    b_gate = proj[:, 0:d]
    z = proj[:, d:2 * d] * proj[:, 2 * d:3 * d]
    zbuf_ref[SUBLANES:SUBLANES + tm, :] = z
    z1 = zbuf_ref[SUBLANES - 1:SUBLANES - 1 + tm, :]
    z2 = zbuf_ref[SUBLANES - 2:SUBLANES - 2 + tm, :]
    zc = cw_ref[0:1, :] * z2 + cw_ref[1:2, :] * z1 + cw_ref[2:3, :] * z
    o_ref[...] = (b_gate * zc).astype(BF16)
    zbuf_ref[0:SUBLANES, :] = z[tm - SUBLANES:tm, :]


def _conv_pre(x2d, mod, g1, w_in, conv_w, t_len, tm):
    n, d = x2d.shape
    tpb = t_len // tm
    row = lambda i: (i, 0)
    return pl.pallas_call(
        functools.partial(_conv_pre_kernel, tpb=tpb),
        grid=(n // tm,),
        in_specs=[pl.BlockSpec((tm, d), row),
                  pl.BlockSpec((1, 6, d), lambda i: (i // tpb, 0, 0)),
                  _const_spec((1, d)),
                  _const_spec((d, 3 * d)),
                  _const_spec((CONV_WIDTH, d))],
        out_specs=pl.BlockSpec((tm, d), row),
        out_shape=jax.ShapeDtypeStruct((n, d), BF16),
        scratch_shapes=[pltpu.VMEM((tm + SUBLANES, d), F32)],
        compiler_params=_params(1),
        name="conv_pre",
    )(x2d, mod, g1.reshape(1, d), w_in.astype(BF16), conv_w.astype(F32))


def _post_kernel(x_ref, mix_ref, mod_ref, g2_ref, wo_ref, wg_ref, wu_ref, wd_ref, o_ref, *, th):
    gate1 = mod_ref[0, 2:3, :]
    sh2 = mod_ref[0, 3:4, :]
    sc2 = mod_ref[0, 4:5, :]
    gate2 = mod_ref[0, 5:6, :]
    y = jnp.dot(mix_ref[...], wo_ref[...], preferred_element_type=F32)
    x1 = x_ref[...] + gate1 * y
    ms = jnp.mean(x1 * x1, axis=-1, keepdims=True)
    h = ((x1 * lax.rsqrt(ms + NORM_EPS) * g2_ref[...]) * (1.0 + sc2) + sh2).astype(BF16)
    hidden = wg_ref.shape[1]
    acc = jnp.zeros(x1.shape, F32)
    for j in range(hidden // th):
        gt = jnp.dot(h, wg_ref[:, th * j:th * (j + 1)], preferred_element_type=F32)
        up = jnp.dot(h, wu_ref[:, th * j:th * (j + 1)], preferred_element_type=F32)
        a = (gt * jax.nn.sigmoid(gt)) * up
        acc = acc + jnp.dot(a.astype(BF16), wd_ref[th * j:th * (j + 1), :], preferred_element_type=F32)
    o_ref[...] = x1 + gate2 * acc


def _post(x2d, mix, mod, g2, w_out, w_gate, w_up, w_down, t_len, tm):
    n, d = x2d.shape
    hidden = w_gate.shape[1]
    tpb = t_len // tm
    row = lambda i: (i, 0)
    return pl.pallas_call(
        functools.partial(_post_kernel, th=256),
        grid=(n // tm,),
        in_specs=[pl.BlockSpec((tm, d), row),
                  pl.BlockSpec((tm, d), row),
                  pl.BlockSpec((1, 6, d), lambda i: (i // tpb, 0, 0)),
                  _const_spec((1, d)),
                  _const_spec((d, d)),
                  _const_spec((d, hidden)),
                  _const_spec((d, hidden)),
                  _const_spec((hidden, d))],
        out_specs=pl.BlockSpec((tm, d), row),
        out_shape=jax.ShapeDtypeStruct((n, d), F32),
        compiler_params=_params(1),
        name="mixer_out_ffn",
    )(x2d, mix, mod, g2.reshape(1, d), w_out.astype(BF16), w_gate.astype(BF16), w_up.astype(BF16),
      w_down.astype(BF16))


def kernel(x, c, positions, ada_w, ada_b, norm1_g, norm2_g, attn_w_in, attn_q_norm_g, attn_k_norm_g,
           idx_k_ln_g, idx_k_ln_b, attn_w_out, conv_w_in, conv_w, conv_w_out, ffn_w_gate, ffn_w_up,
           ffn_w_down):
    bn, t_len, d = x.shape
    depth = ada_w.shape[0]
    topk = min(TOPK_MAX, t_len // 4)
    assert topk == TOPK_MAX and t_len % topk == 0 and (t_len & (t_len - 1)) == 0
    n = bn * t_len
    tm_pre = 256
    tm_post = 512
    mod = _modulation(c, ada_w, ada_b)
    x2d = x.reshape(n, d)
    for i in range(depth):
        j = i // 2
        if i % 2 == 0:
            qT, qiT, wiT, vT, k2, kiP = _attn_pre(
                x2d, mod[i], norm1_g[i], attn_w_in[j], attn_q_norm_g[j], attn_k_norm_g[j],
                idx_k_ln_g[j], idx_k_ln_b[j], positions, bn, t_len, tm_pre)
            mix = _dsa_attention(qT, qiT, wiT, k2, kiP, vT, bn, t_len, topk).reshape(n, d)
            w_mix_out = attn_w_out[j]
        else:
            mix = _conv_pre(x2d, mod[i], norm1_g[i], conv_w_in[j], conv_w[j], t_len, tm_post)
            w_mix_out = conv_w_out[j]
        x2d = _post(x2d, mix, mod[i], norm2_g[i], w_mix_out, ffn_w_gate[i], ffn_w_up[i], ffn_w_down[i],
                    t_len, tm_post)
    return x2d.reshape(bn, t_len, d)
```

```python
import functools
import math

import numpy as np
import jax
import jax.numpy as jnp
from jax import lax
from jax.experimental import pallas as pl
from jax.experimental.pallas import tpu as pltpu

F32 = jnp.float32
BF16 = jnp.bfloat16
I32 = jnp.int32

N_HEADS = 16
HEAD_DIM = 64
N_KV_HEADS = 4
GROUP = N_HEADS // N_KV_HEADS
IDX_HEADS = 8
IDX_DIM = 64
TOPK_MAX = 256
ROPE_THETA = 10000.0
CONV_WIDTH = 3
NORM_EPS = 1e-6
LOG2E = math.log2(math.e)

LANES = 128
SUBLANES = 8
VMEM_LIMIT = 56 * 1024 * 1024

Q_COLS = N_HEADS * HEAD_DIM
KV_COLS = N_KV_HEADS * HEAD_DIM
QI_COLS = IDX_HEADS * IDX_DIM
V_AUG = HEAD_DIM + 16

INT_MIN = np.int32(-2**31)
KEY_NEG_INF = np.int32(np.uint32(0x807FFFFF).astype(np.int64) - 2**32)
KEY_HI_NEG_INF = np.int32(np.uint32(0x807F0000).astype(np.int64) - 2**32)


def _const_spec(shape):
    nd = len(shape)
    return pl.BlockSpec(shape, lambda *_: (0,) * nd, pipeline_mode=pl.Buffered(1))


def _params(n_axes):
    return pltpu.CompilerParams(dimension_semantics=("arbitrary",) * n_axes,
                                vmem_limit_bytes=VMEM_LIMIT)


def _mod_kernel(c_ref, w_ref, b_ref, o_ref):
    c = c_ref[...]
    ca = c * jax.nn.sigmoid(c)
    o_ref[0] = jnp.dot(ca.astype(BF16), w_ref[0].astype(BF16), preferred_element_type=F32) + b_ref[0]


def _modulation(c, ada_w, ada_b):
    depth, d, six_d = ada_w.shape
    bn = c.shape[0]
    rows = 16
    cp = jnp.zeros((rows, d), F32).at[:bn].set(c)
    tn = 1536
    out = pl.pallas_call(
        _mod_kernel,
        grid=(depth, six_d // tn),
        in_specs=[pl.BlockSpec((rows, d), lambda i, j: (0, 0)),
                  pl.BlockSpec((1, d, tn), lambda i, j: (i, 0, j)),
                  pl.BlockSpec((1, 1, tn), lambda i, j: (i, 0, j))],
        out_specs=pl.BlockSpec((1, rows, tn), lambda i, j: (i, 0, j)),
        out_shape=jax.ShapeDtypeStruct((depth, rows, six_d), F32),
        compiler_params=_params(2),
        name="adaln_mod",
    )(cp, ada_w, ada_b.reshape(depth, 1, six_d))
    return out[:, :bn].reshape(depth, bn, 6, d)


def _segmean64(v, gmat):
    hi = v.astype(BF16)
    lo = (v - hi.astype(F32)).astype(BF16)
    s = jnp.dot(hi, gmat, preferred_element_type=F32) + jnp.dot(lo, gmat, preferred_element_type=F32)
    return s * (1.0 / HEAD_DIM)


def _attn_pre_kernel(x_ref, mod_ref, g1_ref, wta_ref, wb_ref, posr_ref, posc_ref, invft_ref, invfr_ref,
                     gq_ref, gk_ref, lng_ref, lnb_ref,
                     qT_ref, qiT_ref, wiT_ref, vT_ref, k4_ref, ki_ref):
    tm = x_ref.shape[0]
    x = x_ref[...]
    sh = mod_ref[0, 0:1, :]
    sc = mod_ref[0, 1:2, :]
    ms = jnp.mean(x * x, axis=-1, keepdims=True)
    h = (x * lax.rsqrt(ms + NORM_EPS) * g1_ref[...]) * (1.0 + sc) + sh
    hb = h.astype(BF16)

    pT = lax.dot_general(wta_ref[...], hb, (((1,), (1,)), ((), ())), preferred_element_type=F32)
    angT = invft_ref[...] * posr_ref[0].astype(F32)
    cT = jnp.cos(angT)
    sT = jnp.sin(angT)
    half = HEAD_DIM // 2

    def rope_t(y):
        x1 = y[:half]
        x2 = y[half:]
        return jnp.concatenate([x1 * cT - x2 * sT, x2 * cT + x1 * sT], axis=0)

    gq = gq_ref[...]
    for hh in range(N_HEADS):
        xq = pT[HEAD_DIM * hh:HEAD_DIM * (hh + 1)]
        inv = lax.rsqrt(jnp.mean(xq * xq, axis=0, keepdims=True) + NORM_EPS)
        y = xq * inv * gq
        qT_ref[0, hh] = (rope_t(y) * (HEAD_DIM ** -0.5 * LOG2E)).astype(BF16)
    o0 = Q_COLS
    ones = jnp.ones((V_AUG - HEAD_DIM, tm), BF16)
    for g in range(N_KV_HEADS):
        vT_ref[0, g, 0, 0:HEAD_DIM, :] = pT[o0 + HEAD_DIM * g:o0 + HEAD_DIM * (g + 1)].astype(BF16)
        vT_ref[0, g, 0, HEAD_DIM:V_AUG, :] = ones
    o0 += KV_COLS
    for hh in range(IDX_HEADS):
        qiT_ref[0, hh] = rope_t(pT[o0 + IDX_DIM * hh:o0 + IDX_DIM * (hh + 1)]).astype(BF16)
    o0 += QI_COLS
    wiT_ref[0] = pT[o0:o0 + IDX_HEADS] * (IDX_HEADS ** -0.5 * IDX_DIM ** -0.5)

    pk = jnp.dot(hb, wb_ref[...], preferred_element_type=F32)
    ang = posc_ref[...].astype(F32) * invfr_ref[...]
    cR = jnp.cos(ang)
    sR = jnp.sin(ang)
    lane = lax.broadcasted_iota(I32, (tm, LANES), 1)
    first = (lane % HEAD_DIM) < half
    s_signed = jnp.where(first, -sR, sR)
    ri = lax.broadcasted_iota(I32, (LANES, LANES), 0) // HEAD_DIM
    ci = lax.broadcasted_iota(I32, (LANES, LANES), 1) // HEAD_DIM
    gmat = jnp.where(ri == ci, 1.0, 0.0).astype(BF16)

    def rope_r(y):
        rot = jnp.where(first, pltpu.roll(y, LANES - half, 1), pltpu.roll(y, half, 1))
        return y * cR + rot * s_signed

    for j in range(KV_COLS // LANES):
        xs = pk[:, LANES * j:LANES * (j + 1)]
        inv = lax.rsqrt(_segmean64(xs * xs, gmat) + NORM_EPS)
        kr = rope_r(xs * inv * gk_ref[...])
        k4_ref[0, 2 * j] = kr[:, 0:HEAD_DIM].astype(BF16)
        k4_ref[0, 2 * j + 1] = kr[:, HEAD_DIM:LANES].astype(BF16)
    t = pk[:, KV_COLS:KV_COLS + LANES]
    mu = _segmean64(t, gmat)
    dlt = t - mu
    var = _segmean64(dlt * dlt, gmat)
    kin = dlt * lax.rsqrt(var + NORM_EPS) * lng_ref[...] + lnb_ref[...]
    ki_ref[...] = rope_r(kin)[:, 0:IDX_DIM].astype(BF16)


def _attn_pre(x2d, mod, g1, w_in, gq, gk, lng, lnb, positions, bn, t_len, tm):
    n, d = x2d.shape
    tpb = t_len // tm
    half = HEAD_DIM // 2
    wq = w_in[:, :Q_COLS]
    wk = w_in[:, Q_COLS:Q_COLS + KV_COLS]
    wv = w_in[:, Q_COLS + KV_COLS:Q_COLS + 2 * KV_COLS]
    o0 = Q_COLS + 2 * KV_COLS
    wqi = w_in[:, o0:o0 + QI_COLS]
    wki = w_in[:, o0 + QI_COLS:o0 + QI_COLS + IDX_DIM]
    wwi = w_in[:, o0 + QI_COLS + IDX_DIM:]
    ma = Q_COLS + KV_COLS + QI_COLS + IDX_HEADS
    ma_pad = -(-ma // 16) * 16
    wta = jnp.concatenate([wq, wv, wqi, wwi, jnp.zeros((d, ma_pad - ma), F32)], axis=1).T.astype(BF16)
    wb = jnp.concatenate([wk, wki, jnp.zeros((d, LANES - IDX_DIM), F32)], axis=1).astype(BF16)
    nb = wb.shape[1]
    inv_freq = ROPE_THETA ** (-(jnp.arange(half, dtype=F32) * 2.0 / HEAD_DIM))
    invft = jnp.broadcast_to(inv_freq[:, None], (half, tm))
    invfr = jnp.tile(inv_freq, LANES // half).reshape(1, LANES)
    gq_b = jnp.broadcast_to(gq.astype(F32)[:, None], (HEAD_DIM, tm))
    gk_r = jnp.tile(gk.astype(F32), LANES // HEAD_DIM).reshape(1, LANES)
    zpad = jnp.zeros((LANES - IDX_DIM,), F32)
    lng_r = jnp.concatenate([lng.astype(F32), zpad]).reshape(1, LANES)
    lnb_r = jnp.concatenate([lnb.astype(F32), zpad]).reshape(1, LANES)
    posr = positions.reshape(bn, 1, t_len)
    posc = positions.reshape(n, 1)

    row = lambda i: (i, 0)
    outs = pl.pallas_call(
        _attn_pre_kernel,
        grid=(n // tm,),
        in_specs=[pl.BlockSpec((tm, d), row),
                  pl.BlockSpec((1, 6, d), lambda i: (i // tpb, 0, 0)),
                  _const_spec((1, d)),
                  _const_spec((ma_pad, d)),
                  _const_spec((d, nb)),
                  pl.BlockSpec((1, 1, tm), lambda i: (i // tpb, 0, i % tpb)),
                  pl.BlockSpec((tm, 1), row),
                  _const_spec((half, tm)),
                  _const_spec((1, LANES)),
                  _const_spec((HEAD_DIM, tm)),
                  _const_spec((1, LANES)),
                  _const_spec((1, LANES)),
                  _const_spec((1, LANES))],
        out_specs=[pl.BlockSpec((1, N_HEADS, HEAD_DIM, tm), lambda i: (i // tpb, 0, 0, i % tpb)),
                   pl.BlockSpec((1, IDX_HEADS, IDX_DIM, tm), lambda i: (i // tpb, 0, 0, i % tpb)),
                   pl.BlockSpec((1, IDX_HEADS, tm), lambda i: (i // tpb, 0, i % tpb)),
                   pl.BlockSpec((1, N_KV_HEADS, 1, V_AUG, tm), lambda i: (i // tpb, 0, i % tpb, 0, 0)),
                   pl.BlockSpec((1, N_KV_HEADS, tm, HEAD_DIM), lambda i: (i // tpb, 0, i % tpb, 0)),
                   pl.BlockSpec((tm, IDX_DIM), row)],
        out_shape=[jax.ShapeDtypeStruct((bn, N_HEADS, HEAD_DIM, t_len), BF16),
                   jax.ShapeDtypeStruct((bn, IDX_HEADS, IDX_DIM, t_len), BF16),
                   jax.ShapeDtypeStruct((bn, IDX_HEADS, t_len), F32),
                   jax.ShapeDtypeStruct((bn, N_KV_HEADS, tpb, V_AUG, tm), BF16),
                   jax.ShapeDtypeStruct((bn, N_KV_HEADS, t_len, HEAD_DIM), BF16),
                   jax.ShapeDtypeStruct((n, IDX_DIM), BF16)],
        compiler_params=_params(1),
        name="attn_pre",
    )(x2d, mod, g1.reshape(1, d), wta, wb, posr, posc, invft, invfr, gq_b, gk_r, lng_r, lnb_r)
    return outs


def _dsa_kernel(qT_ref, qiT_ref, wiT_ref, k4_ref, ki_ref, vT_ref, o_ref,
                st_ref, hi_ref, m_ref, al_ref, acc_ref, bias_ref, lg_ref, left_ref, ltri_ref, *, topk, rb, ka):
    tq = o_ref.shape[1]
    kc = tq
    i = pl.program_id(1)
    n_c = i + 1
    q0 = i * tq

    tcol = q0 + lax.broadcasted_iota(I32, (kc, tq), 1)
    srow0 = lax.broadcasted_iota(I32, (kc, tq), 0)
    wi = wiT_ref[0]
    qi_all = jnp.concatenate([qiT_ref[0, hh] for hh in range(IDX_HEADS)], axis=1)

    def score_chunk(c):
        r0 = pl.multiple_of(c * kc, kc)
        ki = ki_ref[0, pl.ds(r0, kc), :]
        s_all = jnp.dot(ki, qi_all, preferred_element_type=F32)
        acc = jnp.zeros((kc, tq), F32)
        for hh in range(IDX_HEADS):
            acc = acc + jnp.maximum(s_all[:, tq * hh:tq * (hh + 1)], 0.0) * wi[hh:hh + 1, :]
        causal = (srow0 + r0) <= tcol
        sc = jnp.where(causal, acc, -jnp.inf)
        st_ref[pl.ds(r0, kc), :] = sc
        top = lax.bitcast_convert_type(sc, I32) & np.int32(-65536)
        hi_ref[pl.ds(r0, kc), :] = lax.bitcast_convert_type(top, F32).astype(BF16)

    def score_body(j, carry):
        score_chunk(2 * j)
        score_chunk(2 * j + 1)
        return carry

    lax.fori_loop(0, (n_c + 1) // 2, score_body, 0)

    n_rb = (n_c * kc + rb - 1) // rb
    n_par = 4
    pack = 2 * SUBLANES

    def all_sublanes(tot):
        for sh in (4, 2, 1):
            tot = tot + pltpu.roll(tot, sh, 0)
        return tot

    def count(pred):
        def body(r, acc):
            r0 = pl.multiple_of(r * rb, rb)
            blk = st_ref[pl.ds(r0, rb), :].reshape(rb // SUBLANES, SUBLANES, tq)
            hit = pred(blk, r0).astype(I32).reshape(rb // (SUBLANES * n_par), n_par, SUBLANES, tq)
            return acc + jnp.sum(hit, axis=0)
        acc = lax.fori_loop(0, n_rb, body, jnp.zeros((n_par, SUBLANES, tq), I32))
        return all_sublanes(jnp.sum(acc, axis=0))

    def count_hi(cand16):
        def body(r, acc):
            r0 = pl.multiple_of(r * rb, rb)
            blk = hi_ref[pl.ds(r0, rb), :].reshape(rb // pack, pack, tq)
            hit = jnp.where(blk >= cand16[None], jnp.ones((), BF16), jnp.zeros((), BF16))
            parts = [hit[g] for g in range(rb // pack)]
            while len(parts) > 1:
                parts = [parts[g] + parts[g + 1] for g in range(0, len(parts), 2)]
            return acc + parts[0].astype(F32)
        acc = lax.fori_loop(0, n_rb, body, jnp.zeros((pack, tq), F32))
        return all_sublanes((acc[0:SUBLANES] + acc[SUBLANES:pack]).astype(I32))

    def key_to_float(u):
        ks = u ^ INT_MIN
        bits = jnp.where(ks < 0, ks ^ np.int32(0x7FFFFFFF), ks)
        return ks, bits

    def hi_body(it, carry):
        tau_u, cnt_tau = carry
        cand_u = tau_u | lax.shift_left(np.int32(1), np.int32(31) - it)
        ks, bits = key_to_float(cand_u)
        cand_top = lax.bitcast_convert_type(bits & np.int32(-65536), F32)
        cand16 = jnp.concatenate([cand_top, cand_top], axis=0).astype(BF16)
        cnt = count_hi(cand16)
        take = (cnt >= topk) | (ks < KEY_HI_NEG_INF)
        return jnp.where(take, cand_u, tau_u), jnp.where(take, cnt, cnt_tau)

    def lo_body(it, carry):
        tau_u, cnt_tau = carry
        cand_u = tau_u | lax.shift_left(np.int32(1), np.int32(15) - it)
        ks, bits = key_to_float(cand_u)
        cand_f = lax.bitcast_convert_type(bits, F32)
        cnt = count(lambda blk, r0: blk >= cand_f[None])
        take = (cnt >= topk) | (ks < KEY_NEG_INF)
        return jnp.where(take, cand_u, tau_u), jnp.where(take, cnt, cnt_tau)

    zero8 = jnp.zeros((SUBLANES, tq), I32)
    carry = lax.fori_loop(0, 16, hi_body, (zero8, zero8))
    tau_u, _ = lax.fori_loop(0, 16, lo_body, carry)
    tau8 = lax.bitcast_convert_type(key_to_float(tau_u)[1], F32)
    cnt_gt = count(lambda blk, r0: blk > tau8[None])
    left_ref[...] = (topk - cnt_gt).astype(F32)
    tau1 = tau8[0:1]
    ri = lax.broadcasted_iota(I32, (kc, kc), 0)
    ci = lax.broadcasted_iota(I32, (kc, kc), 1)
    ltri_ref[...] = jnp.where(ci <= ri, 1.0, 0.0).astype(BF16)

    m_ref[...] = jnp.full(m_ref.shape, -1e30, F32)
    acc_ref[...] = jnp.zeros(acc_ref.shape, F32)

    sub = ka // kc
    n_a = (n_c + sub - 1) // sub

    def set_bias(c):
        for u in range(sub):
            r0 = pl.multiple_of(c * ka + u * kc, kc)
            sc = st_ref[pl.ds(r0, kc), :]
            srow = srow0 + r0
            tie = sc == tau1
            seen = jnp.dot(ltri_ref[...], jnp.where(tie, 1.0, 0.0).astype(BF16), preferred_element_type=F32)
            left = left_ref[0:1, :]
            mask = ((sc > tau1) | (tie & (seen <= left))) & (srow <= tcol)
            bias_ref[u * kc:(u + 1) * kc, :] = jnp.where(mask, 0.0, -jnp.inf)
            left_ref[...] = jnp.broadcast_to(left - seen[kc - 1:kc, :], left_ref.shape)

    def logits_stage(c, hh):
        r0 = pl.multiple_of(c * ka, ka)
        kg = k4_ref[0, hh // GROUP, pl.ds(r0, ka), :]
        lg = jnp.dot(kg, qT_ref[0, hh], preferred_element_type=F32) + bias_ref[...]
        lg_ref[hh] = lg
        mx = jnp.max(lg.reshape(ka // SUBLANES, SUBLANES, tq), axis=0)
        m_old = m_ref[hh:hh + 1, :]
        m_new = jnp.maximum(m_old, jnp.max(mx, axis=0, keepdims=True))
        m_ref[hh:hh + 1, :] = m_new
        al_ref[hh:hh + 1, :] = jnp.exp2(m_old - m_new)

    def value_stage(c, hh):
        vt = jnp.concatenate([vT_ref[0, hh // GROUP, c * sub + u] for u in range(sub)], axis=1)
        p = jnp.exp2(lg_ref[hh] - m_ref[hh:hh + 1, :]).astype(BF16)
        acc_ref[hh] = acc_ref[hh] * al_ref[hh:hh + 1, :] + jnp.dot(vt, p, preferred_element_type=F32)

    set_bias(0)
    for hh in range(N_HEADS):
        logits_stage(0, hh)

    def attn_body(c, carry):
        set_bias(c)
        for hh in range(N_HEADS):
            value_stage(c - 1, hh)
            logits_stage(c, hh)
        return carry

    lax.fori_loop(1, n_a, attn_body, 0)
    for hh in range(N_HEADS):
        value_stage(n_a - 1, hh)

    for hp in range(N_HEADS // 2):
        parts = []
        for hh in (2 * hp, 2 * hp + 1):
            a = acc_ref[hh]
            parts.append(a[0:HEAD_DIM] / a[HEAD_DIM:HEAD_DIM + 1])
        pair = jnp.concatenate(parts, axis=0)
        o_ref[0, :, LANES * hp:LANES * (hp + 1)] = pair.T.astype(BF16)


def _dsa_attention(qT, qiT, wiT, k4, ki, vT, bn, t_len, topk):
    tq = topk
    rb = 512
    ka = 256
    kern = functools.partial(_dsa_kernel, topk=topk, rb=rb, ka=ka)
    return pl.pallas_call(
        kern,
        grid=(bn, t_len // tq),
        in_specs=[pl.BlockSpec((1, N_HEADS, HEAD_DIM, tq), lambda b, i: (b, 0, 0, i)),
                  pl.BlockSpec((1, IDX_HEADS, IDX_DIM, tq), lambda b, i: (b, 0, 0, i)),
                  pl.BlockSpec((1, IDX_HEADS, tq), lambda b, i: (b, 0, i)),
                  pl.BlockSpec((1, N_KV_HEADS, t_len, HEAD_DIM), lambda b, i: (b, 0, 0, 0),
                               pipeline_mode=pl.Buffered(1)),
                  pl.BlockSpec((1, t_len, IDX_DIM), lambda b, i: (b, 0, 0), pipeline_mode=pl.Buffered(1)),
                  pl.BlockSpec((1, N_KV_HEADS, t_len // tq, V_AUG, tq), lambda b, i: (b, 0, 0, 0, 0),
                               pipeline_mode=pl.Buffered(1))],
        out_specs=pl.BlockSpec((1, tq, Q_COLS), lambda b, i: (b, i, 0)),
        out_shape=jax.ShapeDtypeStruct((bn, t_len, Q_COLS), BF16),
        scratch_shapes=[pltpu.VMEM((t_len, tq), F32),
                        pltpu.VMEM((t_len, tq), BF16),
                        pltpu.VMEM((N_HEADS, tq), F32),
                        pltpu.VMEM((N_HEADS, tq), F32),
                        pltpu.VMEM((N_HEADS, V_AUG, tq), F32),
                        pltpu.VMEM((ka, tq), F32),
                        pltpu.VMEM((N_HEADS, ka, tq), F32),
                        pltpu.VMEM((SUBLANES, tq), F32),
                        pltpu.VMEM((tq, tq), BF16)],
        compiler_params=_params(2),
        name="dsa_attention",
    )(qT, qiT, wiT, k4, ki.reshape(bn, t_len, IDX_DIM), vT)


def _conv_pre_kernel(x_ref, mod_ref, g1_ref, w_ref, cw_ref, o_ref, zbuf_ref, *, tpb):
    tm, d = x_ref.shape
    i = pl.program_id(0)

    @pl.when(i % tpb == 0)
    def _():
        zbuf_ref[0:SUBLANES, :] = jnp.zeros((SUBLANES, d), F32)

    x = x_ref[...]
    sh = mod_ref[0, 0:1, :]
    sc = mod_ref[0, 1:2, :]
    ms = jnp.mean(x * x, axis=-1, keepdims=True)
    h = (x * lax.rsqrt(ms + NORM_EPS) * g1_ref[...]) * (1.0 + sc) + sh
    proj = jnp.dot(h.astype(BF16), w_ref[...], preferred_element_type=F32)
    b_gate = proj[:, 0:d]
    z = proj[:, d:2 * d] * proj[:, 2 * d:3 * d]
    zbuf_ref[SUBLANES:SUBLANES + tm, :] = z
    z1 = zbuf_ref[SUBLANES - 1:SUBLANES - 1 + tm, :]
    z2 = zbuf_ref[SUBLANES - 2:SUBLANES - 2 + tm, :]
    zc = cw_ref[0:1, :] * z2 + cw_ref[1:2, :] * z1 + cw_ref[2:3, :] * z
    o_ref[...] = (b_gate * zc).astype(BF16)
    zbuf_ref[0:SUBLANES, :] = z[tm - SUBLANES:tm, :]


def _conv_pre(x2d, mod, g1, w_in, conv_w, t_len, tm):
    n, d = x2d.shape
    tpb = t_len // tm
    row = lambda i: (i, 0)
    return pl.pallas_call(
        functools.partial(_conv_pre_kernel, tpb=tpb),
        grid=(n // tm,),
        in_specs=[pl.BlockSpec((tm, d), row),
                  pl.BlockSpec((1, 6, d), lambda i: (i // tpb, 0, 0)),
                  _const_spec((1, d)),
                  _const_spec((d, 3 * d)),
                  _const_spec((CONV_WIDTH, d))],
        out_specs=pl.BlockSpec((tm, d), row),
        out_shape=jax.ShapeDtypeStruct((n, d), BF16),
        scratch_shapes=[pltpu.VMEM((tm + SUBLANES, d), F32)],
        compiler_params=_params(1),
        name="conv_pre",
    )(x2d, mod, g1.reshape(1, d), w_in.astype(BF16), conv_w.astype(F32))


def _post_kernel(x_ref, mix_ref, mod_ref, g2_ref, wo_ref, wg_ref, wu_ref, wd_ref, o_ref, *, th):
    gate1 = mod_ref[0, 2:3, :]
    sh2 = mod_ref[0, 3:4, :]
    sc2 = mod_ref[0, 4:5, :]
    gate2 = mod_ref[0, 5:6, :]
    y = jnp.dot(mix_ref[...], wo_ref[...], preferred_element_type=F32)
    x1 = x_ref[...] + gate1 * y
    ms = jnp.mean(x1 * x1, axis=-1, keepdims=True)
    h = ((x1 * lax.rsqrt(ms + NORM_EPS) * g2_ref[...]) * (1.0 + sc2) + sh2).astype(BF16)
    hidden = wg_ref.shape[1]
    acc = jnp.zeros(x1.shape, F32)
    for j in range(hidden // th):
        gt = jnp.dot(h, wg_ref[:, th * j:th * (j + 1)], preferred_element_type=F32)
        up = jnp.dot(h, wu_ref[:, th * j:th * (j + 1)], preferred_element_type=F32)
        a = (gt * jax.nn.sigmoid(gt)) * up
        acc = acc + jnp.dot(a.astype(BF16), wd_ref[th * j:th * (j + 1), :], preferred_element_type=F32)
    o_ref[...] = x1 + gate2 * acc


def _post(x2d, mix, mod, g2, w_out, w_gate, w_up, w_down, t_len, tm):
    n, d = x2d.shape
    hidden = w_gate.shape[1]
    tpb = t_len // tm
    row = lambda i: (i, 0)
    return pl.pallas_call(
        functools.partial(_post_kernel, th=256),
        grid=(n // tm,),
        in_specs=[pl.BlockSpec((tm, d), row),
                  pl.BlockSpec((tm, d), row),
                  pl.BlockSpec((1, 6, d), lambda i: (i // tpb, 0, 0)),
                  _const_spec((1, d)),
                  _const_spec((d, d)),
                  _const_spec((d, hidden)),
                  _const_spec((d, hidden)),
                  _const_spec((hidden, d))],
        out_specs=pl.BlockSpec((tm, d), row),
        out_shape=jax.ShapeDtypeStruct((n, d), F32),
        compiler_params=_params(1),
        name="mixer_out_ffn",
    )(x2d, mix, mod, g2.reshape(1, d), w_out.astype(BF16), w_gate.astype(BF16), w_up.astype(BF16),
      w_down.astype(BF16))


def kernel(x, c, positions, ada_w, ada_b, norm1_g, norm2_g, attn_w_in, attn_q_norm_g, attn_k_norm_g,
           idx_k_ln_g, idx_k_ln_b, attn_w_out, conv_w_in, conv_w, conv_w_out, ffn_w_gate, ffn_w_up,
           ffn_w_down):
    bn, t_len, d = x.shape
    depth = ada_w.shape[0]
    topk = min(TOPK_MAX, t_len // 4)
    assert topk == TOPK_MAX and t_len % topk == 0 and (t_len & (t_len - 1)) == 0
    n = bn * t_len
    tm_pre = 256
    tm_post = 512
    mod = _modulation(c, ada_w, ada_b)
    x2d = x.reshape(n, d)
    for i in range(depth):
        j = i // 2
        if i % 2 == 0:
            qT, qiT, wiT, vT, k2, kiP = _attn_pre(
                x2d, mod[i], norm1_g[i], attn_w_in[j], attn_q_norm_g[j], attn_k_norm_g[j],
                idx_k_ln_g[j], idx_k_ln_b[j], positions, bn, t_len, tm_pre)
            mix = _dsa_attention(qT, qiT, wiT, k2, kiP, vT, bn, t_len, topk).reshape(n, d)
            w_mix_out = attn_w_out[j]
        else:
            mix = _conv_pre(x2d, mod[i], norm1_g[i], conv_w_in[j], conv_w[j], t_len, tm_post)
            w_mix_out = conv_w_out[j]
        x2d = _post(x2d, mix, mod[i], norm2_g[i], w_mix_out, ffn_w_gate[i], ffn_w_up[i], ffn_w_down[i],
                    t_len, tm_post)
    return x2d.reshape(bn, t_len, d)
```

```python
import functools
import math

import numpy as np
import jax
import jax.numpy as jnp
from jax import lax
from jax.experimental import pallas as pl
from jax.experimental.pallas import tpu as pltpu

F32 = jnp.float32
BF16 = jnp.bfloat16
I32 = jnp.int32

N_HEADS = 16
HEAD_DIM = 64
N_KV_HEADS = 4
GROUP = N_HEADS // N_KV_HEADS
IDX_HEADS = 8
IDX_DIM = 64
TOPK_MAX = 256
ROPE_THETA = 10000.0
CONV_WIDTH = 3
NORM_EPS = 1e-6
LOG2E = math.log2(math.e)

LANES = 128
SUBLANES = 8
VMEM_LIMIT = 56 * 1024 * 1024

Q_COLS = N_HEADS * HEAD_DIM
KV_COLS = N_KV_HEADS * HEAD_DIM
QI_COLS = IDX_HEADS * IDX_DIM
V_AUG = HEAD_DIM + 16

INT_MIN = np.int32(-2**31)
KEY_NEG_INF = np.int32(np.uint32(0x807FFFFF).astype(np.int64) - 2**32)
KEY_HI_NEG_INF = np.int32(np.uint32(0x807F0000).astype(np.int64) - 2**32)


def _const_spec(shape):
    nd = len(shape)
    return pl.BlockSpec(shape, lambda *_: (0,) * nd, pipeline_mode=pl.Buffered(1))


def _params(n_axes):
    return pltpu.CompilerParams(dimension_semantics=("arbitrary",) * n_axes,
                                vmem_limit_bytes=VMEM_LIMIT)


def _mod_kernel(c_ref, w_ref, b_ref, o_ref):
    c = c_ref[...]
    ca = c * jax.nn.sigmoid(c)
    o_ref[0] = jnp.dot(ca.astype(BF16), w_ref[0].astype(BF16), preferred_element_type=F32) + b_ref[0]


def _modulation(c, ada_w, ada_b):
    depth, d, six_d = ada_w.shape
    bn = c.shape[0]
    rows = 16
    cp = jnp.zeros((rows, d), F32).at[:bn].set(c)
    tn = 1536
    out = pl.pallas_call(
        _mod_kernel,
        grid=(depth, six_d // tn),
        in_specs=[pl.BlockSpec((rows, d), lambda i, j: (0, 0)),
                  pl.BlockSpec((1, d, tn), lambda i, j: (i, 0, j)),
                  pl.BlockSpec((1, 1, tn), lambda i, j: (i, 0, j))],
        out_specs=pl.BlockSpec((1, rows, tn), lambda i, j: (i, 0, j)),
        out_shape=jax.ShapeDtypeStruct((depth, rows, six_d), F32),
        compiler_params=_params(2),
        name="adaln_mod",
    )(cp, ada_w, ada_b.reshape(depth, 1, six_d))
    return out[:, :bn].reshape(depth, bn, 6, d)


def _segmean64(v, gmat):
    hi = v.astype(BF16)
    lo = (v - hi.astype(F32)).astype(BF16)
    s = jnp.dot(hi, gmat, preferred_element_type=F32) + jnp.dot(lo, gmat, preferred_element_type=F32)
    return s * (1.0 / HEAD_DIM)


def _attn_pre_kernel(x_ref, mod_ref, g1_ref, wta_ref, wb_ref, posr_ref, posc_ref, invft_ref, invfr_ref,
                     gq_ref, gk_ref, lng_ref, lnb_ref,
                     qT_ref, qiT_ref, wiT_ref, vT_ref, k4_ref, ki_ref):
    tm = x_ref.shape[0]
    x = x_ref[...]
    sh = mod_ref[0, 0:1, :]
    sc = mod_ref[0, 1:2, :]
    ms = jnp.mean(x * x, axis=-1, keepdims=True)
    h = (x * lax.rsqrt(ms + NORM_EPS) * g1_ref[...]) * (1.0 + sc) + sh
    hb = h.astype(BF16)

    pT = lax.dot_general(wta_ref[...], hb, (((1,), (1,)), ((), ())), preferred_element_type=F32)
    angT = invft_ref[...] * posr_ref[0].astype(F32)
    cT = jnp.cos(angT)
    sT = jnp.sin(angT)
    half = HEAD_DIM // 2

    def rope_t(y):
        x1 = y[:half]
        x2 = y[half:]
        return jnp.concatenate([x1 * cT - x2 * sT, x2 * cT + x1 * sT], axis=0)

    gq = gq_ref[...]
    for hh in range(N_HEADS):
        xq = pT[HEAD_DIM * hh:HEAD_DIM * (hh + 1)]
        inv = lax.rsqrt(jnp.mean(xq * xq, axis=0, keepdims=True) + NORM_EPS)
        y = xq * inv * gq
        qT_ref[0, hh] = (rope_t(y) * (HEAD_DIM ** -0.5 * LOG2E)).astype(BF16)
    o0 = Q_COLS
    ones = jnp.ones((V_AUG - HEAD_DIM, tm), BF16)
    for g in range(N_KV_HEADS):
        vT_ref[0, g, 0, 0:HEAD_DIM, :] = pT[o0 + HEAD_DIM * g:o0 + HEAD_DIM * (g + 1)].astype(BF16)
        vT_ref[0, g, 0, HEAD_DIM:V_AUG, :] = ones
    o0 += KV_COLS
    for hh in range(IDX_HEADS):
        qiT_ref[0, hh] = rope_t(pT[o0 + IDX_DIM * hh:o0 + IDX_DIM * (hh + 1)]).astype(BF16)
    o0 += QI_COLS
    wiT_ref[0] = pT[o0:o0 + IDX_HEADS] * (IDX_HEADS ** -0.5 * IDX_DIM ** -0.5)

    pk = jnp.dot(hb, wb_ref[...], preferred_element_type=F32)
    ang = posc_ref[...].astype(F32) * invfr_ref[...]
    cR = jnp.cos(ang)
    sR = jnp.sin(ang)
    lane = lax.broadcasted_iota(I32, (tm, LANES), 1)
    first = (lane % HEAD_DIM) < half
    s_signed = jnp.where(first, -sR, sR)
    ri = lax.broadcasted_iota(I32, (LANES, LANES), 0) // HEAD_DIM
    ci = lax.broadcasted_iota(I32, (LANES, LANES), 1) // HEAD_DIM
    gmat = jnp.where(ri == ci, 1.0, 0.0).astype(BF16)

    def rope_r(y):
        rot = jnp.where(first, pltpu.roll(y, LANES - half, 1), pltpu.roll(y, half, 1))
        return y * cR + rot * s_signed

    for j in range(KV_COLS // LANES):
        xs = pk[:, LANES * j:LANES * (j + 1)]
        inv = lax.rsqrt(_segmean64(xs * xs, gmat) + NORM_EPS)
        kr = rope_r(xs * inv * gk_ref[...])
        k4_ref[0, 2 * j] = kr[:, 0:HEAD_DIM].astype(BF16)
        k4_ref[0, 2 * j + 1] = kr[:, HEAD_DIM:LANES].astype(BF16)
    t = pk[:, KV_COLS:KV_COLS + LANES]
    mu = _segmean64(t, gmat)
    dlt = t - mu
    var = _segmean64(dlt * dlt, gmat)
    kin = dlt * lax.rsqrt(var + NORM_EPS) * lng_ref[...] + lnb_ref[...]
    ki_ref[...] = rope_r(kin)[:, 0:IDX_DIM].astype(BF16)


def _attn_pre(x2d, mod, g1, w_in, gq, gk, lng, lnb, positions, bn, t_len, tm):
    n, d = x2d.shape
    tpb = t_len // tm
    half = HEAD_DIM // 2
    wq = w_in[:, :Q_COLS]
    wk = w_in[:, Q_COLS:Q_COLS + KV_COLS]
    wv = w_in[:, Q_COLS + KV_COLS:Q_COLS + 2 * KV_COLS]
    o0 = Q_COLS + 2 * KV_COLS
    wqi = w_in[:, o0:o0 + QI_COLS]
    wki = w_in[:, o0 + QI_COLS:o0 + QI_COLS + IDX_DIM]
    wwi = w_in[:, o0 + QI_COLS + IDX_DIM:]
    ma = Q_COLS + KV_COLS + QI_COLS + IDX_HEADS
    ma_pad = -(-ma // 16) * 16
    wta = jnp.concatenate([wq, wv, wqi, wwi, jnp.zeros((d, ma_pad - ma), F32)], axis=1).T.astype(BF16)
    wb = jnp.concatenate([wk, wki, jnp.zeros((d, LANES - IDX_DIM), F32)], axis=1).astype(BF16)
    nb = wb.shape[1]
    inv_freq = ROPE_THETA ** (-(jnp.arange(half, dtype=F32) * 2.0 / HEAD_DIM))
    invft = jnp.broadcast_to(inv_freq[:, None], (half, tm))
    invfr = jnp.tile(inv_freq, LANES // half).reshape(1, LANES)
    gq_b = jnp.broadcast_to(gq.astype(F32)[:, None], (HEAD_DIM, tm))
    gk_r = jnp.tile(gk.astype(F32), LANES // HEAD_DIM).reshape(1, LANES)
    zpad = jnp.zeros((LANES - IDX_DIM,), F32)
    lng_r = jnp.concatenate([lng.astype(F32), zpad]).reshape(1, LANES)
    lnb_r = jnp.concatenate([lnb.astype(F32), zpad]).reshape(1, LANES)
    posr = positions.reshape(bn, 1, t_len)
    posc = positions.reshape(n, 1)

    row = lambda i: (i, 0)
    outs = pl.pallas_call(
        _attn_pre_kernel,
        grid=(n // tm,),
        in_specs=[pl.BlockSpec((tm, d), row),
                  pl.BlockSpec((1, 6, d), lambda i: (i // tpb, 0, 0)),
                  _const_spec((1, d)),
                  _const_spec((ma_pad, d)),
                  _const_spec((d, nb)),
                  pl.BlockSpec((1, 1, tm), lambda i: (i // tpb, 0, i % tpb)),
                  pl.BlockSpec((tm, 1), row),
                  _const_spec((half, tm)),
                  _const_spec((1, LANES)),
                  _const_spec((HEAD_DIM, tm)),
                  _const_spec((1, LANES)),
                  _const_spec((1, LANES)),
                  _const_spec((1, LANES))],
        out_specs=[pl.BlockSpec((1, N_HEADS, HEAD_DIM, tm), lambda i: (i // tpb, 0, 0, i % tpb)),
                   pl.BlockSpec((1, IDX_HEADS, IDX_DIM, tm), lambda i: (i // tpb, 0, 0, i % tpb)),
                   pl.BlockSpec((1, IDX_HEADS, tm), lambda i: (i // tpb, 0, i % tpb)),
                   pl.BlockSpec((1, N_KV_HEADS, 1, V_AUG, tm), lambda i: (i // tpb, 0, i % tpb, 0, 0)),
                   pl.BlockSpec((1, N_KV_HEADS, tm, HEAD_DIM), lambda i: (i // tpb, 0, i % tpb, 0)),
                   pl.BlockSpec((tm, IDX_DIM), row)],
        out_shape=[jax.ShapeDtypeStruct((bn, N_HEADS, HEAD_DIM, t_len), BF16),
                   jax.ShapeDtypeStruct((bn, IDX_HEADS, IDX_DIM, t_len), BF16),
                   jax.ShapeDtypeStruct((bn, IDX_HEADS, t_len), F32),
                   jax.ShapeDtypeStruct((bn, N_KV_HEADS, tpb, V_AUG, tm), BF16),
                   jax.ShapeDtypeStruct((bn, N_KV_HEADS, t_len, HEAD_DIM), BF16),
                   jax.ShapeDtypeStruct((n, IDX_DIM), BF16)],
        compiler_params=_params(1),
        name="attn_pre",
    )(x2d, mod, g1.reshape(1, d), wta, wb, posr, posc, invft, invfr, gq_b, gk_r, lng_r, lnb_r)
    return outs


def _dsa_kernel(qT_ref, qiT_ref, wiT_ref, k4_ref, ki_ref, vT_ref, o_ref,
                st_ref, hi_ref, m_ref, al_ref, acc_ref, bias_ref, lg_ref, left_ref, ltri_ref, *, topk, rb, ka):
    tq = o_ref.shape[1]
    kc = tq
    i = pl.program_id(1)
    n_c = i + 1
    q0 = i * tq

    tcol = q0 + lax.broadcasted_iota(I32, (kc, tq), 1)
    srow0 = lax.broadcasted_iota(I32, (kc, tq), 0)
    wi = wiT_ref[0]
    qi_all = jnp.concatenate([qiT_ref[0, hh] for hh in range(IDX_HEADS)], axis=1)

    def score_chunk(c):
        r0 = pl.multiple_of(c * kc, kc)
        ki = ki_ref[0, pl.ds(r0, kc), :]
        s_all = jnp.dot(ki, qi_all, preferred_element_type=F32)
        acc = jnp.zeros((kc, tq), F32)
        for hh in range(IDX_HEADS):
            acc = acc + jnp.maximum(s_all[:, tq * hh:tq * (hh + 1)], 0.0) * wi[hh:hh + 1, :]
        causal = (srow0 + r0) <= tcol
        sc = jnp.where(causal, acc, -jnp.inf)
        st_ref[pl.ds(r0, kc), :] = sc
        top = lax.bitcast_convert_type(sc, I32) & np.int32(-65536)
        hi_ref[pl.ds(r0, kc), :] = lax.bitcast_convert_type(top, F32).astype(BF16)

    def score_body(j, carry):
        score_chunk(2 * j)
        score_chunk(2 * j + 1)
        return carry

    lax.fori_loop(0, (n_c + 1) // 2, score_body, 0)

    n_rb = (n_c * kc + rb - 1) // rb
    n_par = 4
    pack = 2 * SUBLANES

    def all_sublanes(tot):
        for sh in (4, 2, 1):
            tot = tot + pltpu.roll(tot, sh, 0)
        return tot

    def count(pred):
        def body(r, acc):
            r0 = pl.multiple_of(r * rb, rb)
            blk = st_ref[pl.ds(r0, rb), :].reshape(rb // SUBLANES, SUBLANES, tq)
            hit = pred(blk, r0).astype(I32).reshape(rb // (SUBLANES * n_par), n_par, SUBLANES, tq)
            return acc + jnp.sum(hit, axis=0)
        acc = lax.fori_loop(0, n_rb, body, jnp.zeros((n_par, SUBLANES, tq), I32))
        return all_sublanes(jnp.sum(acc, axis=0))

    def count_hi(cand16):
        def body(r, acc):
            r0 = pl.multiple_of(r * rb, rb)
            blk = hi_ref[pl.ds(r0, rb), :].reshape(rb // pack, pack, tq)
            hit = jnp.where(blk >= cand16[None], jnp.ones((), BF16), jnp.zeros((), BF16))
            parts = [hit[g] for g in range(rb // pack)]
            while len(parts) > 1:
                parts = [parts[g] + parts[g + 1] for g in range(0, len(parts), 2)]
            return acc + parts[0].astype(F32)
        acc = lax.fori_loop(0, n_rb, body, jnp.zeros((pack, tq), F32))
        return all_sublanes((acc[0:SUBLANES] + acc[SUBLANES:pack]).astype(I32))

    def key_to_float(u):
        ks = u ^ INT_MIN
        bits = jnp.where(ks < 0, ks ^ np.int32(0x7FFFFFFF), ks)
        return ks, bits

    def hi_body(it, carry):
        tau_u, cnt_tau = carry
        cand_u = tau_u | lax.shift_left(np.int32(1), np.int32(31) - it)
        ks, bits = key_to_float(cand_u)
        cand_top = lax.bitcast_convert_type(bits & np.int32(-65536), F32)
        cand16 = jnp.concatenate([cand_top, cand_top], axis=0).astype(BF16)
        cnt = count_hi(cand16)
        take = (cnt >= topk) | (ks < KEY_HI_NEG_INF)
        return jnp.where(take, cand_u, tau_u), jnp.where(take, cnt, cnt_tau)

    def lo_body(it, carry):
        tau_u, cnt_tau = carry
        cand_u = tau_u | lax.shift_left(np.int32(1), np.int32(15) - it)
        ks, bits = key_to_float(cand_u)
        cand_f = lax.bitcast_convert_type(bits, F32)
        cnt = count(lambda blk, r0: blk >= cand_f[None])
        take = (cnt >= topk) | (ks < KEY_NEG_INF)
        return jnp.where(take, cand_u, tau_u), jnp.where(take, cnt, cnt_tau)

    def lo_stage(state, first, n):
        tau_u, cnt_tau, _ = state
        tau_u, cnt_tau = lax.fori_loop(first, first + n, lo_body, (tau_u, cnt_tau))
        tau8 = lax.bitcast_convert_type(key_to_float(tau_u)[1], F32)
        return tau_u, cnt_tau, count(lambda blk, r0: blk > tau8[None])

    def pending(state):
        tau_u, cnt_tau, cnt_gt = state
        open_lane = ((cnt_tau != topk) & (cnt_gt >= topk)) | ((tau_u ^ INT_MIN) < KEY_NEG_INF)
        return jnp.max(open_lane.astype(I32)) > 0

    zero8 = jnp.zeros((SUBLANES, tq), I32)
    tau_u, cnt_tau = lax.fori_loop(0, 16, hi_body, (zero8, zero8))
    state = lo_stage((tau_u, cnt_tau, zero8), 0, 10)
    state = lax.cond(pending(state), lambda s: lo_stage(s, 10, 2), lambda s: s, state)
    state = lax.cond(pending(state), lambda s: lo_stage(s, 12, 4), lambda s: s, state)
    tau_u, _, cnt_gt = state
    tau8 = lax.bitcast_convert_type(key_to_float(tau_u)[1], F32)
    left_ref[...] = (topk - cnt_gt).astype(F32)
    tau1 = tau8[0:1]
    ri = lax.broadcasted_iota(I32, (kc, kc), 0)
    ci = lax.broadcasted_iota(I32, (kc, kc), 1)
    ltri_ref[...] = jnp.where(ci <= ri, 1.0, 0.0).astype(BF16)

    m_ref[...] = jnp.full(m_ref.shape, -1e30, F32)
    acc_ref[...] = jnp.zeros(acc_ref.shape, F32)

    sub = ka // kc
    n_a = (n_c + sub - 1) // sub

    def set_bias(c):
        for u in range(sub):
            r0 = pl.multiple_of(c * ka + u * kc, kc)
            sc = st_ref[pl.ds(r0, kc), :]
            srow = srow0 + r0
            tie = sc == tau1
            seen = jnp.dot(ltri_ref[...], jnp.where(tie, 1.0, 0.0).astype(BF16), preferred_element_type=F32)
            left = left_ref[0:1, :]
            mask = ((sc > tau1) | (tie & (seen <= left))) & (srow <= tcol)
            bias_ref[u * kc:(u + 1) * kc, :] = jnp.where(mask, 0.0, -jnp.inf)
            left_ref[...] = jnp.broadcast_to(left - seen[kc - 1:kc, :], left_ref.shape)

    def logits_stage(c, hh):
        r0 = pl.multiple_of(c * ka, ka)
        kg = k4_ref[0, hh // GROUP, pl.ds(r0, ka), :]
        lg = jnp.dot(kg, qT_ref[0, hh], preferred_element_type=F32) + bias_ref[...]
        lg_ref[hh] = lg
        mx = jnp.max(lg.reshape(ka // SUBLANES, SUBLANES, tq), axis=0)
        m_old = m_ref[hh:hh + 1, :]
        m_new = jnp.maximum(m_old, jnp.max(mx, axis=0, keepdims=True))
        m_ref[hh:hh + 1, :] = m_new
        al_ref[hh:hh + 1, :] = jnp.exp2(m_old - m_new)

    def value_stage(c, hh):
        vt = jnp.concatenate([vT_ref[0, hh // GROUP, c * sub + u] for u in range(sub)], axis=1)
        p = jnp.exp2(lg_ref[hh] - m_ref[hh:hh + 1, :]).astype(BF16)
        acc_ref[hh] = acc_ref[hh] * al_ref[hh:hh + 1, :] + jnp.dot(vt, p, preferred_element_type=F32)

    set_bias(0)
    for hh in range(N_HEADS):
        logits_stage(0, hh)

    def attn_body(c, carry):
        set_bias(c)
        for hh in range(N_HEADS):
            value_stage(c - 1, hh)
            logits_stage(c, hh)
        return carry

    lax.fori_loop(1, n_a, attn_body, 0)
    for hh in range(N_HEADS):
        value_stage(n_a - 1, hh)

    for hp in range(N_HEADS // 2):
        parts = []
        for hh in (2 * hp, 2 * hp + 1):
            a = acc_ref[hh]
            parts.append(a[0:HEAD_DIM] / a[HEAD_DIM:HEAD_DIM + 1])
        pair = jnp.concatenate(parts, axis=0)
        o_ref[0, :, LANES * hp:LANES * (hp + 1)] = pair.T.astype(BF16)


def _dsa_attention(qT, qiT, wiT, k4, ki, vT, bn, t_len, topk):
    tq = topk
    rb = 512
    ka = 256
    kern = functools.partial(_dsa_kernel, topk=topk, rb=rb, ka=ka)
    return pl.pallas_call(
        kern,
        grid=(bn, t_len // tq),
        in_specs=[pl.BlockSpec((1, N_HEADS, HEAD_DIM, tq), lambda b, i: (b, 0, 0, i)),
                  pl.BlockSpec((1, IDX_HEADS, IDX_DIM, tq), lambda b, i: (b, 0, 0, i)),
                  pl.BlockSpec((1, IDX_HEADS, tq), lambda b, i: (b, 0, i)),
                  pl.BlockSpec((1, N_KV_HEADS, t_len, HEAD_DIM), lambda b, i: (b, 0, 0, 0),
                               pipeline_mode=pl.Buffered(1)),
                  pl.BlockSpec((1, t_len, IDX_DIM), lambda b, i: (b, 0, 0), pipeline_mode=pl.Buffered(1)),
                  pl.BlockSpec((1, N_KV_HEADS, t_len // tq, V_AUG, tq), lambda b, i: (b, 0, 0, 0, 0),
                               pipeline_mode=pl.Buffered(1))],
        out_specs=pl.BlockSpec((1, tq, Q_COLS), lambda b, i: (b, i, 0)),
        out_shape=jax.ShapeDtypeStruct((bn, t_len, Q_COLS), BF16),
        scratch_shapes=[pltpu.VMEM((t_len, tq), F32),
                        pltpu.VMEM((t_len, tq), BF16),
                        pltpu.VMEM((N_HEADS, tq), F32),
                        pltpu.VMEM((N_HEADS, tq), F32),
                        pltpu.VMEM((N_HEADS, V_AUG, tq), F32),
                        pltpu.VMEM((ka, tq), F32),
                        pltpu.VMEM((N_HEADS, ka, tq), F32),
                        pltpu.VMEM((SUBLANES, tq), F32),
                        pltpu.VMEM((tq, tq), BF16)],
        compiler_params=_params(2),
        name="dsa_attention",
    )(qT, qiT, wiT, k4, ki.reshape(bn, t_len, IDX_DIM), vT)


def _conv_pre_kernel(x_ref, mod_ref, g1_ref, w_ref, cw_ref, o_ref, zbuf_ref, *, tpb):
    tm, d = x_ref.shape
    i = pl.program_id(0)

    @pl.when(i % tpb == 0)
    def _():
        zbuf_ref[0:SUBLANES, :] = jnp.zeros((SUBLANES, d), F32)

    x = x_ref[...]
    sh = mod_ref[0, 0:1, :]
    sc = mod_ref[0, 1:2, :]
    ms = jnp.mean(x * x, axis=-1, keepdims=True)
    h = (x * lax.rsqrt(ms + NORM_EPS) * g1_ref[...]) * (1.0 + sc) + sh
    proj = jnp.dot(h.astype(BF16), w_ref[...], preferred_element_type=F32)
    b_gate = proj[:, 0:d]
    z = proj[:, d:2 * d] * proj[:, 2 * d:3 * d]
    zbuf_ref[SUBLANES:SUBLANES + tm, :] = z
    z1 = zbuf_ref[SUBLANES - 1:SUBLANES - 1 + tm, :]
    z2 = zbuf_ref[SUBLANES - 2:SUBLANES - 2 + tm, :]
    zc = cw_ref[0:1, :] * z2 + cw_ref[1:2, :] * z1 + cw_ref[2:3, :] * z
    o_ref[...] = (b_gate * zc).astype(BF16)
    zbuf_ref[0:SUBLANES, :] = z[tm - SUBLANES:tm, :]


def _conv_pre(x2d, mod, g1, w_in, conv_w, t_len, tm):
    n, d = x2d.shape
    tpb = t_len // tm
    row = lambda i: (i, 0)
    return pl.pallas_call(
        functools.partial(_conv_pre_kernel, tpb=tpb),
        grid=(n // tm,),
        in_specs=[pl.BlockSpec((tm, d), row),
                  pl.BlockSpec((1, 6, d), lambda i: (i // tpb, 0, 0)),
                  _const_spec((1, d)),
                  _const_spec((d, 3 * d)),
                  _const_spec((CONV_WIDTH, d))],
        out_specs=pl.BlockSpec((tm, d), row),
        out_shape=jax.ShapeDtypeStruct((n, d), BF16),
        scratch_shapes=[pltpu.VMEM((tm + SUBLANES, d), F32)],
        compiler_params=_params(1),
        name="conv_pre",
    )(x2d, mod, g1.reshape(1, d), w_in.astype(BF16), conv_w.astype(F32))


def _post_kernel(x_ref, mix_ref, mod_ref, g2_ref, wo_ref, wg_ref, wu_ref, wd_ref, o_ref, *, th):
    gate1 = mod_ref[0, 2:3, :]
    sh2 = mod_ref[0, 3:4, :]
    sc2 = mod_ref[0, 4:5, :]
    gate2 = mod_ref[0, 5:6, :]
    y = jnp.dot(mix_ref[...], wo_ref[...], preferred_element_type=F32)
    x1 = x_ref[...] + gate1 * y
    ms = jnp.mean(x1 * x1, axis=-1, keepdims=True)
    h = ((x1 * lax.rsqrt(ms + NORM_EPS) * g2_ref[...]) * (1.0 + sc2) + sh2).astype(BF16)
    hidden = wg_ref.shape[1]
    acc = jnp.zeros(x1.shape, F32)
    for j in range(hidden // th):
        gt = jnp.dot(h, wg_ref[:, th * j:th * (j + 1)], preferred_element_type=F32)
        up = jnp.dot(h, wu_ref[:, th * j:th * (j + 1)], preferred_element_type=F32)
        a = (gt * jax.nn.sigmoid(gt)) * up
        acc = acc + jnp.dot(a.astype(BF16), wd_ref[th * j:th * (j + 1), :], preferred_element_type=F32)
    o_ref[...] = x1 + gate2 * acc


def _post(x2d, mix, mod, g2, w_out, w_gate, w_up, w_down, t_len, tm):
    n, d = x2d.shape
    hidden = w_gate.shape[1]
    tpb = t_len // tm
    row = lambda i: (i, 0)
    return pl.pallas_call(
        functools.partial(_post_kernel, th=256),
        grid=(n // tm,),
        in_specs=[pl.BlockSpec((tm, d), row),
                  pl.BlockSpec((tm, d), row),
                  pl.BlockSpec((1, 6, d), lambda i: (i // tpb, 0, 0)),
                  _const_spec((1, d)),
                  _const_spec((d, d)),
                  _const_spec((d, hidden)),
                  _const_spec((d, hidden)),
                  _const_spec((hidden, d))],
        out_specs=pl.BlockSpec((tm, d), row),
        out_shape=jax.ShapeDtypeStruct((n, d), F32),
        compiler_params=_params(1),
        name="mixer_out_ffn",
    )(x2d, mix, mod, g2.reshape(1, d), w_out.astype(BF16), w_gate.astype(BF16), w_up.astype(BF16),
      w_down.astype(BF16))


def kernel(x, c, positions, ada_w, ada_b, norm1_g, norm2_g, attn_w_in, attn_q_norm_g, attn_k_norm_g,
           idx_k_ln_g, idx_k_ln_b, attn_w_out, conv_w_in, conv_w, conv_w_out, ffn_w_gate, ffn_w_up,
           ffn_w_down):
    bn, t_len, d = x.shape
    depth = ada_w.shape[0]
    topk = min(TOPK_MAX, t_len // 4)
    assert topk == TOPK_MAX and t_len % topk == 0 and (t_len & (t_len - 1)) == 0
    n = bn * t_len
    tm_pre = 256
    tm_post = 512
    mod = _modulation(c, ada_w, ada_b)
    x2d = x.reshape(n, d)
    for i in range(depth):
        j = i // 2
        if i % 2 == 0:
            qT, qiT, wiT, vT, k2, kiP = _attn_pre(
                x2d, mod[i], norm1_g[i], attn_w_in[j], attn_q_norm_g[j], attn_k_norm_g[j],
                idx_k_ln_g[j], idx_k_ln_b[j], positions, bn, t_len, tm_pre)
            mix = _dsa_attention(qT, qiT, wiT, k2, kiP, vT, bn, t_len, topk).reshape(n, d)
            w_mix_out = attn_w_out[j]
        else:
            mix = _conv_pre(x2d, mod[i], norm1_g[i], conv_w_in[j], conv_w[j], t_len, tm_post)
            w_mix_out = conv_w_out[j]
        x2d = _post(x2d, mix, mod[i], norm2_g[i], w_mix_out, ffn_w_gate[i], ffn_w_up[i], ffn_w_down[i],
                    t_len, tm_post)
    return x2d.reshape(bn, t_len, d)
```

```python
import functools
import math

import numpy as np
import jax
import jax.numpy as jnp
from jax import lax
from jax.experimental import pallas as pl
from jax.experimental.pallas import tpu as pltpu

F32 = jnp.float32
BF16 = jnp.bfloat16
I32 = jnp.int32

N_HEADS = 16
HEAD_DIM = 64
N_KV_HEADS = 4
GROUP = N_HEADS // N_KV_HEADS
IDX_HEADS = 8
IDX_DIM = 64
TOPK_MAX = 256
ROPE_THETA = 10000.0
CONV_WIDTH = 3
NORM_EPS = 1e-6
LOG2E = math.log2(math.e)

LANES = 128
SUBLANES = 8
VMEM_LIMIT = 56 * 1024 * 1024

Q_COLS = N_HEADS * HEAD_DIM
KV_COLS = N_KV_HEADS * HEAD_DIM
QI_COLS = IDX_HEADS * IDX_DIM
V_AUG = HEAD_DIM + 16

INT_MIN = np.int32(-2**31)
KEY_NEG_INF = np.int32(np.uint32(0x807FFFFF).astype(np.int64) - 2**32)
KEY_HI_NEG_INF = np.int32(np.uint32(0x807F0000).astype(np.int64) - 2**32)


def _const_spec(shape):
    nd = len(shape)
    return pl.BlockSpec(shape, lambda *_: (0,) * nd, pipeline_mode=pl.Buffered(1))


def _params(n_axes):
    return pltpu.CompilerParams(dimension_semantics=("arbitrary",) * n_axes,
                                vmem_limit_bytes=VMEM_LIMIT)


def _mod_kernel(c_ref, w_ref, b_ref, o_ref):
    c = c_ref[...]
    ca = c * jax.nn.sigmoid(c)
    o_ref[0] = jnp.dot(ca.astype(BF16), w_ref[0].astype(BF16), preferred_element_type=F32) + b_ref[0]


def _modulation(c, ada_w, ada_b):
    depth, d, six_d = ada_w.shape
    bn = c.shape[0]
    rows = 16
    cp = jnp.zeros((rows, d), F32).at[:bn].set(c)
    tn = 1536
    out = pl.pallas_call(
        _mod_kernel,
        grid=(depth, six_d // tn),
        in_specs=[pl.BlockSpec((rows, d), lambda i, j: (0, 0)),
                  pl.BlockSpec((1, d, tn), lambda i, j: (i, 0, j)),
                  pl.BlockSpec((1, 1, tn), lambda i, j: (i, 0, j))],
        out_specs=pl.BlockSpec((1, rows, tn), lambda i, j: (i, 0, j)),
        out_shape=jax.ShapeDtypeStruct((depth, rows, six_d), F32),
        compiler_params=_params(2),
        name="adaln_mod",
    )(cp, ada_w, ada_b.reshape(depth, 1, six_d))
    return out[:, :bn].reshape(depth, bn, 6, d)


def _attn_pre_kernel(x_ref, mod_ref, g1_ref, wt_ref, posr_ref, invft_ref, gq_ref, gk_ref, lng_ref, lnb_ref,
                     qT_ref, qiT_ref, wiT_ref, vT_ref, k4_ref, ki_ref):
    tm = x_ref.shape[0]
    x = x_ref[...]
    sh = mod_ref[0, 0:1, :]
    sc = mod_ref[0, 1:2, :]
    ms = jnp.mean(x * x, axis=-1, keepdims=True)
    h = (x * lax.rsqrt(ms + NORM_EPS) * g1_ref[...]) * (1.0 + sc) + sh
    hb = h.astype(BF16)

    pT = lax.dot_general(wt_ref[...], hb, (((1,), (1,)), ((), ())), preferred_element_type=F32)
    angT = invft_ref[...] * posr_ref[0].astype(F32)
    cT = jnp.cos(angT)
    sT = jnp.sin(angT)
    half = HEAD_DIM // 2

    def rope_t(y):
        x1 = y[:half]
        x2 = y[half:]
        return jnp.concatenate([x1 * cT - x2 * sT, x2 * cT + x1 * sT], axis=0)

    gq = gq_ref[...]
    for hh in range(N_HEADS):
        xq = pT[HEAD_DIM * hh:HEAD_DIM * (hh + 1)]
        inv = lax.rsqrt(jnp.mean(xq * xq, axis=0, keepdims=True) + NORM_EPS)
        y = xq * inv * gq
        qT_ref[0, hh] = (rope_t(y) * (HEAD_DIM ** -0.5 * LOG2E)).astype(BF16)
    o0 = Q_COLS
    gk = gk_ref[...]
    for j in range(N_KV_HEADS // 2):
        pair = []
        for g in (2 * j, 2 * j + 1):
            xk = pT[o0 + HEAD_DIM * g:o0 + HEAD_DIM * (g + 1)]
            inv = lax.rsqrt(jnp.mean(xk * xk, axis=0, keepdims=True) + NORM_EPS)
            pair.append(rope_t(xk * inv * gk))
        kr = jnp.concatenate(pair, axis=0).T
        k4_ref[0, 2 * j] = kr[:, 0:HEAD_DIM].astype(BF16)
        k4_ref[0, 2 * j + 1] = kr[:, HEAD_DIM:LANES].astype(BF16)
    o0 += KV_COLS
    ones = jnp.ones((V_AUG - HEAD_DIM, tm), BF16)
    for g in range(N_KV_HEADS):
        vT_ref[0, g, 0, 0:HEAD_DIM, :] = pT[o0 + HEAD_DIM * g:o0 + HEAD_DIM * (g + 1)].astype(BF16)
        vT_ref[0, g, 0, HEAD_DIM:V_AUG, :] = ones
    o0 += KV_COLS
    for hh in range(IDX_HEADS):
        qiT_ref[0, hh] = rope_t(pT[o0 + IDX_DIM * hh:o0 + IDX_DIM * (hh + 1)]).astype(BF16)
    o0 += QI_COLS
    t = pT[o0:o0 + IDX_DIM]
    mu = jnp.mean(t, axis=0, keepdims=True)
    dlt = t - mu
    var = jnp.mean(dlt * dlt, axis=0, keepdims=True)
    kin = rope_t(dlt * lax.rsqrt(var + NORM_EPS) * lng_ref[...] + lnb_ref[...])
    kin = jnp.concatenate([kin, jnp.zeros((LANES - IDX_DIM, tm), F32)], axis=0).T
    ki_ref[...] = kin[:, 0:IDX_DIM].astype(BF16)
    o0 += IDX_DIM
    wiT_ref[0] = pT[o0:o0 + IDX_HEADS] * (IDX_HEADS ** -0.5 * IDX_DIM ** -0.5)


def _attn_pre(x2d, mod, g1, w_in, gq, gk, lng, lnb, positions, bn, t_len, tm):
    n, d = x2d.shape
    tpb = t_len // tm
    half = HEAD_DIM // 2
    ma = w_in.shape[1]
    ma_pad = -(-ma // 16) * 16
    wt = jnp.concatenate([w_in, jnp.zeros((d, ma_pad - ma), F32)], axis=1).T.astype(BF16)
    inv_freq = ROPE_THETA ** (-(jnp.arange(half, dtype=F32) * 2.0 / HEAD_DIM))
    invft = jnp.broadcast_to(inv_freq[:, None], (half, tm))
    col = lambda v: jnp.broadcast_to(v.astype(F32)[:, None], (v.shape[0], tm))
    posr = positions.reshape(bn, 1, t_len)

    row = lambda i: (i, 0)
    outs = pl.pallas_call(
        _attn_pre_kernel,
        grid=(n // tm,),
        in_specs=[pl.BlockSpec((tm, d), row),
                  pl.BlockSpec((1, 6, d), lambda i: (i // tpb, 0, 0)),
                  _const_spec((1, d)),
                  _const_spec((ma_pad, d)),
                  pl.BlockSpec((1, 1, tm), lambda i: (i // tpb, 0, i % tpb)),
                  _const_spec((half, tm)),
                  _const_spec((HEAD_DIM, tm)),
                  _const_spec((HEAD_DIM, tm)),
                  _const_spec((IDX_DIM, tm)),
                  _const_spec((IDX_DIM, tm))],
        out_specs=[pl.BlockSpec((1, N_HEADS, HEAD_DIM, tm), lambda i: (i // tpb, 0, 0, i % tpb)),
                   pl.BlockSpec((1, IDX_HEADS, IDX_DIM, tm), lambda i: (i // tpb, 0, 0, i % tpb)),
                   pl.BlockSpec((1, IDX_HEADS, tm), lambda i: (i // tpb, 0, i % tpb)),
                   pl.BlockSpec((1, N_KV_HEADS, 1, V_AUG, tm), lambda i: (i // tpb, 0, i % tpb, 0, 0)),
                   pl.BlockSpec((1, N_KV_HEADS, tm, HEAD_DIM), lambda i: (i // tpb, 0, i % tpb, 0)),
                   pl.BlockSpec((tm, IDX_DIM), row)],
        out_shape=[jax.ShapeDtypeStruct((bn, N_HEADS, HEAD_DIM, t_len), BF16),
                   jax.ShapeDtypeStruct((bn, IDX_HEADS, IDX_DIM, t_len), BF16),
                   jax.ShapeDtypeStruct((bn, IDX_HEADS, t_len), F32),
                   jax.ShapeDtypeStruct((bn, N_KV_HEADS, tpb, V_AUG, tm), BF16),
                   jax.ShapeDtypeStruct((bn, N_KV_HEADS, t_len, HEAD_DIM), BF16),
                   jax.ShapeDtypeStruct((n, IDX_DIM), BF16)],
        compiler_params=_params(1),
        name="attn_pre",
    )(x2d, mod, g1.reshape(1, d), wt, posr, invft, col(gq), col(gk), col(lng), col(lnb))
    return outs


def _dsa_kernel(qT_ref, qiT_ref, wiT_ref, k4_ref, ki_ref, vT_ref, o_ref,
                st_ref, hi_ref, m_ref, al_ref, acc_ref, bias_ref, lg_ref, left_ref, ltri_ref, *, topk, rb, ka):
    tq = o_ref.shape[1]
    kc = tq
    i = pl.program_id(1)
    n_c = i + 1
    q0 = i * tq

    tcol = q0 + lax.broadcasted_iota(I32, (kc, tq), 1)
    srow0 = lax.broadcasted_iota(I32, (kc, tq), 0)
    wi = wiT_ref[0]
    qi_all = jnp.concatenate([qiT_ref[0, hh] for hh in range(IDX_HEADS)], axis=1)

    def score_chunk(c):
        r0 = pl.multiple_of(c * kc, kc)
        ki = ki_ref[0, pl.ds(r0, kc), :]
        s_all = jnp.dot(ki, qi_all, preferred_element_type=F32)
        acc = jnp.zeros((kc, tq), F32)
        for hh in range(IDX_HEADS):
            acc = acc + jnp.maximum(s_all[:, tq * hh:tq * (hh + 1)], 0.0) * wi[hh:hh + 1, :]
        causal = (srow0 + r0) <= tcol
        sc = jnp.where(causal, acc, -jnp.inf)
        st_ref[pl.ds(r0, kc), :] = sc
        top = lax.bitcast_convert_type(sc, I32) & np.int32(-65536)
        hi_ref[pl.ds(r0, kc), :] = lax.bitcast_convert_type(top, F32).astype(BF16)

    def score_body(j, carry):
        score_chunk(2 * j)
        score_chunk(2 * j + 1)
        return carry

    lax.fori_loop(0, (n_c + 1) // 2, score_body, 0)

    n_rb = (n_c * kc + rb - 1) // rb
    n_par = 4
    pack = 2 * SUBLANES

    def all_sublanes(tot):
        for sh in (4, 2, 1):
            tot = tot + pltpu.roll(tot, sh, 0)
        return tot

    def count(pred):
        def body(r, acc):
            r0 = pl.multiple_of(r * rb, rb)
            blk = st_ref[pl.ds(r0, rb), :].reshape(rb // SUBLANES, SUBLANES, tq)
            hit = pred(blk, r0).astype(I32).reshape(rb // (SUBLANES * n_par), n_par, SUBLANES, tq)
            return acc + jnp.sum(hit, axis=0)
        acc = lax.fori_loop(0, n_rb, body, jnp.zeros((n_par, SUBLANES, tq), I32))
        return all_sublanes(jnp.sum(acc, axis=0))

    def count_hi(cand16):
        def body(r, acc):
            r0 = pl.multiple_of(r * rb, rb)
            blk = hi_ref[pl.ds(r0, rb), :].reshape(rb // pack, pack, tq)
            hit = jnp.where(blk >= cand16[None], jnp.ones((), BF16), jnp.zeros((), BF16))
            parts = [hit[g] for g in range(rb // pack)]
            while len(parts) > 1:
                parts = [parts[g] + parts[g + 1] for g in range(0, len(parts), 2)]
            return acc + parts[0].astype(F32)
        acc = lax.fori_loop(0, n_rb, body, jnp.zeros((pack, tq), F32))
        return all_sublanes((acc[0:SUBLANES] + acc[SUBLANES:pack]).astype(I32))

    def key_to_float(u):
        ks = u ^ INT_MIN
        bits = jnp.where(ks < 0, ks ^ np.int32(0x7FFFFFFF), ks)
        return ks, bits

    def hi_body(it, carry):
        tau_u, cnt_tau = carry
        cand_u = tau_u | lax.shift_left(np.int32(1), np.int32(31) - it)
        ks, bits = key_to_float(cand_u)
        cand_top = lax.bitcast_convert_type(bits & np.int32(-65536), F32)
        cand16 = jnp.concatenate([cand_top, cand_top], axis=0).astype(BF16)
        cnt = count_hi(cand16)
        take = (cnt >= topk) | (ks < KEY_HI_NEG_INF)
        return jnp.where(take, cand_u, tau_u), jnp.where(take, cnt, cnt_tau)

    def lo_body(it, carry):
        tau_u, cnt_tau = carry
        cand_u = tau_u | lax.shift_left(np.int32(1), np.int32(15) - it)
        ks, bits = key_to_float(cand_u)
        cand_f = lax.bitcast_convert_type(bits, F32)
        cnt = count(lambda blk, r0: blk >= cand_f[None])
        take = (cnt >= topk) | (ks < KEY_NEG_INF)
        return jnp.where(take, cand_u, tau_u), jnp.where(take, cnt, cnt_tau)

    def lo_stage(state, first, n):
        tau_u, cnt_tau, _ = state
        tau_u, cnt_tau = lax.fori_loop(first, first + n, lo_body, (tau_u, cnt_tau))
        tau8 = lax.bitcast_convert_type(key_to_float(tau_u)[1], F32)
        return tau_u, cnt_tau, count(lambda blk, r0: blk > tau8[None])

    def pending(state):
        tau_u, cnt_tau, cnt_gt = state
        open_lane = ((cnt_tau != topk) & (cnt_gt >= topk)) | ((tau_u ^ INT_MIN) < KEY_NEG_INF)
        return jnp.max(open_lane.astype(I32)) > 0

    zero8 = jnp.zeros((SUBLANES, tq), I32)
    tau_u, cnt_tau = lax.fori_loop(0, 16, hi_body, (zero8, zero8))
    state = lo_stage((tau_u, cnt_tau, zero8), 0, 10)
    state = lax.cond(pending(state), lambda s: lo_stage(s, 10, 2), lambda s: s, state)
    state = lax.cond(pending(state), lambda s: lo_stage(s, 12, 4), lambda s: s, state)
    tau_u, _, cnt_gt = state
    tau8 = lax.bitcast_convert_type(key_to_float(tau_u)[1], F32)
    left_ref[...] = (topk - cnt_gt).astype(F32)
    tau1 = tau8[0:1]
    ri = lax.broadcasted_iota(I32, (kc, kc), 0)
    ci = lax.broadcasted_iota(I32, (kc, kc), 1)
    ltri_ref[...] = jnp.where(ci <= ri, 1.0, 0.0).astype(BF16)

    m_ref[...] = jnp.full(m_ref.shape, -1e30, F32)
    acc_ref[...] = jnp.zeros(acc_ref.shape, F32)

    sub = ka // kc
    n_a = (n_c + sub - 1) // sub

    def set_bias(c):
        for u in range(sub):
            r0 = pl.multiple_of(c * ka + u * kc, kc)
            sc = st_ref[pl.ds(r0, kc), :]
            srow = srow0 + r0
            tie = sc == tau1
            seen = jnp.dot(ltri_ref[...], jnp.where(tie, 1.0, 0.0).astype(BF16), preferred_element_type=F32)
            left = left_ref[0:1, :]
            mask = ((sc > tau1) | (tie & (seen <= left))) & (srow <= tcol)
            bias_ref[u * kc:(u + 1) * kc, :] = jnp.where(mask, 0.0, -jnp.inf)
            left_ref[...] = jnp.broadcast_to(left - seen[kc - 1:kc, :], left_ref.shape)

    def logits_stage(c, hh):
        r0 = pl.multiple_of(c * ka, ka)
        kg = k4_ref[0, hh // GROUP, pl.ds(r0, ka), :]
        lg = jnp.dot(kg, qT_ref[0, hh], preferred_element_type=F32) + bias_ref[...]
        lg_ref[hh] = lg
        mx = jnp.max(lg.reshape(ka // SUBLANES, SUBLANES, tq), axis=0)
        m_old = m_ref[hh:hh + 1, :]
        m_new = jnp.maximum(m_old, jnp.max(mx, axis=0, keepdims=True))
        m_ref[hh:hh + 1, :] = m_new
        al_ref[hh:hh + 1, :] = jnp.exp2(m_old - m_new)

    def value_stage(c, hh):
        vt = jnp.concatenate([vT_ref[0, hh // GROUP, c * sub + u] for u in range(sub)], axis=1)
        p = jnp.exp2(lg_ref[hh] - m_ref[hh:hh + 1, :]).astype(BF16)
        acc_ref[hh] = acc_ref[hh] * al_ref[hh:hh + 1, :] + jnp.dot(vt, p, preferred_element_type=F32)

    set_bias(0)
    for hh in range(N_HEADS):
        logits_stage(0, hh)

    def attn_body(c, carry):
        set_bias(c)
        for hh in range(N_HEADS):
            value_stage(c - 1, hh)
            logits_stage(c, hh)
        return carry

    lax.fori_loop(1, n_a, attn_body, 0)
    for hh in range(N_HEADS):
        value_stage(n_a - 1, hh)

    for hp in range(N_HEADS // 2):
        parts = []
        for hh in (2 * hp, 2 * hp + 1):
            a = acc_ref[hh]
            parts.append(a[0:HEAD_DIM] / a[HEAD_DIM:HEAD_DIM + 1])
        pair = jnp.concatenate(parts, axis=0)
        o_ref[0, :, LANES * hp:LANES * (hp + 1)] = pair.T.astype(BF16)


def _dsa_attention(qT, qiT, wiT, k4, ki, vT, bn, t_len, topk):
    tq = topk
    rb = 512
    ka = 256
    kern = functools.partial(_dsa_kernel, topk=topk, rb=rb, ka=ka)
    return pl.pallas_call(
        kern,
        grid=(bn, t_len // tq),
        in_specs=[pl.BlockSpec((1, N_HEADS, HEAD_DIM, tq), lambda b, i: (b, 0, 0, i)),
                  pl.BlockSpec((1, IDX_HEADS, IDX_DIM, tq), lambda b, i: (b, 0, 0, i)),
                  pl.BlockSpec((1, IDX_HEADS, tq), lambda b, i: (b, 0, i)),
                  pl.BlockSpec((1, N_KV_HEADS, t_len, HEAD_DIM), lambda b, i: (b, 0, 0, 0),
                               pipeline_mode=pl.Buffered(1)),
                  pl.BlockSpec((1, t_len, IDX_DIM), lambda b, i: (b, 0, 0), pipeline_mode=pl.Buffered(1)),
                  pl.BlockSpec((1, N_KV_HEADS, t_len // tq, V_AUG, tq), lambda b, i: (b, 0, 0, 0, 0),
                               pipeline_mode=pl.Buffered(1))],
        out_specs=pl.BlockSpec((1, tq, Q_COLS), lambda b, i: (b, i, 0)),
        out_shape=jax.ShapeDtypeStruct((bn, t_len, Q_COLS), BF16),
        scratch_shapes=[pltpu.VMEM((t_len, tq), F32),
                        pltpu.VMEM((t_len, tq), BF16),
                        pltpu.VMEM((N_HEADS, tq), F32),
                        pltpu.VMEM((N_HEADS, tq), F32),
                        pltpu.VMEM((N_HEADS, V_AUG, tq), F32),
                        pltpu.VMEM((ka, tq), F32),
                        pltpu.VMEM((N_HEADS, ka, tq), F32),
                        pltpu.VMEM((SUBLANES, tq), F32),
                        pltpu.VMEM((tq, tq), BF16)],
        compiler_params=_params(2),
        name="dsa_attention",
    )(qT, qiT, wiT, k4, ki.reshape(bn, t_len, IDX_DIM), vT)


def _conv_pre_kernel(x_ref, mod_ref, g1_ref, w_ref, cw_ref, o_ref, zbuf_ref, *, tpb):
    tm, d = x_ref.shape
    i = pl.program_id(0)

    @pl.when(i % tpb == 0)
    def _():
        zbuf_ref[0:SUBLANES, :] = jnp.zeros((SUBLANES, d), F32)

    x = x_ref[...]
    sh = mod_ref[0, 0:1, :]
    sc = mod_ref[0, 1:2, :]
    ms = jnp.mean(x * x, axis=-1, keepdims=True)
    h = (x * lax.rsqrt(ms + NORM_EPS) * g1_ref[...]) * (1.0 + sc) + sh
    proj = jnp.dot(h.astype(BF16), w_ref[...], preferred_element_type=F32)
    b_gate = proj[:, 0:d]
    z = proj[:, d:2 * d] * proj[:, 2 * d:3 * d]
    zbuf_ref[SUBLANES:SUBLANES + tm, :] = z
    z1 = zbuf_ref[SUBLANES - 1:SUBLANES - 1 + tm, :]
    z2 = zbuf_ref[SUBLANES - 2:SUBLANES - 2 + tm, :]
    zc = cw_ref[0:1, :] * z2 + cw_ref[1:2, :] * z1 + cw_ref[2:3, :] * z
    o_ref[...] = (b_gate * zc).astype(BF16)
    zbuf_ref[0:SUBLANES, :] = z[tm - SUBLANES:tm, :]


def _conv_pre(x2d, mod, g1, w_in, conv_w, t_len, tm):
    n, d = x2d.shape
    tpb = t_len // tm
    row = lambda i: (i, 0)
    return pl.pallas_call(
        functools.partial(_conv_pre_kernel, tpb=tpb),
        grid=(n // tm,),
        in_specs=[pl.BlockSpec((tm, d), row),
                  pl.BlockSpec((1, 6, d), lambda i: (i // tpb, 0, 0)),
                  _const_spec((1, d)),
                  _const_spec((d, 3 * d)),
                  _const_spec((CONV_WIDTH, d))],
        out_specs=pl.BlockSpec((tm, d), row),
        out_shape=jax.ShapeDtypeStruct((n, d), BF16),
        scratch_shapes=[pltpu.VMEM((tm + SUBLANES, d), F32)],
        compiler_params=_params(1),
        name="conv_pre",
    )(x2d, mod, g1.reshape(1, d), w_in.astype(BF16), conv_w.astype(F32))


def _post_kernel(x_ref, mix_ref, mod_ref, g2_ref, wo_ref, wg_ref, wu_ref, wd_ref, o_ref, *, th):
    gate1 = mod_ref[0, 2:3, :]
    sh2 = mod_ref[0, 3:4, :]
    sc2 = mod_ref[0, 4:5, :]
    gate2 = mod_ref[0, 5:6, :]
    y = jnp.dot(mix_ref[...], wo_ref[...], preferred_element_type=F32)
    x1 = x_ref[...] + gate1 * y
    ms = jnp.mean(x1 * x1, axis=-1, keepdims=True)
    h = ((x1 * lax.rsqrt(ms + NORM_EPS) * g2_ref[...]) * (1.0 + sc2) + sh2).astype(BF16)
    hidden = wg_ref.shape[1]
    acc = jnp.zeros(x1.shape, F32)
    for j in range(hidden // th):
        gt = jnp.dot(h, wg_ref[:, th * j:th * (j + 1)], preferred_element_type=F32)
        up = jnp.dot(h, wu_ref[:, th * j:th * (j + 1)], preferred_element_type=F32)
        a = (gt * jax.nn.sigmoid(gt)) * up
        acc = acc + jnp.dot(a.astype(BF16), wd_ref[th * j:th * (j + 1), :], preferred_element_type=F32)
    o_ref[...] = x1 + gate2 * acc


def _post(x2d, mix, mod, g2, w_out, w_gate, w_up, w_down, t_len, tm):
    n, d = x2d.shape
    hidden = w_gate.shape[1]
    tpb = t_len // tm
    row = lambda i: (i, 0)
    return pl.pallas_call(
        functools.partial(_post_kernel, th=256),
        grid=(n // tm,),
        in_specs=[pl.BlockSpec((tm, d), row),
                  pl.BlockSpec((tm, d), row),
                  pl.BlockSpec((1, 6, d), lambda i: (i // tpb, 0, 0)),
                  _const_spec((1, d)),
                  _const_spec((d, d)),
                  _const_spec((d, hidden)),
                  _const_spec((d, hidden)),
                  _const_spec((hidden, d))],
        out_specs=pl.BlockSpec((tm, d), row),
        out_shape=jax.ShapeDtypeStruct((n, d), F32),
        compiler_params=_params(1),
        name="mixer_out_ffn",
    )(x2d, mix, mod, g2.reshape(1, d), w_out.astype(BF16), w_gate.astype(BF16), w_up.astype(BF16),
      w_down.astype(BF16))


def kernel(x, c, positions, ada_w, ada_b, norm1_g, norm2_g, attn_w_in, attn_q_norm_g, attn_k_norm_g,
           idx_k_ln_g, idx_k_ln_b, attn_w_out, conv_w_in, conv_w, conv_w_out, ffn_w_gate, ffn_w_up,
           ffn_w_down):
    bn, t_len, d = x.shape
    depth = ada_w.shape[0]
    topk = min(TOPK_MAX, t_len // 4)
    assert topk == TOPK_MAX and t_len % topk == 0 and (t_len & (t_len - 1)) == 0
    n = bn * t_len
    tm_pre = 256
    tm_post = 512
    mod = _modulation(c, ada_w, ada_b)
    x2d = x.reshape(n, d)
    for i in range(depth):
        j = i // 2
        if i % 2 == 0:
            qT, qiT, wiT, vT, k2, kiP = _attn_pre(
                x2d, mod[i], norm1_g[i], attn_w_in[j], attn_q_norm_g[j], attn_k_norm_g[j],
                idx_k_ln_g[j], idx_k_ln_b[j], positions, bn, t_len, tm_pre)
            mix = _dsa_attention(qT, qiT, wiT, k2, kiP, vT, bn, t_len, topk).reshape(n, d)
            w_mix_out = attn_w_out[j]
        else:
            mix = _conv_pre(x2d, mod[i], norm1_g[i], conv_w_in[j], conv_w[j], t_len, tm_post)
            w_mix_out = conv_w_out[j]
        x2d = _post(x2d, mix, mod[i], norm2_g[i], w_mix_out, ffn_w_gate[i], ffn_w_up[i], ffn_w_down[i],
                    t_len, tm_post)
    return x2d.reshape(bn, t_len, d)
```

```python
import functools
import math

import numpy as np
import jax
import jax.numpy as jnp
from jax import lax
from jax.experimental import pallas as pl
from jax.experimental.pallas import tpu as pltpu

F32 = jnp.float32
BF16 = jnp.bfloat16
I32 = jnp.int32

N_HEADS = 16
HEAD_DIM = 64
N_KV_HEADS = 4
GROUP = N_HEADS // N_KV_HEADS
IDX_HEADS = 8
IDX_DIM = 64
TOPK_MAX = 256
ROPE_THETA = 10000.0
CONV_WIDTH = 3
NORM_EPS = 1e-6
LOG2E = math.log2(math.e)

LANES = 128
SUBLANES = 8
VMEM_LIMIT = 56 * 1024 * 1024

Q_COLS = N_HEADS * HEAD_DIM
KV_COLS = N_KV_HEADS * HEAD_DIM
QI_COLS = IDX_HEADS * IDX_DIM
V_AUG = HEAD_DIM + 16

INT_MIN = np.int32(-2**31)
KEY_NEG_INF = np.int32(np.uint32(0x807FFFFF).astype(np.int64) - 2**32)
KEY_HI_NEG_INF = np.int32(np.uint32(0x807F0000).astype(np.int64) - 2**32)


def _const_spec(shape):
    nd = len(shape)
    return pl.BlockSpec(shape, lambda *_: (0,) * nd, pipeline_mode=pl.Buffered(1))


def _params(n_axes):
    return pltpu.CompilerParams(dimension_semantics=("arbitrary",) * n_axes,
                                vmem_limit_bytes=VMEM_LIMIT)


def _mod_kernel(c_ref, w_ref, b_ref, o_ref):
    c = c_ref[...]
    ca = c * jax.nn.sigmoid(c)
    o_ref[0] = jnp.dot(ca.astype(BF16), w_ref[0].astype(BF16), preferred_element_type=F32) + b_ref[0]


def _modulation(c, ada_w, ada_b):
    depth, d, six_d = ada_w.shape
    bn = c.shape[0]
    rows = 16
    cp = jnp.zeros((rows, d), F32).at[:bn].set(c)
    tn = 1536
    out = pl.pallas_call(
        _mod_kernel,
        grid=(depth, six_d // tn),
        in_specs=[pl.BlockSpec((rows, d), lambda i, j: (0, 0)),
                  pl.BlockSpec((1, d, tn), lambda i, j: (i, 0, j)),
                  pl.BlockSpec((1, 1, tn), lambda i, j: (i, 0, j))],
        out_specs=pl.BlockSpec((1, rows, tn), lambda i, j: (i, 0, j)),
        out_shape=jax.ShapeDtypeStruct((depth, rows, six_d), F32),
        compiler_params=_params(2),
        name="adaln_mod",
    )(cp, ada_w, ada_b.reshape(depth, 1, six_d))
    return out[:, :bn].reshape(depth, bn, 6, d)


def _attn_pre_kernel(x_ref, mod_ref, g1_ref, wt_ref, posr_ref, invft_ref, gq_ref, gk_ref, lng_ref, lnb_ref,
                     qT_ref, qiT_ref, wiT_ref, vT_ref, k4_ref, ki_ref):
    tm = x_ref.shape[0]
    x = x_ref[...]
    sh = mod_ref[0, 0:1, :]
    sc = mod_ref[0, 1:2, :]
    ms = jnp.mean(x * x, axis=-1, keepdims=True)
    h = (x * lax.rsqrt(ms + NORM_EPS) * g1_ref[...]) * (1.0 + sc) + sh
    hb = h.astype(BF16)

    pT = lax.dot_general(wt_ref[...], hb, (((1,), (1,)), ((), ())), preferred_element_type=F32)
    angT = invft_ref[...] * posr_ref[0].astype(F32)
    cT = jnp.cos(angT)
    sT = jnp.sin(angT)
    half = HEAD_DIM // 2

    def rope_t(y):
        x1 = y[:half]
        x2 = y[half:]
        return jnp.concatenate([x1 * cT - x2 * sT, x2 * cT + x1 * sT], axis=0)

    gq = gq_ref[...]
    for hh in range(N_HEADS):
        xq = pT[HEAD_DIM * hh:HEAD_DIM * (hh + 1)]
        inv = lax.rsqrt(jnp.mean(xq * xq, axis=0, keepdims=True) + NORM_EPS)
        y = xq * inv * gq
        qT_ref[0, hh] = (rope_t(y) * (HEAD_DIM ** -0.5 * LOG2E)).astype(BF16)
    o0 = Q_COLS
    gk = gk_ref[...]
    for j in range(N_KV_HEADS // 2):
        pair = []
        for g in (2 * j, 2 * j + 1):
            xk = pT[o0 + HEAD_DIM * g:o0 + HEAD_DIM * (g + 1)]
            inv = lax.rsqrt(jnp.mean(xk * xk, axis=0, keepdims=True) + NORM_EPS)
            pair.append(rope_t(xk * inv * gk))
        kr = jnp.concatenate(pair, axis=0).T
        k4_ref[0, 2 * j] = kr[:, 0:HEAD_DIM].astype(BF16)
        k4_ref[0, 2 * j + 1] = kr[:, HEAD_DIM:LANES].astype(BF16)
    o0 += KV_COLS
    ones = jnp.ones((V_AUG - HEAD_DIM, tm), BF16)
    for g in range(N_KV_HEADS):
        vT_ref[0, g, 0, 0:HEAD_DIM, :] = pT[o0 + HEAD_DIM * g:o0 + HEAD_DIM * (g + 1)].astype(BF16)
        vT_ref[0, g, 0, HEAD_DIM:V_AUG, :] = ones
    o0 += KV_COLS
    for hh in range(IDX_HEADS):
        qiT_ref[0, hh] = rope_t(pT[o0 + IDX_DIM * hh:o0 + IDX_DIM * (hh + 1)]).astype(BF16)
    o0 += QI_COLS
    t = pT[o0:o0 + IDX_DIM]
    mu = jnp.mean(t, axis=0, keepdims=True)
    dlt = t - mu
    var = jnp.mean(dlt * dlt, axis=0, keepdims=True)
    kin = rope_t(dlt * lax.rsqrt(var + NORM_EPS) * lng_ref[...] + lnb_ref[...])
    kin = jnp.concatenate([kin, jnp.zeros((LANES - IDX_DIM, tm), F32)], axis=0).T
    ki_ref[...] = kin[:, 0:IDX_DIM].astype(BF16)
    o0 += IDX_DIM
    wiT_ref[0] = pT[o0:o0 + IDX_HEADS] * (IDX_HEADS ** -0.5 * IDX_DIM ** -0.5)


def _attn_pre(x2d, mod, g1, w_in, gq, gk, lng, lnb, positions, bn, t_len, tm):
    n, d = x2d.shape
    tpb = t_len // tm
    half = HEAD_DIM // 2
    ma = w_in.shape[1]
    ma_pad = -(-ma // 16) * 16
    wt = jnp.concatenate([w_in, jnp.zeros((d, ma_pad - ma), F32)], axis=1).T.astype(BF16)
    inv_freq = ROPE_THETA ** (-(jnp.arange(half, dtype=F32) * 2.0 / HEAD_DIM))
    invft = jnp.broadcast_to(inv_freq[:, None], (half, tm))
    col = lambda v: jnp.broadcast_to(v.astype(F32)[:, None], (v.shape[0], tm))
    posr = positions.reshape(bn, 1, t_len)

    row = lambda i: (i, 0)
    outs = pl.pallas_call(
        _attn_pre_kernel,
        grid=(n // tm,),
        in_specs=[pl.BlockSpec((tm, d), row),
                  pl.BlockSpec((1, 6, d), lambda i: (i // tpb, 0, 0)),
                  _const_spec((1, d)),
                  _const_spec((ma_pad, d)),
                  pl.BlockSpec((1, 1, tm), lambda i: (i // tpb, 0, i % tpb)),
                  _const_spec((half, tm)),
                  _const_spec((HEAD_DIM, tm)),
                  _const_spec((HEAD_DIM, tm)),
                  _const_spec((IDX_DIM, tm)),
                  _const_spec((IDX_DIM, tm))],
        out_specs=[pl.BlockSpec((1, N_HEADS, HEAD_DIM, tm), lambda i: (i // tpb, 0, 0, i % tpb)),
                   pl.BlockSpec((1, IDX_HEADS, IDX_DIM, tm), lambda i: (i // tpb, 0, 0, i % tpb)),
                   pl.BlockSpec((1, IDX_HEADS, tm), lambda i: (i // tpb, 0, i % tpb)),
                   pl.BlockSpec((1, N_KV_HEADS, 1, V_AUG, tm), lambda i: (i // tpb, 0, i % tpb, 0, 0)),
                   pl.BlockSpec((1, N_KV_HEADS, tm, HEAD_DIM), lambda i: (i // tpb, 0, i % tpb, 0)),
                   pl.BlockSpec((tm, IDX_DIM), row)],
        out_shape=[jax.ShapeDtypeStruct((bn, N_HEADS, HEAD_DIM, t_len), BF16),
                   jax.ShapeDtypeStruct((bn, IDX_HEADS, IDX_DIM, t_len), BF16),
                   jax.ShapeDtypeStruct((bn, IDX_HEADS, t_len), F32),
                   jax.ShapeDtypeStruct((bn, N_KV_HEADS, tpb, V_AUG, tm), BF16),
                   jax.ShapeDtypeStruct((bn, N_KV_HEADS, t_len, HEAD_DIM), BF16),
                   jax.ShapeDtypeStruct((n, IDX_DIM), BF16)],
        compiler_params=_params(1),
        name="attn_pre",
    )(x2d, mod, g1.reshape(1, d), wt, posr, invft, col(gq), col(gk), col(lng), col(lnb))
    return outs


def _dsa_kernel(qT_ref, qiT_ref, wiT_ref, k4_ref, ki_ref, vT_ref, o_ref,
                st_ref, hi_ref, m_ref, al_ref, acc_ref, bias_ref, lg_ref, left_ref, ltri_ref, *, topk, rb, ka):
    tq = o_ref.shape[1]
    kc = tq
    i = pl.program_id(1)
    n_c = i + 1
    q0 = i * tq

    tcol = q0 + lax.broadcasted_iota(I32, (kc, tq), 1)
    srow0 = lax.broadcasted_iota(I32, (kc, tq), 0)
    wi = wiT_ref[0]
    qi_all = jnp.concatenate([qiT_ref[0, hh] for hh in range(IDX_HEADS)], axis=1)

    def score_chunk(c):
        r0 = pl.multiple_of(c * kc, kc)
        ki = ki_ref[0, pl.ds(r0, kc), :]
        s_all = jnp.dot(ki, qi_all, preferred_element_type=F32)
        acc = jnp.zeros((kc, tq), F32)
        for hh in range(IDX_HEADS):
            acc = acc + jnp.maximum(s_all[:, tq * hh:tq * (hh + 1)], 0.0) * wi[hh:hh + 1, :]
        causal = (srow0 + r0) <= tcol
        sc = jnp.where(causal, acc, -jnp.inf)
        st_ref[pl.ds(r0, kc), :] = sc
        top = lax.bitcast_convert_type(sc, I32) & np.int32(-65536)
        hi_ref[pl.ds(r0, kc), :] = lax.bitcast_convert_type(top, F32).astype(BF16)

    def score_body(j, carry):
        score_chunk(2 * j)
        score_chunk(2 * j + 1)
        return carry

    lax.fori_loop(0, (n_c + 1) // 2, score_body, 0)

    n_rb = (n_c * kc + rb - 1) // rb
    n_par = 4
    pack = 2 * SUBLANES

    def all_sublanes(tot):
        for sh in (4, 2, 1):
            tot = tot + pltpu.roll(tot, sh, 0)
        return tot

    def count(pred):
        def body(r, acc):
            r0 = pl.multiple_of(r * rb, rb)
            blk = st_ref[pl.ds(r0, rb), :].reshape(rb // SUBLANES, SUBLANES, tq)
            hit = pred(blk, r0).astype(I32).reshape(rb // (SUBLANES * n_par), n_par, SUBLANES, tq)
            return acc + jnp.sum(hit, axis=0)
        acc = lax.fori_loop(0, n_rb, body, jnp.zeros((n_par, SUBLANES, tq), I32))
        return all_sublanes(jnp.sum(acc, axis=0))

    def count_hi(cand16):
        def body(r, acc):
            r0 = pl.multiple_of(r * rb, rb)
            blk = hi_ref[pl.ds(r0, rb), :].reshape(rb // pack, pack, tq)
            hit = jnp.where(blk >= cand16[None], jnp.ones((), BF16), jnp.zeros((), BF16))
            parts = [hit[g] for g in range(rb // pack)]
            while len(parts) > 1:
                parts = [parts[g] + parts[g + 1] for g in range(0, len(parts), 2)]
            return acc + parts[0].astype(F32)
        acc = lax.fori_loop(0, n_rb, body, jnp.zeros((pack, tq), F32))
        return all_sublanes((acc[0:SUBLANES] + acc[SUBLANES:pack]).astype(I32))

    def key_to_float(u):
        ks = u ^ INT_MIN
        bits = jnp.where(ks < 0, ks ^ np.int32(0x7FFFFFFF), ks)
        return ks, bits

    def hi_body(it, carry):
        tau_u, cnt_tau = carry
        cand_u = tau_u | lax.shift_left(np.int32(1), np.int32(31) - it)
        ks, bits = key_to_float(cand_u)
        cand_top = lax.bitcast_convert_type(bits & np.int32(-65536), F32)
        cand16 = jnp.concatenate([cand_top, cand_top], axis=0).astype(BF16)
        cnt = count_hi(cand16)
        take = (cnt >= topk) | (ks < KEY_HI_NEG_INF)
        return jnp.where(take, cand_u, tau_u), jnp.where(take, cnt, cnt_tau)

    def lo_body(it, carry):
        tau_u, cnt_tau = carry
        cand_u = tau_u | lax.shift_left(np.int32(1), np.int32(15) - it)
        ks, bits = key_to_float(cand_u)
        cand_f = lax.bitcast_convert_type(bits, F32)
        cnt = count(lambda blk, r0: blk >= cand_f[None])
        take = (cnt >= topk) | (ks < KEY_NEG_INF)
        return jnp.where(take, cand_u, tau_u), jnp.where(take, cnt, cnt_tau)

    def lo_stage(state, first, n):
        tau_u, cnt_tau, _ = state
        tau_u, cnt_tau = lax.fori_loop(first, first + n, lo_body, (tau_u, cnt_tau))
        tau8 = lax.bitcast_convert_type(key_to_float(tau_u)[1], F32)
        return tau_u, cnt_tau, count(lambda blk, r0: blk > tau8[None])

    def pending(state):
        tau_u, cnt_tau, cnt_gt = state
        open_lane = ((cnt_tau != topk) & (cnt_gt >= topk)) | ((tau_u ^ INT_MIN) < KEY_NEG_INF)
        return jnp.max(open_lane.astype(I32)) > 0

    zero8 = jnp.zeros((SUBLANES, tq), I32)
    tau_u, cnt_tau = lax.fori_loop(0, 16, hi_body, (zero8, zero8))
    state = lo_stage((tau_u, cnt_tau, zero8), 0, 10)
    state = lax.cond(pending(state), lambda s: lo_stage(s, 10, 2), lambda s: s, state)
    state = lax.cond(pending(state), lambda s: lo_stage(s, 12, 4), lambda s: s, state)
    tau_u, _, cnt_gt = state
    tau8 = lax.bitcast_convert_type(key_to_float(tau_u)[1], F32)
    left_ref[...] = jnp.where(tau8 == -jnp.inf, 0.0, (topk - cnt_gt).astype(F32))
    tau1 = tau8[0:1]
    ri = lax.broadcasted_iota(I32, (kc, kc), 0)
    ci = lax.broadcasted_iota(I32, (kc, kc), 1)
    ltri_ref[...] = jnp.where(ci <= ri, 1.0, 0.0).astype(BF16)

    m_ref[...] = jnp.full(m_ref.shape, -1e30, F32)
    acc_ref[...] = jnp.zeros(acc_ref.shape, F32)

    sub = ka // kc
    n_a = (n_c + sub - 1) // sub

    def set_bias(c):
        for u in range(sub):
            r0 = pl.multiple_of(c * ka + u * kc, kc)
            sc = st_ref[pl.ds(r0, kc), :]
            tie = sc == tau1
            seen = jnp.dot(ltri_ref[...], jnp.where(tie, 1.0, 0.0).astype(BF16), preferred_element_type=F32)
            left = left_ref[0:1, :]
            mask = (sc > tau1) | (tie & (seen <= left))
            bias_ref[u * kc:(u + 1) * kc, :] = jnp.where(mask, 0.0, -jnp.inf)
            left_ref[...] = jnp.broadcast_to(left - seen[kc - 1:kc, :], left_ref.shape)

    def logits_stage(c, hh):
        r0 = pl.multiple_of(c * ka, ka)
        kg = k4_ref[0, hh // GROUP, pl.ds(r0, ka), :]
        lg = jnp.dot(kg, qT_ref[0, hh], preferred_element_type=F32) + bias_ref[...]
        lg_ref[hh] = lg
        mx = jnp.max(lg.reshape(ka // SUBLANES, SUBLANES, tq), axis=0)
        m_old = m_ref[hh:hh + 1, :]
        m_new = jnp.maximum(m_old, jnp.max(mx, axis=0, keepdims=True))
        m_ref[hh:hh + 1, :] = m_new
        al_ref[hh:hh + 1, :] = jnp.exp2(m_old - m_new)

    def value_stage(c, hh):
        vt = jnp.concatenate([vT_ref[0, hh // GROUP, c * sub + u] for u in range(sub)], axis=1)
        p = jnp.exp2(lg_ref[hh] - m_ref[hh:hh + 1, :]).astype(BF16)
        acc_ref[hh] = acc_ref[hh] * al_ref[hh:hh + 1, :] + jnp.dot(vt, p, preferred_element_type=F32)

    set_bias(0)
    for hh in range(N_HEADS):
        logits_stage(0, hh)

    def attn_body(c, carry):
        set_bias(c)
        for hh in range(N_HEADS):
            value_stage(c - 1, hh)
            logits_stage(c, hh)
        return carry

    lax.fori_loop(1, n_a, attn_body, 0)
    for hh in range(N_HEADS):
        value_stage(n_a - 1, hh)

    for hp in range(N_HEADS // 2):
        parts = []
        for hh in (2 * hp, 2 * hp + 1):
            a = acc_ref[hh]
            parts.append(a[0:HEAD_DIM] / a[HEAD_DIM:HEAD_DIM + 1])
        pair = jnp.concatenate(parts, axis=0)
        o_ref[0, :, LANES * hp:LANES * (hp + 1)] = pair.T.astype(BF16)


def _dsa_attention(qT, qiT, wiT, k4, ki, vT, bn, t_len, topk):
    tq = topk
    rb = 512
    ka = 256
    kern = functools.partial(_dsa_kernel, topk=topk, rb=rb, ka=ka)
    return pl.pallas_call(
        kern,
        grid=(bn, t_len // tq),
        in_specs=[pl.BlockSpec((1, N_HEADS, HEAD_DIM, tq), lambda b, i: (b, 0, 0, i)),
                  pl.BlockSpec((1, IDX_HEADS, IDX_DIM, tq), lambda b, i: (b, 0, 0, i)),
                  pl.BlockSpec((1, IDX_HEADS, tq), lambda b, i: (b, 0, i)),
                  pl.BlockSpec((1, N_KV_HEADS, t_len, HEAD_DIM), lambda b, i: (b, 0, 0, 0),
                               pipeline_mode=pl.Buffered(1)),
                  pl.BlockSpec((1, t_len, IDX_DIM), lambda b, i: (b, 0, 0), pipeline_mode=pl.Buffered(1)),
                  pl.BlockSpec((1, N_KV_HEADS, t_len // tq, V_AUG, tq), lambda b, i: (b, 0, 0, 0, 0),
                               pipeline_mode=pl.Buffered(1))],
        out_specs=pl.BlockSpec((1, tq, Q_COLS), lambda b, i: (b, i, 0)),
        out_shape=jax.ShapeDtypeStruct((bn, t_len, Q_COLS), BF16),
        scratch_shapes=[pltpu.VMEM((t_len, tq), F32),
                        pltpu.VMEM((t_len, tq), BF16),
                        pltpu.VMEM((N_HEADS, tq), F32),
                        pltpu.VMEM((N_HEADS, tq), F32),
                        pltpu.VMEM((N_HEADS, V_AUG, tq), F32),
                        pltpu.VMEM((ka, tq), F32),
                        pltpu.VMEM((N_HEADS, ka, tq), F32),
                        pltpu.VMEM((SUBLANES, tq), F32),
                        pltpu.VMEM((tq, tq), BF16)],
        compiler_params=_params(2),
        name="dsa_attention",
    )(qT, qiT, wiT, k4, ki.reshape(bn, t_len, IDX_DIM), vT)


def _conv_pre_kernel(x_ref, mod_ref, g1_ref, w_ref, cw_ref, o_ref, zbuf_ref, *, tpb):
    tm, d = x_ref.shape
    i = pl.program_id(0)

    @pl.when(i % tpb == 0)
    def _():
        zbuf_ref[0:SUBLANES, :] = jnp.zeros((SUBLANES, d), F32)

    x = x_ref[...]
    sh = mod_ref[0, 0:1, :]
    sc = mod_ref[0, 1:2, :]
    ms = jnp.mean(x * x, axis=-1, keepdims=True)
    h = (x * lax.rsqrt(ms + NORM_EPS) * g1_ref[...]) * (1.0 + sc) + sh
    proj = jnp.dot(h.astype(BF16), w_ref[...], preferred_element_type=F32)
    b_gate = proj[:, 0:d]
    z = proj[:, d:2 * d] * proj[:, 2 * d:3 * d]
    zbuf_ref[SUBLANES:SUBLANES + tm, :] = z
    z1 = zbuf_ref[SUBLANES - 1:SUBLANES - 1 + tm, :]
    z2 = zbuf_ref[SUBLANES - 2:SUBLANES - 2 + tm, :]
    zc = cw_ref[0:1, :] * z2 + cw_ref[1:2, :] * z1 + cw_ref[2:3, :] * z
    o_ref[...] = (b_gate * zc).astype(BF16)
    zbuf_ref[0:SUBLANES, :] = z[tm - SUBLANES:tm, :]


def _conv_pre(x2d, mod, g1, w_in, conv_w, t_len, tm):
    n, d = x2d.shape
    tpb = t_len // tm
    row = lambda i: (i, 0)
    return pl.pallas_call(
        functools.partial(_conv_pre_kernel, tpb=tpb),
        grid=(n // tm,),
        in_specs=[pl.BlockSpec((tm, d), row),
                  pl.BlockSpec((1, 6, d), lambda i: (i // tpb, 0, 0)),
                  _const_spec((1, d)),
                  _const_spec((d, 3 * d)),
                  _const_spec((CONV_WIDTH, d))],
        out_specs=pl.BlockSpec((tm, d), row),
        out_shape=jax.ShapeDtypeStruct((n, d), BF16),
        scratch_shapes=[pltpu.VMEM((tm + SUBLANES, d), F32)],
        compiler_params=_params(1),
        name="conv_pre",
    )(x2d, mod, g1.reshape(1, d), w_in.astype(BF16), conv_w.astype(F32))


def _post_kernel(x_ref, mix_ref, mod_ref, g2_ref, wo_ref, wg_ref, wu_ref, wd_ref, o_ref, *, th):
    gate1 = mod_ref[0, 2:3, :]
    sh2 = mod_ref[0, 3:4, :]
    sc2 = mod_ref[0, 4:5, :]
    gate2 = mod_ref[0, 5:6, :]
    y = jnp.dot(mix_ref[...], wo_ref[...], preferred_element_type=F32)
    x1 = x_ref[...] + gate1 * y
    ms = jnp.mean(x1 * x1, axis=-1, keepdims=True)
    h = ((x1 * lax.rsqrt(ms + NORM_EPS) * g2_ref[...]) * (1.0 + sc2) + sh2).astype(BF16)
    hidden = wg_ref.shape[1]
    acc = jnp.zeros(x1.shape, F32)
    for j in range(hidden // th):
        gt = jnp.dot(h, wg_ref[:, th * j:th * (j + 1)], preferred_element_type=F32)
        up = jnp.dot(h, wu_ref[:, th * j:th * (j + 1)], preferred_element_type=F32)
        a = (gt * jax.nn.sigmoid(gt)) * up
        acc = acc + jnp.dot(a.astype(BF16), wd_ref[th * j:th * (j + 1), :], preferred_element_type=F32)
    o_ref[...] = x1 + gate2 * acc


def _post(x2d, mix, mod, g2, w_out, w_gate, w_up, w_down, t_len, tm):
    n, d = x2d.shape
    hidden = w_gate.shape[1]
    tpb = t_len // tm
    row = lambda i: (i, 0)
    return pl.pallas_call(
        functools.partial(_post_kernel, th=256),
        grid=(n // tm,),
        in_specs=[pl.BlockSpec((tm, d), row),
                  pl.BlockSpec((tm, d), row),
                  pl.BlockSpec((1, 6, d), lambda i: (i // tpb, 0, 0)),
                  _const_spec((1, d)),
                  _const_spec((d, d)),
                  _const_spec((d, hidden)),
                  _const_spec((d, hidden)),
                  _const_spec((hidden, d))],
        out_specs=pl.BlockSpec((tm, d), row),
        out_shape=jax.ShapeDtypeStruct((n, d), F32),
        compiler_params=_params(1),
        name="mixer_out_ffn",
    )(x2d, mix, mod, g2.reshape(1, d), w_out.astype(BF16), w_gate.astype(BF16), w_up.astype(BF16),
      w_down.astype(BF16))


def kernel(x, c, positions, ada_w, ada_b, norm1_g, norm2_g, attn_w_in, attn_q_norm_g, attn_k_norm_g,
           idx_k_ln_g, idx_k_ln_b, attn_w_out, conv_w_in, conv_w, conv_w_out, ffn_w_gate, ffn_w_up,
           ffn_w_down):
    bn, t_len, d = x.shape
    depth = ada_w.shape[0]
    topk = min(TOPK_MAX, t_len // 4)
    assert topk == TOPK_MAX and t_len % topk == 0 and (t_len & (t_len - 1)) == 0
    n = bn * t_len
    tm_pre = 256
    tm_post = 512
    mod = _modulation(c, ada_w, ada_b)
    x2d = x.reshape(n, d)
    for i in range(depth):
        j = i // 2
        if i % 2 == 0:
            qT, qiT, wiT, vT, k2, kiP = _attn_pre(
                x2d, mod[i], norm1_g[i], attn_w_in[j], attn_q_norm_g[j], attn_k_norm_g[j],
                idx_k_ln_g[j], idx_k_ln_b[j], positions, bn, t_len, tm_pre)
            mix = _dsa_attention(qT, qiT, wiT, k2, kiP, vT, bn, t_len, topk).reshape(n, d)
            w_mix_out = attn_w_out[j]
        else:
            mix = _conv_pre(x2d, mod[i], norm1_g[i], conv_w_in[j], conv_w[j], t_len, tm_post)
            w_mix_out = conv_w_out[j]
        x2d = _post(x2d, mix, mod[i], norm2_g[i], w_mix_out, ffn_w_gate[i], ffn_w_up[i], ffn_w_down[i],
                    t_len, tm_post)
    return x2d.reshape(bn, t_len, d)
```

```python
import functools
import math

import numpy as np
import jax
import jax.numpy as jnp
from jax import lax
from jax.experimental import pallas as pl
from jax.experimental.pallas import tpu as pltpu

F32 = jnp.float32
BF16 = jnp.bfloat16
I32 = jnp.int32

N_HEADS = 16
HEAD_DIM = 64
N_KV_HEADS = 4
GROUP = N_HEADS // N_KV_HEADS
IDX_HEADS = 8
IDX_DIM = 64
TOPK_MAX = 256
ROPE_THETA = 10000.0
CONV_WIDTH = 3
NORM_EPS = 1e-6
LOG2E = math.log2(math.e)

LANES = 128
SUBLANES = 8
VMEM_LIMIT = 56 * 1024 * 1024

Q_COLS = N_HEADS * HEAD_DIM
KV_COLS = N_KV_HEADS * HEAD_DIM
QI_COLS = IDX_HEADS * IDX_DIM
V_AUG = HEAD_DIM + 16

INT_MIN = np.int32(-2**31)
KEY_NEG_INF = np.int32(np.uint32(0x807FFFFF).astype(np.int64) - 2**32)
KEY_HI_NEG_INF = np.int32(np.uint32(0x807F0000).astype(np.int64) - 2**32)


def _const_spec(shape):
    nd = len(shape)
    return pl.BlockSpec(shape, lambda *_: (0,) * nd, pipeline_mode=pl.Buffered(1))


def _params(n_axes):
    return pltpu.CompilerParams(dimension_semantics=("arbitrary",) * n_axes,
                                vmem_limit_bytes=VMEM_LIMIT)


def _mod_kernel(c_ref, w_ref, b_ref, o_ref):
    c = c_ref[...]
    ca = c * jax.nn.sigmoid(c)
    o_ref[0] = jnp.dot(ca.astype(BF16), w_ref[0].astype(BF16), preferred_element_type=F32) + b_ref[0]


def _modulation(c, ada_w, ada_b):
    depth, d, six_d = ada_w.shape
    bn = c.shape[0]
    rows = 16
    cp = jnp.zeros((rows, d), F32).at[:bn].set(c)
    tn = 1536
    out = pl.pallas_call(
        _mod_kernel,
        grid=(depth, six_d // tn),
        in_specs=[pl.BlockSpec((rows, d), lambda i, j: (0, 0)),
                  pl.BlockSpec((1, d, tn), lambda i, j: (i, 0, j)),
                  pl.BlockSpec((1, 1, tn), lambda i, j: (i, 0, j))],
        out_specs=pl.BlockSpec((1, rows, tn), lambda i, j: (i, 0, j)),
        out_shape=jax.ShapeDtypeStruct((depth, rows, six_d), F32),
        compiler_params=_params(2),
        name="adaln_mod",
    )(cp, ada_w, ada_b.reshape(depth, 1, six_d))
    return out[:, :bn].reshape(depth, bn, 6, d)


def _attn_pre_kernel(x_ref, mod_ref, g1_ref, wt_ref, posr_ref, invft_ref, gq_ref, gk_ref, lng_ref, lnb_ref,
                     qT_ref, qiT_ref, wiT_ref, vT_ref, k4_ref, ki_ref):
    tm = x_ref.shape[0]
    x = x_ref[...]
    sh = mod_ref[0, 0:1, :]
    sc = mod_ref[0, 1:2, :]
    ms = jnp.mean(x * x, axis=-1, keepdims=True)
    h = (x * lax.rsqrt(ms + NORM_EPS) * g1_ref[...]) * (1.0 + sc) + sh
    hb = h.astype(BF16)

    pT = lax.dot_general(wt_ref[...], hb, (((1,), (1,)), ((), ())), preferred_element_type=F32)
    angT = invft_ref[...] * posr_ref[0].astype(F32)
    cT = jnp.cos(angT)
    sT = jnp.sin(angT)
    half = HEAD_DIM // 2

    def rope_t(y):
        x1 = y[:half]
        x2 = y[half:]
        return jnp.concatenate([x1 * cT - x2 * sT, x2 * cT + x1 * sT], axis=0)

    gq = gq_ref[...]
    for hh in range(N_HEADS):
        xq = pT[HEAD_DIM * hh:HEAD_DIM * (hh + 1)]
        inv = lax.rsqrt(jnp.mean(xq * xq, axis=0, keepdims=True) + NORM_EPS)
        y = xq * inv * gq
        qT_ref[0, hh] = (rope_t(y) * (HEAD_DIM ** -0.5 * LOG2E)).astype(BF16)
    o0 = Q_COLS
    gk = gk_ref[...]
    for j in range(N_KV_HEADS // 2):
        pair = []
        for g in (2 * j, 2 * j + 1):
            xk = pT[o0 + HEAD_DIM * g:o0 + HEAD_DIM * (g + 1)]
            inv = lax.rsqrt(jnp.mean(xk * xk, axis=0, keepdims=True) + NORM_EPS)
            pair.append(rope_t(xk * inv * gk))
        kr = jnp.concatenate(pair, axis=0).T
        k4_ref[0, 2 * j] = kr[:, 0:HEAD_DIM].astype(BF16)
        k4_ref[0, 2 * j + 1] = kr[:, HEAD_DIM:LANES].astype(BF16)
    o0 += KV_COLS
    ones = jnp.ones((V_AUG - HEAD_DIM, tm), BF16)
    for g in range(N_KV_HEADS):
        vT_ref[0, g, 0, 0:HEAD_DIM, :] = pT[o0 + HEAD_DIM * g:o0 + HEAD_DIM * (g + 1)].astype(BF16)
        vT_ref[0, g, 0, HEAD_DIM:V_AUG, :] = ones
    o0 += KV_COLS
    for hh in range(IDX_HEADS):
        qiT_ref[0, hh] = rope_t(pT[o0 + IDX_DIM * hh:o0 + IDX_DIM * (hh + 1)]).astype(BF16)
    o0 += QI_COLS
    t = pT[o0:o0 + IDX_DIM]
    mu = jnp.mean(t, axis=0, keepdims=True)
    dlt = t - mu
    var = jnp.mean(dlt * dlt, axis=0, keepdims=True)
    kin = rope_t(dlt * lax.rsqrt(var + NORM_EPS) * lng_ref[...] + lnb_ref[...])
    kin = jnp.concatenate([kin, jnp.zeros((LANES - IDX_DIM, tm), F32)], axis=0).T
    ki_ref[...] = kin[:, 0:IDX_DIM].astype(BF16)
    o0 += IDX_DIM
    wiT_ref[0] = pT[o0:o0 + IDX_HEADS] * (IDX_HEADS ** -0.5 * IDX_DIM ** -0.5)


def _attn_pre(x2d, mod, g1, w_in, gq, gk, lng, lnb, positions, bn, t_len, tm):
    n, d = x2d.shape
    tpb = t_len // tm
    half = HEAD_DIM // 2
    ma = w_in.shape[1]
    ma_pad = -(-ma // 16) * 16
    wt = jnp.concatenate([w_in, jnp.zeros((d, ma_pad - ma), F32)], axis=1).T.astype(BF16)
    inv_freq = ROPE_THETA ** (-(jnp.arange(half, dtype=F32) * 2.0 / HEAD_DIM))
    invft = jnp.broadcast_to(inv_freq[:, None], (half, tm))
    col = lambda v: jnp.broadcast_to(v.astype(F32)[:, None], (v.shape[0], tm))
    posr = positions.reshape(bn, 1, t_len)

    row = lambda i: (i, 0)
    outs = pl.pallas_call(
        _attn_pre_kernel,
        grid=(n // tm,),
        in_specs=[pl.BlockSpec((tm, d), row),
                  pl.BlockSpec((1, 6, d), lambda i: (i // tpb, 0, 0)),
                  _const_spec((1, d)),
                  _const_spec((ma_pad, d)),
                  pl.BlockSpec((1, 1, tm), lambda i: (i // tpb, 0, i % tpb)),
                  _const_spec((half, tm)),
                  _const_spec((HEAD_DIM, tm)),
                  _const_spec((HEAD_DIM, tm)),
                  _const_spec((IDX_DIM, tm)),
                  _const_spec((IDX_DIM, tm))],
        out_specs=[pl.BlockSpec((1, N_HEADS, HEAD_DIM, tm), lambda i: (i // tpb, 0, 0, i % tpb)),
                   pl.BlockSpec((1, IDX_HEADS, IDX_DIM, tm), lambda i: (i // tpb, 0, 0, i % tpb)),
                   pl.BlockSpec((1, IDX_HEADS, tm), lambda i: (i // tpb, 0, i % tpb)),
                   pl.BlockSpec((1, N_KV_HEADS, 1, V_AUG, tm), lambda i: (i // tpb, 0, i % tpb, 0, 0)),
                   pl.BlockSpec((1, N_KV_HEADS, tm, HEAD_DIM), lambda i: (i // tpb, 0, i % tpb, 0)),
                   pl.BlockSpec((tm, IDX_DIM), row)],
        out_shape=[jax.ShapeDtypeStruct((bn, N_HEADS, HEAD_DIM, t_len), BF16),
                   jax.ShapeDtypeStruct((bn, IDX_HEADS, IDX_DIM, t_len), BF16),
                   jax.ShapeDtypeStruct((bn, IDX_HEADS, t_len), F32),
                   jax.ShapeDtypeStruct((bn, N_KV_HEADS, tpb, V_AUG, tm), BF16),
                   jax.ShapeDtypeStruct((bn, N_KV_HEADS, t_len, HEAD_DIM), BF16),
                   jax.ShapeDtypeStruct((n, IDX_DIM), BF16)],
        compiler_params=_params(1),
        name="attn_pre",
    )(x2d, mod, g1.reshape(1, d), wt, posr, invft, col(gq), col(gk), col(lng), col(lnb))
    return outs


def _dsa_kernel(qT_ref, qiT_ref, wiT_ref, k4_ref, ki_ref, vT_ref, o_ref,
                st_ref, hi_ref, m_ref, al_ref, acc_ref, bias_ref, lg_ref, left_ref, ltri_ref, *, topk, rb, ka):
    tq = o_ref.shape[1]
    kc = tq
    i = pl.program_id(1)
    n_c = i + 1
    q0 = i * tq

    tcol = q0 + lax.broadcasted_iota(I32, (kc, tq), 1)
    srow0 = lax.broadcasted_iota(I32, (kc, tq), 0)
    wi = wiT_ref[0]
    qi_all = jnp.concatenate([qiT_ref[0, hh] for hh in range(IDX_HEADS)], axis=1)

    def score_chunk(c):
        r0 = pl.multiple_of(c * kc, kc)
        ki = ki_ref[0, pl.ds(r0, kc), :]
        s_all = jnp.dot(ki, qi_all, preferred_element_type=F32)
        acc = jnp.zeros((kc, tq), F32)
        for hh in range(IDX_HEADS):
            acc = acc + jnp.maximum(s_all[:, tq * hh:tq * (hh + 1)], 0.0) * wi[hh:hh + 1, :]
        causal = (srow0 + r0) <= tcol
        sc = jnp.where(causal, acc, -jnp.inf)
        st_ref[pl.ds(r0, kc), :] = sc
        top = lax.bitcast_convert_type(sc, I32) & np.int32(-65536)
        hi_ref[pl.ds(r0, kc), :] = lax.bitcast_convert_type(top, F32).astype(BF16)

    def score_body(j, carry):
        score_chunk(2 * j)
        score_chunk(2 * j + 1)
        return carry

    lax.fori_loop(0, (n_c + 1) // 2, score_body, 0)

    n_big = (n_c * kc) // rb
    n_small = (n_c * kc - n_big * rb) // kc
    n_par = 4
    pack = 2 * SUBLANES

    def all_sublanes(tot):
        for sh in (4, 2, 1):
            tot = tot + pltpu.roll(tot, sh, 0)
        return tot

    def two_level(block_fn, init):
        acc = lax.fori_loop(0, n_big, lambda r, a: block_fn(pl.multiple_of(r * rb, rb), rb, a), init)
        base = n_big * rb
        return lax.fori_loop(0, n_small, lambda r, a: block_fn(pl.multiple_of(base + r * kc, kc), kc, a), acc)

    def count(pred):
        def block(r0, rows, acc):
            blk = st_ref[pl.ds(r0, rows), :].reshape(rows // SUBLANES, SUBLANES, tq)
            hit = pred(blk).astype(I32).reshape(rows // (SUBLANES * n_par), n_par, SUBLANES, tq)
            return acc + jnp.sum(hit, axis=0)
        acc = two_level(block, jnp.zeros((n_par, SUBLANES, tq), I32))
        return all_sublanes(jnp.sum(acc, axis=0))

    def count_hi(cand16):
        def block(r0, rows, acc):
            blk = hi_ref[pl.ds(r0, rows), :].reshape(rows // pack, pack, tq)
            hit = jnp.where(blk >= cand16[None], jnp.ones((), BF16), jnp.zeros((), BF16))
            parts = [hit[g] for g in range(rows // pack)]
            while len(parts) > 1:
                parts = [parts[g] + parts[g + 1] for g in range(0, len(parts), 2)]
            return acc + parts[0].astype(F32)
        acc = two_level(block, jnp.zeros((pack, tq), F32))
        return all_sublanes((acc[0:SUBLANES] + acc[SUBLANES:pack]).astype(I32))

    def key_to_float(u):
        ks = u ^ INT_MIN
        bits = jnp.where(ks < 0, ks ^ np.int32(0x7FFFFFFF), ks)
        return ks, bits

    def hi_body(it, carry):
        tau_u, cnt_tau = carry
        cand_u = tau_u | lax.shift_left(np.int32(1), np.int32(31) - it)
        ks, bits = key_to_float(cand_u)
        cand_top = lax.bitcast_convert_type(bits & np.int32(-65536), F32)
        cand16 = jnp.concatenate([cand_top, cand_top], axis=0).astype(BF16)
        cnt = count_hi(cand16)
        take = (cnt >= topk) | (ks < KEY_HI_NEG_INF)
        return jnp.where(take, cand_u, tau_u), jnp.where(take, cnt, cnt_tau)

    def lo_body(it, carry):
        tau_u, cnt_tau = carry
        cand_u = tau_u | lax.shift_left(np.int32(1), np.int32(15) - it)
        ks, bits = key_to_float(cand_u)
        cand_f = lax.bitcast_convert_type(bits, F32)
        cnt = count(lambda blk: blk >= cand_f[None])
        take = (cnt >= topk) | (ks < KEY_NEG_INF)
        return jnp.where(take, cand_u, tau_u), jnp.where(take, cnt, cnt_tau)

    def lo_stage(state, first, n):
        tau_u, cnt_tau, _ = state
        tau_u, cnt_tau = lax.fori_loop(first, first + n, lo_body, (tau_u, cnt_tau))
        tau8 = lax.bitcast_convert_type(key_to_float(tau_u)[1], F32)
        return tau_u, cnt_tau, count(lambda blk: blk > tau8[None])

    def pending(state):
        tau_u, cnt_tau, cnt_gt = state
        open_lane = ((cnt_tau != topk) & (cnt_gt >= topk)) | ((tau_u ^ INT_MIN) < KEY_NEG_INF)
        return jnp.max(open_lane.astype(I32)) > 0

    zero8 = jnp.zeros((SUBLANES, tq), I32)
    tau_u, cnt_tau = lax.fori_loop(0, 16, hi_body, (zero8, zero8))
    state = lo_stage((tau_u, cnt_tau, zero8), 0, 10)
    state = lax.cond(pending(state), lambda s: lo_stage(s, 10, 2), lambda s: s, state)
    state = lax.cond(pending(state), lambda s: lo_stage(s, 12, 4), lambda s: s, state)
    tau_u, _, cnt_gt = state
    tau8 = lax.bitcast_convert_type(key_to_float(tau_u)[1], F32)
    left_ref[...] = jnp.where(tau8 == -jnp.inf, 0.0, (topk - cnt_gt).astype(F32))
    tau1 = tau8[0:1]
    ri = lax.broadcasted_iota(I32, (kc, kc), 0)
    ci = lax.broadcasted_iota(I32, (kc, kc), 1)
    ltri_ref[...] = jnp.where(ci <= ri, 1.0, 0.0).astype(BF16)

    m_ref[...] = jnp.full(m_ref.shape, -1e30, F32)
    acc_ref[...] = jnp.zeros(acc_ref.shape, F32)

    sub = ka // kc
    n_a = (n_c + sub - 1) // sub

    def set_bias(c):
        for u in range(sub):
            r0 = pl.multiple_of(c * ka + u * kc, kc)
            sc = st_ref[pl.ds(r0, kc), :]
            tie = sc == tau1
            seen = jnp.dot(ltri_ref[...], jnp.where(tie, 1.0, 0.0).astype(BF16), preferred_element_type=F32)
            left = left_ref[0:1, :]
            mask = (sc > tau1) | (tie & (seen <= left))
            bias_ref[u * kc:(u + 1) * kc, :] = jnp.where(mask, 0.0, -jnp.inf)
            left_ref[...] = jnp.broadcast_to(left - seen[kc - 1:kc, :], left_ref.shape)

    def logits_stage(c, hh):
        r0 = pl.multiple_of(c * ka, ka)
        kg = k4_ref[0, hh // GROUP, pl.ds(r0, ka), :]
        lg = jnp.dot(kg, qT_ref[0, hh], preferred_element_type=F32) + bias_ref[...]
        lg_ref[hh] = lg
        mx = jnp.max(lg.reshape(ka // SUBLANES, SUBLANES, tq), axis=0)
        m_old = m_ref[hh:hh + 1, :]
        m_new = jnp.maximum(m_old, jnp.max(mx, axis=0, keepdims=True))
        m_ref[hh:hh + 1, :] = m_new
        al_ref[hh:hh + 1, :] = jnp.exp2(m_old - m_new)

    def value_stage(c, hh):
        vt = jnp.concatenate([vT_ref[0, hh // GROUP, c * sub + u] for u in range(sub)], axis=1)
        p = jnp.exp2(lg_ref[hh] - m_ref[hh:hh + 1, :]).astype(BF16)
        acc_ref[hh] = acc_ref[hh] * al_ref[hh:hh + 1, :] + jnp.dot(vt, p, preferred_element_type=F32)

    set_bias(0)
    for hh in range(N_HEADS):
        logits_stage(0, hh)

    def attn_body(c, carry):
        set_bias(c)
        for hh in range(N_HEADS):
            value_stage(c - 1, hh)
            logits_stage(c, hh)
        return carry

    lax.fori_loop(1, n_a, attn_body, 0)
    for hh in range(N_HEADS):
        value_stage(n_a - 1, hh)

    for hp in range(N_HEADS // 2):
        parts = []
        for hh in (2 * hp, 2 * hp + 1):
            a = acc_ref[hh]
            parts.append(a[0:HEAD_DIM] / a[HEAD_DIM:HEAD_DIM + 1])
        pair = jnp.concatenate(parts, axis=0)
        o_ref[0, :, LANES * hp:LANES * (hp + 1)] = pair.T.astype(BF16)


def _dsa_attention(qT, qiT, wiT, k4, ki, vT, bn, t_len, topk):
    tq = topk
    rb = 1024
    ka = 256
    kern = functools.partial(_dsa_kernel, topk=topk, rb=rb, ka=ka)
    return pl.pallas_call(
        kern,
        grid=(bn, t_len // tq),
        in_specs=[pl.BlockSpec((1, N_HEADS, HEAD_DIM, tq), lambda b, i: (b, 0, 0, i)),
                  pl.BlockSpec((1, IDX_HEADS, IDX_DIM, tq), lambda b, i: (b, 0, 0, i)),
                  pl.BlockSpec((1, IDX_HEADS, tq), lambda b, i: (b, 0, i)),
                  pl.BlockSpec((1, N_KV_HEADS, t_len, HEAD_DIM), lambda b, i: (b, 0, 0, 0),
                               pipeline_mode=pl.Buffered(1)),
                  pl.BlockSpec((1, t_len, IDX_DIM), lambda b, i: (b, 0, 0), pipeline_mode=pl.Buffered(1)),
                  pl.BlockSpec((1, N_KV_HEADS, t_len // tq, V_AUG, tq), lambda b, i: (b, 0, 0, 0, 0),
                               pipeline_mode=pl.Buffered(1))],
        out_specs=pl.BlockSpec((1, tq, Q_COLS), lambda b, i: (b, i, 0)),
        out_shape=jax.ShapeDtypeStruct((bn, t_len, Q_COLS), BF16),
        scratch_shapes=[pltpu.VMEM((t_len, tq), F32),
                        pltpu.VMEM((t_len, tq), BF16),
                        pltpu.VMEM((N_HEADS, tq), F32),
                        pltpu.VMEM((N_HEADS, tq), F32),
                        pltpu.VMEM((N_HEADS, V_AUG, tq), F32),
                        pltpu.VMEM((ka, tq), F32),
                        pltpu.VMEM((N_HEADS, ka, tq), F32),
                        pltpu.VMEM((SUBLANES, tq), F32),
                        pltpu.VMEM((tq, tq), BF16)],
        compiler_params=_params(2),
        name="dsa_attention",
    )(qT, qiT, wiT, k4, ki.reshape(bn, t_len, IDX_DIM), vT)


def _conv_pre_kernel(x_ref, mod_ref, g1_ref, w_ref, cw_ref, o_ref, zbuf_ref, *, tpb):
    tm, d = x_ref.shape
    i = pl.program_id(0)

    @pl.when(i % tpb == 0)
    def _():
        zbuf_ref[0:SUBLANES, :] = jnp.zeros((SUBLANES, d), F32)

    x = x_ref[...]
    sh = mod_ref[0, 0:1, :]
    sc = mod_ref[0, 1:2, :]
    ms = jnp.mean(x * x, axis=-1, keepdims=True)
    h = (x * lax.rsqrt(ms + NORM_EPS) * g1_ref[...]) * (1.0 + sc) + sh
    proj = jnp.dot(h.astype(BF16), w_ref[...], preferred_element_type=F32)
    b_gate = proj[:, 0:d]
    z = proj[:, d:2 * d] * proj[:, 2 * d:3 * d]
    zbuf_ref[SUBLANES:SUBLANES + tm, :] = z
    z1 = zbuf_ref[SUBLANES - 1:SUBLANES - 1 + tm, :]
    z2 = zbuf_ref[SUBLANES - 2:SUBLANES - 2 + tm, :]
    zc = cw_ref[0:1, :] * z2 + cw_ref[1:2, :] * z1 + cw_ref[2:3, :] * z
    o_ref[...] = (b_gate * zc).astype(BF16)
    zbuf_ref[0:SUBLANES, :] = z[tm - SUBLANES:tm, :]


def _conv_pre(x2d, mod, g1, w_in, conv_w, t_len, tm):
    n, d = x2d.shape
    tpb = t_len // tm
    row = lambda i: (i, 0)
    return pl.pallas_call(
        functools.partial(_conv_pre_kernel, tpb=tpb),
        grid=(n // tm,),
        in_specs=[pl.BlockSpec((tm, d), row),
                  pl.BlockSpec((1, 6, d), lambda i: (i // tpb, 0, 0)),
                  _const_spec((1, d)),
                  _const_spec((d, 3 * d)),
                  _const_spec((CONV_WIDTH, d))],
        out_specs=pl.BlockSpec((tm, d), row),
        out_shape=jax.ShapeDtypeStruct((n, d), BF16),
        scratch_shapes=[pltpu.VMEM((tm + SUBLANES, d), F32)],
        compiler_params=_params(1),
        name="conv_pre",
    )(x2d, mod, g1.reshape(1, d), w_in.astype(BF16), conv_w.astype(F32))


def _post_kernel(x_ref, mix_ref, mod_ref, g2_ref, wo_ref, wg_ref, wu_ref, wd_ref, o_ref, *, th):
    gate1 = mod_ref[0, 2:3, :]
    sh2 = mod_ref[0, 3:4, :]
    sc2 = mod_ref[0, 4:5, :]
    gate2 = mod_ref[0, 5:6, :]
    y = jnp.dot(mix_ref[...], wo_ref[...], preferred_element_type=F32)
    x1 = x_ref[...] + gate1 * y
    ms = jnp.mean(x1 * x1, axis=-1, keepdims=True)
    h = ((x1 * lax.rsqrt(ms + NORM_EPS) * g2_ref[...]) * (1.0 + sc2) + sh2).astype(BF16)
    hidden = wg_ref.shape[1]
    acc = jnp.zeros(x1.shape, F32)
    for j in range(hidden // th):
        gt = jnp.dot(h, wg_ref[:, th * j:th * (j + 1)], preferred_element_type=F32)
        up = jnp.dot(h, wu_ref[:, th * j:th * (j + 1)], preferred_element_type=F32)
        a = (gt * jax.nn.sigmoid(gt)) * up
        acc = acc + jnp.dot(a.astype(BF16), wd_ref[th * j:th * (j + 1), :], preferred_element_type=F32)
    o_ref[...] = x1 + gate2 * acc


def _post(x2d, mix, mod, g2, w_out, w_gate, w_up, w_down, t_len, tm):
    n, d = x2d.shape
    hidden = w_gate.shape[1]
    tpb = t_len // tm
    row = lambda i: (i, 0)
    return pl.pallas_call(
        functools.partial(_post_kernel, th=256),
        grid=(n // tm,),
        in_specs=[pl.BlockSpec((tm, d), row),
                  pl.BlockSpec((tm, d), row),
                  pl.BlockSpec((1, 6, d), lambda i: (i // tpb, 0, 0)),
                  _const_spec((1, d)),
                  _const_spec((d, d)),
                  _const_spec((d, hidden)),
                  _const_spec((d, hidden)),
                  _const_spec((hidden, d))],
        out_specs=pl.BlockSpec((tm, d), row),
        out_shape=jax.ShapeDtypeStruct((n, d), F32),
        compiler_params=_params(1),
        name="mixer_out_ffn",
    )(x2d, mix, mod, g2.reshape(1, d), w_out.astype(BF16), w_gate.astype(BF16), w_up.astype(BF16),
      w_down.astype(BF16))


def kernel(x, c, positions, ada_w, ada_b, norm1_g, norm2_g, attn_w_in, attn_q_norm_g, attn_k_norm_g,
           idx_k_ln_g, idx_k_ln_b, attn_w_out, conv_w_in, conv_w, conv_w_out, ffn_w_gate, ffn_w_up,
           ffn_w_down):
    bn, t_len, d = x.shape
    depth = ada_w.shape[0]
    topk = min(TOPK_MAX, t_len // 4)
    assert topk == TOPK_MAX and t_len % topk == 0 and (t_len & (t_len - 1)) == 0
    n = bn * t_len
    tm_pre = 256
    tm_post = 512
    mod = _modulation(c, ada_w, ada_b)
    x2d = x.reshape(n, d)
    for i in range(depth):
        j = i // 2
        if i % 2 == 0:
            qT, qiT, wiT, vT, k2, kiP = _attn_pre(
                x2d, mod[i], norm1_g[i], attn_w_in[j], attn_q_norm_g[j], attn_k_norm_g[j],
                idx_k_ln_g[j], idx_k_ln_b[j], positions, bn, t_len, tm_pre)
            mix = _dsa_attention(qT, qiT, wiT, k2, kiP, vT, bn, t_len, topk).reshape(n, d)
            w_mix_out = attn_w_out[j]
        else:
            mix = _conv_pre(x2d, mod[i], norm1_g[i], conv_w_in[j], conv_w[j], t_len, tm_post)
            w_mix_out = conv_w_out[j]
        x2d = _post(x2d, mix, mod[i], norm2_g[i], w_mix_out, ffn_w_gate[i], ffn_w_up[i], ffn_w_down[i],
                    t_len, tm_post)
    return x2d.reshape(bn, t_len, d)
```

```python
import functools
import math

import numpy as np
import jax
import jax.numpy as jnp
from jax import lax
from jax.experimental import pallas as pl
from jax.experimental.pallas import tpu as pltpu

F32 = jnp.float32
BF16 = jnp.bfloat16
I32 = jnp.int32

N_HEADS = 16
HEAD_DIM = 64
N_KV_HEADS = 4
GROUP = N_HEADS // N_KV_HEADS
IDX_HEADS = 8
IDX_DIM = 64
TOPK_MAX = 256
ROPE_THETA = 10000.0
CONV_WIDTH = 3
NORM_EPS = 1e-6
LOG2E = math.log2(math.e)

LANES = 128
SUBLANES = 8
VMEM_LIMIT = 56 * 1024 * 1024

Q_COLS = N_HEADS * HEAD_DIM
KV_COLS = N_KV_HEADS * HEAD_DIM
QI_COLS = IDX_HEADS * IDX_DIM
V_AUG = HEAD_DIM + 16
SCORE_GROUP = 4

INT_MIN = np.int32(-2**31)
KEY_NEG_INF = np.int32(np.uint32(0x807FFFFF).astype(np.int64) - 2**32)
KEY_HI_NEG_INF = np.int32(np.uint32(0x807F0000).astype(np.int64) - 2**32)


def _const_spec(shape):
    nd = len(shape)
    return pl.BlockSpec(shape, lambda *_: (0,) * nd, pipeline_mode=pl.Buffered(1))


def _params(n_axes):
    return pltpu.CompilerParams(dimension_semantics=("arbitrary",) * n_axes,
                                vmem_limit_bytes=VMEM_LIMIT)


def _mod_kernel(c_ref, w_ref, b_ref, o_ref):
    c = c_ref[...]
    ca = c * jax.nn.sigmoid(c)
    o_ref[0] = jnp.dot(ca.astype(BF16), w_ref[0].astype(BF16), preferred_element_type=F32) + b_ref[0]


def _modulation(c, ada_w, ada_b):
    depth, d, six_d = ada_w.shape
    bn = c.shape[0]
    rows = 16
    cp = jnp.zeros((rows, d), F32).at[:bn].set(c)
    tn = 1536
    out = pl.pallas_call(
        _mod_kernel,
        grid=(depth, six_d // tn),
        in_specs=[pl.BlockSpec((rows, d), lambda i, j: (0, 0)),
                  pl.BlockSpec((1, d, tn), lambda i, j: (i, 0, j)),
                  pl.BlockSpec((1, 1, tn), lambda i, j: (i, 0, j))],
        out_specs=pl.BlockSpec((1, rows, tn), lambda i, j: (i, 0, j)),
        out_shape=jax.ShapeDtypeStruct((depth, rows, six_d), F32),
        compiler_params=_params(2),
        name="adaln_mod",
    )(cp, ada_w, ada_b.reshape(depth, 1, six_d))
    return out[:, :bn].reshape(depth, bn, 6, d)


def _attn_pre_kernel(x_ref, mod_ref, g1_ref, wt_ref, posr_ref, invft_ref, gq_ref, gk_ref, lng_ref, lnb_ref,
                     qT_ref, qiT_ref, wiT_ref, vT_ref, k4_ref, ki_ref):
    tm = x_ref.shape[0]
    x = x_ref[...]
    sh = mod_ref[0, 0:1, :]
    sc = mod_ref[0, 1:2, :]
    ms = jnp.mean(x * x, axis=-1, keepdims=True)
    h = (x * lax.rsqrt(ms + NORM_EPS) * g1_ref[...]) * (1.0 + sc) + sh
    hb = h.astype(BF16)

    pT = lax.dot_general(wt_ref[...], hb, (((1,), (1,)), ((), ())), preferred_element_type=F32)
    angT = invft_ref[...] * posr_ref[0].astype(F32)
    cT = jnp.cos(angT)
    sT = jnp.sin(angT)
    half = HEAD_DIM // 2

    def rope_t(y):
        x1 = y[:half]
        x2 = y[half:]
        return jnp.concatenate([x1 * cT - x2 * sT, x2 * cT + x1 * sT], axis=0)

    gq = gq_ref[...]
    for hh in range(N_HEADS):
        xq = pT[HEAD_DIM * hh:HEAD_DIM * (hh + 1)]
        inv = lax.rsqrt(jnp.mean(xq * xq, axis=0, keepdims=True) + NORM_EPS)
        y = xq * inv * gq
        qT_ref[0, hh] = (rope_t(y) * (HEAD_DIM ** -0.5 * LOG2E)).astype(BF16)
    o0 = Q_COLS
    gk = gk_ref[...]
    for j in range(N_KV_HEADS // 2):
        pair = []
        for g in (2 * j, 2 * j + 1):
            xk = pT[o0 + HEAD_DIM * g:o0 + HEAD_DIM * (g + 1)]
            inv = lax.rsqrt(jnp.mean(xk * xk, axis=0, keepdims=True) + NORM_EPS)
            pair.append(rope_t(xk * inv * gk))
        kr = jnp.concatenate(pair, axis=0).T
        k4_ref[0, 2 * j] = kr[:, 0:HEAD_DIM].astype(BF16)
        k4_ref[0, 2 * j + 1] = kr[:, HEAD_DIM:LANES].astype(BF16)
    o0 += KV_COLS
    ones = jnp.ones((V_AUG - HEAD_DIM, tm), BF16)
    for g in range(N_KV_HEADS):
        vT_ref[0, g, 0, 0:HEAD_DIM, :] = pT[o0 + HEAD_DIM * g:o0 + HEAD_DIM * (g + 1)].astype(BF16)
        vT_ref[0, g, 0, HEAD_DIM:V_AUG, :] = ones
    o0 += KV_COLS
    for hh in range(IDX_HEADS):
        qiT_ref[0, hh] = rope_t(pT[o0 + IDX_DIM * hh:o0 + IDX_DIM * (hh + 1)]).astype(BF16)
    o0 += QI_COLS
    t = pT[o0:o0 + IDX_DIM]
    mu = jnp.mean(t, axis=0, keepdims=True)
    dlt = t - mu
    var = jnp.mean(dlt * dlt, axis=0, keepdims=True)
    kin = rope_t(dlt * lax.rsqrt(var + NORM_EPS) * lng_ref[...] + lnb_ref[...])
    kin = jnp.concatenate([kin, jnp.zeros((LANES - IDX_DIM, tm), F32)], axis=0).T
    ki_ref[...] = kin[:, 0:IDX_DIM].astype(BF16)
    o0 += IDX_DIM
    wiT_ref[0] = pT[o0:o0 + IDX_HEADS] * (IDX_HEADS ** -0.5 * IDX_DIM ** -0.5)


def _attn_pre(x2d, mod, g1, w_in, gq, gk, lng, lnb, positions, bn, t_len, tm):
    n, d = x2d.shape
    tpb = t_len // tm
    half = HEAD_DIM // 2
    ma = w_in.shape[1]
    ma_pad = -(-ma // 16) * 16
    wt = jnp.concatenate([w_in, jnp.zeros((d, ma_pad - ma), F32)], axis=1).T.astype(BF16)
    inv_freq = ROPE_THETA ** (-(jnp.arange(half, dtype=F32) * 2.0 / HEAD_DIM))
    invft = jnp.broadcast_to(inv_freq[:, None], (half, tm))
    col = lambda v: jnp.broadcast_to(v.astype(F32)[:, None], (v.shape[0], tm))
    posr = positions.reshape(bn, 1, t_len)

    row = lambda i: (i, 0)
    outs = pl.pallas_call(
        _attn_pre_kernel,
        grid=(n // tm,),
        in_specs=[pl.BlockSpec((tm, d), row),
                  pl.BlockSpec((1, 6, d), lambda i: (i // tpb, 0, 0)),
                  _const_spec((1, d)),
                  _const_spec((ma_pad, d)),
                  pl.BlockSpec((1, 1, tm), lambda i: (i // tpb, 0, i % tpb)),
                  _const_spec((half, tm)),
                  _const_spec((HEAD_DIM, tm)),
                  _const_spec((HEAD_DIM, tm)),
                  _const_spec((IDX_DIM, tm)),
                  _const_spec((IDX_DIM, tm))],
        out_specs=[pl.BlockSpec((1, N_HEADS, HEAD_DIM, tm), lambda i: (i // tpb, 0, 0, i % tpb)),
                   pl.BlockSpec((1, IDX_HEADS, IDX_DIM, tm), lambda i: (i // tpb, 0, 0, i % tpb)),
                   pl.BlockSpec((1, IDX_HEADS, tm), lambda i: (i // tpb, 0, i % tpb)),
                   pl.BlockSpec((1, N_KV_HEADS, 1, V_AUG, tm), lambda i: (i // tpb, 0, i % tpb, 0, 0)),
                   pl.BlockSpec((1, N_KV_HEADS, tm, HEAD_DIM), lambda i: (i // tpb, 0, i % tpb, 0)),
                   pl.BlockSpec((tm, IDX_DIM), row)],
        out_shape=[jax.ShapeDtypeStruct((bn, N_HEADS, HEAD_DIM, t_len), BF16),
                   jax.ShapeDtypeStruct((bn, IDX_HEADS, IDX_DIM, t_len), BF16),
                   jax.ShapeDtypeStruct((bn, IDX_HEADS, t_len), F32),
                   jax.ShapeDtypeStruct((bn, N_KV_HEADS, tpb, V_AUG, tm), BF16),
                   jax.ShapeDtypeStruct((bn, N_KV_HEADS, t_len, HEAD_DIM), BF16),
                   jax.ShapeDtypeStruct((n, IDX_DIM), BF16)],
        compiler_params=_params(1),
        name="attn_pre",
    )(x2d, mod, g1.reshape(1, d), wt, posr, invft, col(gq), col(gk), col(lng), col(lnb))
    return outs


def _dsa_kernel(qT_ref, qiT_ref, wiT_ref, k4_ref, ki_ref, vT_ref, o_ref,
                st_ref, hi_ref, m_ref, al_ref, acc_ref, bias_ref, lg_ref, left_ref, ltri_ref, *, topk, rb):
    tq = o_ref.shape[1]
    kc = tq
    i = pl.program_id(1)
    n_c = i + 1
    q0 = i * tq

    tcol = q0 + lax.broadcasted_iota(I32, (kc, tq), 1)
    srow0 = lax.broadcasted_iota(I32, (kc, tq), 0)
    wi = wiT_ref[0]
    qi_all = jnp.concatenate([qiT_ref[0, hh] for hh in range(IDX_HEADS)], axis=1)

    def score_chunk(c):
        r0 = pl.multiple_of(c * kc, kc)
        ki = ki_ref[0, pl.ds(r0, kc), :]
        s_all = jnp.dot(ki, qi_all, preferred_element_type=F32)
        acc = jnp.zeros((kc, tq), F32)
        for hh in range(IDX_HEADS):
            acc = acc + jnp.maximum(s_all[:, tq * hh:tq * (hh + 1)], 0.0) * wi[hh:hh + 1, :]
        causal = (srow0 + r0) <= tcol
        sc = jnp.where(causal, acc, -jnp.inf)
        st_ref[pl.ds(r0, kc), :] = sc
        top = lax.bitcast_convert_type(sc, I32) & np.int32(-65536)
        hi_ref[pl.ds(r0, kc), :] = lax.bitcast_convert_type(top, F32).astype(BF16)

    def score_group(j, carry):
        for u in range(SCORE_GROUP):
            score_chunk(SCORE_GROUP * j + u)
        return carry

    def score_single(c, carry):
        score_chunk(c)
        return carry

    n_grp = n_c // SCORE_GROUP
    lax.fori_loop(0, n_grp, score_group, 0)
    lax.fori_loop(n_grp * SCORE_GROUP, n_c, score_single, 0)

    n_big = (n_c * kc) // rb
    n_small = (n_c * kc - n_big * rb) // kc
    n_par = 4
    pack = 2 * SUBLANES

    def all_sublanes(tot):
        for sh in (4, 2, 1):
            tot = tot + pltpu.roll(tot, sh, 0)
        return tot

    def two_level(block_fn, init):
        acc = lax.fori_loop(0, n_big, lambda r, a: block_fn(pl.multiple_of(r * rb, rb), rb, a), init)
        base = n_big * rb
        return lax.fori_loop(0, n_small, lambda r, a: block_fn(pl.multiple_of(base + r * kc, kc), kc, a), acc)

    def count(pred):
        def block(r0, rows, acc):
            blk = st_ref[pl.ds(r0, rows), :].reshape(rows // SUBLANES, SUBLANES, tq)
            hit = pred(blk).astype(I32).reshape(rows // (SUBLANES * n_par), n_par, SUBLANES, tq)
            return acc + jnp.sum(hit, axis=0)
        acc = two_level(block, jnp.zeros((n_par, SUBLANES, tq), I32))
        return all_sublanes(jnp.sum(acc, axis=0))

    def count_hi(cand16):
        def block(r0, rows, acc):
            blk = hi_ref[pl.ds(r0, rows), :].reshape(rows // pack, pack, tq)
            hit = jnp.where(blk >= cand16[None], jnp.ones((), BF16), jnp.zeros((), BF16))
            parts = [hit[g] for g in range(rows // pack)]
            while len(parts) > 1:
                parts = [parts[g] + parts[g + 1] for g in range(0, len(parts), 2)]
            return acc + parts[0].astype(F32)
        acc = two_level(block, jnp.zeros((pack, tq), F32))
        return all_sublanes((acc[0:SUBLANES] + acc[SUBLANES:pack]).astype(I32))

    def key_to_float(u):
        ks = u ^ INT_MIN
        bits = jnp.where(ks < 0, ks ^ np.int32(0x7FFFFFFF), ks)
        return ks, bits

    def hi_body(it, carry):
        tau_u, cnt_tau = carry
        cand_u = tau_u | lax.shift_left(np.int32(1), np.int32(31) - it)
        ks, bits = key_to_float(cand_u)
        cand_top = lax.bitcast_convert_type(bits & np.int32(-65536), F32)
        cand16 = jnp.concatenate([cand_top, cand_top], axis=0).astype(BF16)
        cnt = count_hi(cand16)
        take = (cnt >= topk) | (ks < KEY_HI_NEG_INF)
        return jnp.where(take, cand_u, tau_u), jnp.where(take, cnt, cnt_tau)

    def lo_body(it, carry):
        tau_u, cnt_tau = carry
        cand_u = tau_u | lax.shift_left(np.int32(1), np.int32(15) - it)
        ks, bits = key_to_float(cand_u)
        cand_f = lax.bitcast_convert_type(bits, F32)
        cnt = count(lambda blk: blk >= cand_f[None])
        take = (cnt >= topk) | (ks < KEY_NEG_INF)
        return jnp.where(take, cand_u, tau_u), jnp.where(take, cnt, cnt_tau)

    def lo_stage(state, first, n):
        tau_u, cnt_tau, _ = state
        tau_u, cnt_tau = lax.fori_loop(first, first + n, lo_body, (tau_u, cnt_tau))
        tau8 = lax.bitcast_convert_type(key_to_float(tau_u)[1], F32)
        return tau_u, cnt_tau, count(lambda blk: blk > tau8[None])

    def pending(state):
        tau_u, cnt_tau, cnt_gt = state
        open_lane = ((cnt_tau != topk) & (cnt_gt >= topk)) | ((tau_u ^ INT_MIN) < KEY_NEG_INF)
        return jnp.max(open_lane.astype(I32)) > 0

    zero8 = jnp.zeros((SUBLANES, tq), I32)
    tau_u, cnt_tau = lax.fori_loop(0, 16, hi_body, (zero8, zero8))
    state = lo_stage((tau_u, cnt_tau, zero8), 0, 10)
    state = lax.cond(pending(state), lambda s: lo_stage(s, 10, 2), lambda s: s, state)
    state = lax.cond(pending(state), lambda s: lo_stage(s, 12, 4), lambda s: s, state)
    tau_u, _, cnt_gt = state
    tau8 = lax.bitcast_convert_type(key_to_float(tau_u)[1], F32)
    left_ref[...] = jnp.where(tau8 == -jnp.inf, 0.0, (topk - cnt_gt).astype(F32))
    tau1 = tau8[0:1]
    ri = lax.broadcasted_iota(I32, (kc, kc), 0)
    ci = lax.broadcasted_iota(I32, (kc, kc), 1)
    ltri_ref[...] = jnp.where(ci <= ri, 1.0, 0.0).astype(BF16)

    m_ref[...] = jnp.full(m_ref.shape, -1e30, F32)
    acc_ref[...] = jnp.zeros(acc_ref.shape, F32)

    def set_bias(c):
        r0 = pl.multiple_of(c * kc, kc)
        sc = st_ref[pl.ds(r0, kc), :]
        tie = sc == tau1
        seen = jnp.dot(ltri_ref[...], jnp.where(tie, 1.0, 0.0).astype(BF16), preferred_element_type=F32)
        left = left_ref[0:1, :]
        mask = (sc > tau1) | (tie & (seen <= left))
        bias_ref[...] = jnp.where(mask, 0.0, -jnp.inf)
        left_ref[...] = jnp.broadcast_to(left - seen[kc - 1:kc, :], left_ref.shape)

    def logits_stage(c, hh):
        r0 = pl.multiple_of(c * kc, kc)
        kg = k4_ref[0, hh // GROUP, pl.ds(r0, kc), :]
        lg = jnp.dot(kg, qT_ref[0, hh], preferred_element_type=F32) + bias_ref[...]
        lg_ref[hh] = lg
        mx = jnp.max(lg.reshape(kc // SUBLANES, SUBLANES, tq), axis=0)
        m_old = m_ref[hh:hh + 1, :]
        m_new = jnp.maximum(m_old, jnp.max(mx, axis=0, keepdims=True))
        m_ref[hh:hh + 1, :] = m_new
        al_ref[hh:hh + 1, :] = jnp.exp2(m_old - m_new)

    def value_stage(c, hh):
        vt = vT_ref[0, hh // GROUP, c]
        p = jnp.exp2(lg_ref[hh] - m_ref[hh:hh + 1, :]).astype(BF16)
        acc_ref[hh] = acc_ref[hh] * al_ref[hh:hh + 1, :] + jnp.dot(vt, p, preferred_element_type=F32)

    set_bias(0)
    for hh in range(N_HEADS):
        logits_stage(0, hh)

    def attn_body(c, carry):
        set_bias(c)
        for hh in range(N_HEADS):
            value_stage(c - 1, hh)
            logits_stage(c, hh)
        return carry

    lax.fori_loop(1, n_c, attn_body, 0)
    for hh in range(N_HEADS):
        value_stage(n_c - 1, hh)

    for hp in range(N_HEADS // 2):
        parts = []
        for hh in (2 * hp, 2 * hp + 1):
            a = acc_ref[hh]
            parts.append(a[0:HEAD_DIM] / a[HEAD_DIM:HEAD_DIM + 1])
        pair = jnp.concatenate(parts, axis=0)
        o_ref[0, :, LANES * hp:LANES * (hp + 1)] = pair.T.astype(BF16)


def _dsa_attention(qT, qiT, wiT, k4, ki, vT, bn, t_len, topk):
    tq = topk
    rb = 1024
    kern = functools.partial(_dsa_kernel, topk=topk, rb=rb)
    return pl.pallas_call(
        kern,
        grid=(bn, t_len // tq),
        in_specs=[pl.BlockSpec((1, N_HEADS, HEAD_DIM, tq), lambda b, i: (b, 0, 0, i)),
                  pl.BlockSpec((1, IDX_HEADS, IDX_DIM, tq), lambda b, i: (b, 0, 0, i)),
                  pl.BlockSpec((1, IDX_HEADS, tq), lambda b, i: (b, 0, i)),
                  pl.BlockSpec((1, N_KV_HEADS, t_len, HEAD_DIM), lambda b, i: (b, 0, 0, 0),
                               pipeline_mode=pl.Buffered(1)),
                  pl.BlockSpec((1, t_len, IDX_DIM), lambda b, i: (b, 0, 0), pipeline_mode=pl.Buffered(1)),
                  pl.BlockSpec((1, N_KV_HEADS, t_len // tq, V_AUG, tq), lambda b, i: (b, 0, 0, 0, 0),
                               pipeline_mode=pl.Buffered(1))],
        out_specs=pl.BlockSpec((1, tq, Q_COLS), lambda b, i: (b, i, 0)),
        out_shape=jax.ShapeDtypeStruct((bn, t_len, Q_COLS), BF16),
        scratch_shapes=[pltpu.VMEM((t_len, tq), F32),
                        pltpu.VMEM((t_len, tq), BF16),
                        pltpu.VMEM((N_HEADS, tq), F32),
                        pltpu.VMEM((N_HEADS, tq), F32),
                        pltpu.VMEM((N_HEADS, V_AUG, tq), F32),
                        pltpu.VMEM((tq, tq), F32),
                        pltpu.VMEM((N_HEADS, tq, tq), F32),
                        pltpu.VMEM((SUBLANES, tq), F32),
                        pltpu.VMEM((tq, tq), BF16)],
        compiler_params=_params(2),
        name="dsa_attention",
    )(qT, qiT, wiT, k4, ki.reshape(bn, t_len, IDX_DIM), vT)


def _conv_pre_kernel(x_ref, mod_ref, g1_ref, w_ref, cw_ref, o_ref, zbuf_ref, *, tpb):
    tm, d = x_ref.shape
    i = pl.program_id(0)

    @pl.when(i % tpb == 0)
    def _():
        zbuf_ref[0:SUBLANES, :] = jnp.zeros((SUBLANES, d), F32)

    x = x_ref[...]
    sh = mod_ref[0, 0:1, :]
    sc = mod_ref[0, 1:2, :]
    ms = jnp.mean(x * x, axis=-1, keepdims=True)
    h = (x * lax.rsqrt(ms + NORM_EPS) * g1_ref[...]) * (1.0 + sc) + sh
    proj = jnp.dot(h.astype(BF16), w_ref[...], preferred_element_type=F32)
    b_gate = proj[:, 0:d]
    z = proj[:, d:2 * d] * proj[:, 2 * d:3 * d]
    zbuf_ref[SUBLANES:SUBLANES + tm, :] = z
    z1 = zbuf_ref[SUBLANES - 1:SUBLANES - 1 + tm, :]
    z2 = zbuf_ref[SUBLANES - 2:SUBLANES - 2 + tm, :]
    zc = cw_ref[0:1, :] * z2 + cw_ref[1:2, :] * z1 + cw_ref[2:3, :] * z
    o_ref[...] = (b_gate * zc).astype(BF16)
    zbuf_ref[0:SUBLANES, :] = z[tm - SUBLANES:tm, :]


def _conv_pre(x2d, mod, g1, w_in, conv_w, t_len, tm):
    n, d = x2d.shape
    tpb = t_len // tm
    row = lambda i: (i, 0)
    return pl.pallas_call(
        functools.partial(_conv_pre_kernel, tpb=tpb),
        grid=(n // tm,),
        in_specs=[pl.BlockSpec((tm, d), row),
                  pl.BlockSpec((1, 6, d), lambda i: (i // tpb, 0, 0)),
                  _const_spec((1, d)),
                  _const_spec((d, 3 * d)),
                  _const_spec((CONV_WIDTH, d))],
        out_specs=pl.BlockSpec((tm, d), row),
        out_shape=jax.ShapeDtypeStruct((n, d), BF16),
        scratch_shapes=[pltpu.VMEM((tm + SUBLANES, d), F32)],
        compiler_params=_params(1),
        name="conv_pre",
    )(x2d, mod, g1.reshape(1, d), w_in.astype(BF16), conv_w.astype(F32))


def _post_kernel(x_ref, mix_ref, mod_ref, g2_ref, wo_ref, wg_ref, wu_ref, wd_ref, o_ref, *, th):
    gate1 = mod_ref[0, 2:3, :]
    sh2 = mod_ref[0, 3:4, :]
    sc2 = mod_ref[0, 4:5, :]
    gate2 = mod_ref[0, 5:6, :]
    y = jnp.dot(mix_ref[...], wo_ref[...], preferred_element_type=F32)
    x1 = x_ref[...] + gate1 * y
    ms = jnp.mean(x1 * x1, axis=-1, keepdims=True)
    h = ((x1 * lax.rsqrt(ms + NORM_EPS) * g2_ref[...]) * (1.0 + sc2) + sh2).astype(BF16)
    hidden = wg_ref.shape[1]
    acc = jnp.zeros(x1.shape, F32)
    for j in range(hidden // th):
        gt = jnp.dot(h, wg_ref[:, th * j:th * (j + 1)], preferred_element_type=F32)
        up = jnp.dot(h, wu_ref[:, th * j:th * (j + 1)], preferred_element_type=F32)
        a = (gt * jax.nn.sigmoid(gt)) * up
        acc = acc + jnp.dot(a.astype(BF16), wd_ref[th * j:th * (j + 1), :], preferred_element_type=F32)
    o_ref[...] = x1 + gate2 * acc


def _post(x2d, mix, mod, g2, w_out, w_gate, w_up, w_down, t_len, tm):
    n, d = x2d.shape
    hidden = w_gate.shape[1]
    tpb = t_len // tm
    row = lambda i: (i, 0)
    return pl.pallas_call(
        functools.partial(_post_kernel, th=256),
        grid=(n // tm,),
        in_specs=[pl.BlockSpec((tm, d), row),
                  pl.BlockSpec((tm, d), row),
                  pl.BlockSpec((1, 6, d), lambda i: (i // tpb, 0, 0)),
                  _const_spec((1, d)),
                  _const_spec((d, d)),
                  _const_spec((d, hidden)),
                  _const_spec((d, hidden)),
                  _const_spec((hidden, d))],
        out_specs=pl.BlockSpec((tm, d), row),
        out_shape=jax.ShapeDtypeStruct((n, d), F32),
        compiler_params=_params(1),
        name="mixer_out_ffn",
    )(x2d, mix, mod, g2.reshape(1, d), w_out.astype(BF16), w_gate.astype(BF16), w_up.astype(BF16),
      w_down.astype(BF16))


def kernel(x, c, positions, ada_w, ada_b, norm1_g, norm2_g, attn_w_in, attn_q_norm_g, attn_k_norm_g,
           idx_k_ln_g, idx_k_ln_b, attn_w_out, conv_w_in, conv_w, conv_w_out, ffn_w_gate, ffn_w_up,
           ffn_w_down):
    bn, t_len, d = x.shape
    depth = ada_w.shape[0]
    topk = min(TOPK_MAX, t_len // 4)
    assert topk == TOPK_MAX and t_len % topk == 0 and (t_len & (t_len - 1)) == 0
    n = bn * t_len
    tm_pre = 256
    tm_post = 512
    mod = _modulation(c, ada_w, ada_b)
    x2d = x.reshape(n, d)
    for i in range(depth):
        j = i // 2
        if i % 2 == 0:
            qT, qiT, wiT, vT, k2, kiP = _attn_pre(
                x2d, mod[i], norm1_g[i], attn_w_in[j], attn_q_norm_g[j], attn_k_norm_g[j],
                idx_k_ln_g[j], idx_k_ln_b[j], positions, bn, t_len, tm_pre)
            mix = _dsa_attention(qT, qiT, wiT, k2, kiP, vT, bn, t_len, topk).reshape(n, d)
            w_mix_out = attn_w_out[j]
        else:
            mix = _conv_pre(x2d, mod[i], norm1_g[i], conv_w_in[j], conv_w[j], t_len, tm_post)
            w_mix_out = conv_w_out[j]
        x2d = _post(x2d, mix, mod[i], norm2_g[i], w_mix_out, ffn_w_gate[i], ffn_w_up[i], ffn_w_down[i],
                    t_len, tm_post)
    return x2d.reshape(bn, t_len, d)
```

```python
import functools
import math

import numpy as np
import jax
import jax.numpy as jnp
from jax import lax
from jax.experimental import pallas as pl
from jax.experimental.pallas import tpu as pltpu

F32 = jnp.float32
BF16 = jnp.bfloat16
I32 = jnp.int32

N_HEADS = 16
HEAD_DIM = 64
N_KV_HEADS = 4
GROUP = N_HEADS // N_KV_HEADS
IDX_HEADS = 8
IDX_DIM = 64
TOPK_MAX = 256
ROPE_THETA = 10000.0
CONV_WIDTH = 3
NORM_EPS = 1e-6
LOG2E = math.log2(math.e)

LANES = 128
SUBLANES = 8
VMEM_LIMIT = 56 * 1024 * 1024

Q_COLS = N_HEADS * HEAD_DIM
KV_COLS = N_KV_HEADS * HEAD_DIM
QI_COLS = IDX_HEADS * IDX_DIM
V_AUG = HEAD_DIM + 16
SCORE_GROUP = 4
ATTN_GROUP = 4

INT_MIN = np.int32(-2**31)
KEY_NEG_INF = np.int32(np.uint32(0x807FFFFF).astype(np.int64) - 2**32)
KEY_HI_NEG_INF = np.int32(np.uint32(0x807F0000).astype(np.int64) - 2**32)


def _const_spec(shape):
    nd = len(shape)
    return pl.BlockSpec(shape, lambda *_: (0,) * nd, pipeline_mode=pl.Buffered(1))


def _params(n_axes):
    return pltpu.CompilerParams(dimension_semantics=("arbitrary",) * n_axes,
                                vmem_limit_bytes=VMEM_LIMIT)


def _mod_kernel(c_ref, w_ref, b_ref, o_ref):
    c = c_ref[...]
    ca = c * jax.nn.sigmoid(c)
    o_ref[0] = jnp.dot(ca.astype(BF16), w_ref[0].astype(BF16), preferred_element_type=F32) + b_ref[0]


def _modulation(c, ada_w, ada_b):
    depth, d, six_d = ada_w.shape
    bn = c.shape[0]
    rows = 16
    cp = jnp.zeros((rows, d), F32).at[:bn].set(c)
    tn = 1536
    out = pl.pallas_call(
        _mod_kernel,
        grid=(depth, six_d // tn),
        in_specs=[pl.BlockSpec((rows, d), lambda i, j: (0, 0)),
                  pl.BlockSpec((1, d, tn), lambda i, j: (i, 0, j)),
                  pl.BlockSpec((1, 1, tn), lambda i, j: (i, 0, j))],
        out_specs=pl.BlockSpec((1, rows, tn), lambda i, j: (i, 0, j)),
        out_shape=jax.ShapeDtypeStruct((depth, rows, six_d), F32),
        compiler_params=_params(2),
        name="adaln_mod",
    )(cp, ada_w, ada_b.reshape(depth, 1, six_d))
    return out[:, :bn].reshape(depth, bn, 6, d)


def _attn_pre_kernel(x_ref, mod_ref, g1_ref, wt_ref, posr_ref, invft_ref, gq_ref, gk_ref, lng_ref, lnb_ref,
                     qT_ref, qiT_ref, wiT_ref, vT_ref, k4_ref, ki_ref):
    tm = x_ref.shape[0]
    x = x_ref[...]
    sh = mod_ref[0, 0:1, :]
    sc = mod_ref[0, 1:2, :]
    ms = jnp.mean(x * x, axis=-1, keepdims=True)
    h = (x * lax.rsqrt(ms + NORM_EPS) * g1_ref[...]) * (1.0 + sc) + sh
    hb = h.astype(BF16)

    pT = lax.dot_general(wt_ref[...], hb, (((1,), (1,)), ((), ())), preferred_element_type=F32)
    angT = invft_ref[...] * posr_ref[0].astype(F32)
    cT = jnp.cos(angT)
    sT = jnp.sin(angT)
    half = HEAD_DIM // 2

    def rope_t(y):
        x1 = y[:half]
        x2 = y[half:]
        return jnp.concatenate([x1 * cT - x2 * sT, x2 * cT + x1 * sT], axis=0)

    gq = gq_ref[...]
    for hh in range(N_HEADS):
        xq = pT[HEAD_DIM * hh:HEAD_DIM * (hh + 1)]
        inv = lax.rsqrt(jnp.mean(xq * xq, axis=0, keepdims=True) + NORM_EPS)
        y = xq * inv * gq
        qT_ref[0, hh] = (rope_t(y) * (HEAD_DIM ** -0.5 * LOG2E)).astype(BF16)
    o0 = Q_COLS
    gk = gk_ref[...]
    for j in range(N_KV_HEADS // 2):
        pair = []
        for g in (2 * j, 2 * j + 1):
            xk = pT[o0 + HEAD_DIM * g:o0 + HEAD_DIM * (g + 1)]
            inv = lax.rsqrt(jnp.mean(xk * xk, axis=0, keepdims=True) + NORM_EPS)
            pair.append(rope_t(xk * inv * gk))
        kr = jnp.concatenate(pair, axis=0).T
        k4_ref[0, 2 * j] = kr[:, 0:HEAD_DIM].astype(BF16)
        k4_ref[0, 2 * j + 1] = kr[:, HEAD_DIM:LANES].astype(BF16)
    o0 += KV_COLS
    ones = jnp.ones((V_AUG - HEAD_DIM, tm), BF16)
    for g in range(N_KV_HEADS):
        vT_ref[0, g, 0, 0:HEAD_DIM, :] = pT[o0 + HEAD_DIM * g:o0 + HEAD_DIM * (g + 1)].astype(BF16)
        vT_ref[0, g, 0, HEAD_DIM:V_AUG, :] = ones
    o0 += KV_COLS
    for hh in range(IDX_HEADS):
        qiT_ref[0, hh] = rope_t(pT[o0 + IDX_DIM * hh:o0 + IDX_DIM * (hh + 1)]).astype(BF16)
    o0 += QI_COLS
    t = pT[o0:o0 + IDX_DIM]
    mu = jnp.mean(t, axis=0, keepdims=True)
    dlt = t - mu
    var = jnp.mean(dlt * dlt, axis=0, keepdims=True)
    kin = rope_t(dlt * lax.rsqrt(var + NORM_EPS) * lng_ref[...] + lnb_ref[...])
    kin = jnp.concatenate([kin, jnp.zeros((LANES - IDX_DIM, tm), F32)], axis=0).T
    ki_ref[...] = kin[:, 0:IDX_DIM].astype(BF16)
    o0 += IDX_DIM
    wiT_ref[0] = pT[o0:o0 + IDX_HEADS] * (IDX_HEADS ** -0.5 * IDX_DIM ** -0.5)


def _attn_pre(x2d, mod, g1, w_in, gq, gk, lng, lnb, positions, bn, t_len, tm):
    n, d = x2d.shape
    tpb = t_len // tm
    half = HEAD_DIM // 2
    ma = w_in.shape[1]
    ma_pad = -(-ma // 16) * 16
    wt = jnp.concatenate([w_in, jnp.zeros((d, ma_pad - ma), F32)], axis=1).T.astype(BF16)
    inv_freq = ROPE_THETA ** (-(jnp.arange(half, dtype=F32) * 2.0 / HEAD_DIM))
    invft = jnp.broadcast_to(inv_freq[:, None], (half, tm))
    col = lambda v: jnp.broadcast_to(v.astype(F32)[:, None], (v.shape[0], tm))
    posr = positions.reshape(bn, 1, t_len)

    row = lambda i: (i, 0)
    outs = pl.pallas_call(
        _attn_pre_kernel,
        grid=(n // tm,),
        in_specs=[pl.BlockSpec((tm, d), row),
                  pl.BlockSpec((1, 6, d), lambda i: (i // tpb, 0, 0)),
                  _const_spec((1, d)),
                  _const_spec((ma_pad, d)),
                  pl.BlockSpec((1, 1, tm), lambda i: (i // tpb, 0, i % tpb)),
                  _const_spec((half, tm)),
                  _const_spec((HEAD_DIM, tm)),
                  _const_spec((HEAD_DIM, tm)),
                  _const_spec((IDX_DIM, tm)),
                  _const_spec((IDX_DIM, tm))],
        out_specs=[pl.BlockSpec((1, N_HEADS, HEAD_DIM, tm), lambda i: (i // tpb, 0, 0, i % tpb)),
                   pl.BlockSpec((1, IDX_HEADS, IDX_DIM, tm), lambda i: (i // tpb, 0, 0, i % tpb)),
                   pl.BlockSpec((1, IDX_HEADS, tm), lambda i: (i // tpb, 0, i % tpb)),
                   pl.BlockSpec((1, N_KV_HEADS, 1, V_AUG, tm), lambda i: (i // tpb, 0, i % tpb, 0, 0)),
                   pl.BlockSpec((1, N_KV_HEADS, tm, HEAD_DIM), lambda i: (i // tpb, 0, i % tpb, 0)),
                   pl.BlockSpec((tm, IDX_DIM), row)],
        out_shape=[jax.ShapeDtypeStruct((bn, N_HEADS, HEAD_DIM, t_len), BF16),
                   jax.ShapeDtypeStruct((bn, IDX_HEADS, IDX_DIM, t_len), BF16),
                   jax.ShapeDtypeStruct((bn, IDX_HEADS, t_len), F32),
                   jax.ShapeDtypeStruct((bn, N_KV_HEADS, tpb, V_AUG, tm), BF16),
                   jax.ShapeDtypeStruct((bn, N_KV_HEADS, t_len, HEAD_DIM), BF16),
                   jax.ShapeDtypeStruct((n, IDX_DIM), BF16)],
        compiler_params=_params(1),
        name="attn_pre",
    )(x2d, mod, g1.reshape(1, d), wt, posr, invft, col(gq), col(gk), col(lng), col(lnb))
    return outs


def _dsa_kernel(qT_ref, qiT_ref, wiT_ref, k4_ref, ki_ref, vT_ref, o_ref,
                st_ref, hi_ref, m_ref, al_ref, acc_ref, bias_ref, lg_ref, left_ref, ltri_ref, *, topk, rb):
    tq = o_ref.shape[1]
    kc = tq
    i = pl.program_id(1)
    n_c = i + 1
    q0 = i * tq

    tcol = q0 + lax.broadcasted_iota(I32, (kc, tq), 1)
    srow0 = lax.broadcasted_iota(I32, (kc, tq), 0)
    wi = wiT_ref[0]
    qi_all = jnp.concatenate([qiT_ref[0, hh] for hh in range(IDX_HEADS)], axis=1)

    def score_chunk(c):
        r0 = pl.multiple_of(c * kc, kc)
        ki = ki_ref[0, pl.ds(r0, kc), :]
        s_all = jnp.dot(ki, qi_all, preferred_element_type=F32)
        acc = jnp.zeros((kc, tq), F32)
        for hh in range(IDX_HEADS):
            acc = acc + jnp.maximum(s_all[:, tq * hh:tq * (hh + 1)], 0.0) * wi[hh:hh + 1, :]
        causal = (srow0 + r0) <= tcol
        sc = jnp.where(causal, acc, -jnp.inf)
        st_ref[pl.ds(r0, kc), :] = sc
        top = lax.bitcast_convert_type(sc, I32) & np.int32(-65536)
        hi_ref[pl.ds(r0, kc), :] = lax.bitcast_convert_type(top, F32).astype(BF16)

    def score_group(j, carry):
        for u in range(SCORE_GROUP):
            score_chunk(SCORE_GROUP * j + u)
        return carry

    def score_single(c, carry):
        score_chunk(c)
        return carry

    n_grp = n_c // SCORE_GROUP
    lax.fori_loop(0, n_grp, score_group, 0)
    lax.fori_loop(n_grp * SCORE_GROUP, n_c, score_single, 0)

    n_big = (n_c * kc) // rb
    n_small = (n_c * kc - n_big * rb) // kc
    n_par = 4
    pack = 2 * SUBLANES

    def all_sublanes(tot):
        for sh in (4, 2, 1):
            tot = tot + pltpu.roll(tot, sh, 0)
        return tot

    def two_level(block_fn, init):
        acc = lax.fori_loop(0, n_big, lambda r, a: block_fn(pl.multiple_of(r * rb, rb), rb, a), init)
        base = n_big * rb
        return lax.fori_loop(0, n_small, lambda r, a: block_fn(pl.multiple_of(base + r * kc, kc), kc, a), acc)

    def count(pred):
        def block(r0, rows, acc):
            blk = st_ref[pl.ds(r0, rows), :].reshape(rows // SUBLANES, SUBLANES, tq)
            hit = pred(blk).astype(I32).reshape(rows // (SUBLANES * n_par), n_par, SUBLANES, tq)
            return acc + jnp.sum(hit, axis=0)
        acc = two_level(block, jnp.zeros((n_par, SUBLANES, tq), I32))
        return all_sublanes(jnp.sum(acc, axis=0))

    def count_hi(cand16):
        def block(r0, rows, acc):
            blk = hi_ref[pl.ds(r0, rows), :].reshape(rows // pack, pack, tq)
            hit = jnp.where(blk >= cand16[None], jnp.ones((), BF16), jnp.zeros((), BF16))
            parts = [hit[g] for g in range(rows // pack)]
            while len(parts) > 1:
                parts = [parts[g] + parts[g + 1] for g in range(0, len(parts), 2)]
            return acc + parts[0].astype(F32)
        acc = two_level(block, jnp.zeros((pack, tq), F32))
        return all_sublanes((acc[0:SUBLANES] + acc[SUBLANES:pack]).astype(I32))

    def key_to_float(u):
        ks = u ^ INT_MIN
        bits = jnp.where(ks < 0, ks ^ np.int32(0x7FFFFFFF), ks)
        return ks, bits

    def hi_body(it, carry):
        tau_u, cnt_tau = carry
        cand_u = tau_u | lax.shift_left(np.int32(1), np.int32(31) - it)
        ks, bits = key_to_float(cand_u)
        cand_top = lax.bitcast_convert_type(bits & np.int32(-65536), F32)
        cand16 = jnp.concatenate([cand_top, cand_top], axis=0).astype(BF16)
        cnt = count_hi(cand16)
        take = (cnt >= topk) | (ks < KEY_HI_NEG_INF)
        return jnp.where(take, cand_u, tau_u), jnp.where(take, cnt, cnt_tau)

    def lo_body(it, carry):
        tau_u, cnt_tau = carry
        cand_u = tau_u | lax.shift_left(np.int32(1), np.int32(15) - it)
        ks, bits = key_to_float(cand_u)
        cand_f = lax.bitcast_convert_type(bits, F32)
        cnt = count(lambda blk: blk >= cand_f[None])
        take = (cnt >= topk) | (ks < KEY_NEG_INF)
        return jnp.where(take, cand_u, tau_u), jnp.where(take, cnt, cnt_tau)

    def lo_stage(state, first, n):
        tau_u, cnt_tau, _ = state
        tau_u, cnt_tau = lax.fori_loop(first, first + n, lo_body, (tau_u, cnt_tau))
        tau8 = lax.bitcast_convert_type(key_to_float(tau_u)[1], F32)
        return tau_u, cnt_tau, count(lambda blk: blk > tau8[None])

    def pending(state):
        tau_u, cnt_tau, cnt_gt = state
        open_lane = ((cnt_tau != topk) & (cnt_gt >= topk)) | ((tau_u ^ INT_MIN) < KEY_NEG_INF)
        return jnp.max(open_lane.astype(I32)) > 0

    zero8 = jnp.zeros((SUBLANES, tq), I32)
    tau_u, cnt_tau = lax.fori_loop(0, 16, hi_body, (zero8, zero8))
    state = lo_stage((tau_u, cnt_tau, zero8), 0, 10)
    state = lax.cond(pending(state), lambda s: lo_stage(s, 10, 2), lambda s: s, state)
    state = lax.cond(pending(state), lambda s: lo_stage(s, 12, 4), lambda s: s, state)
    tau_u, _, cnt_gt = state
    tau8 = lax.bitcast_convert_type(key_to_float(tau_u)[1], F32)
    left_ref[...] = jnp.where(tau8 == -jnp.inf, 0.0, (topk - cnt_gt).astype(F32))
    tau1 = tau8[0:1]
    ri = lax.broadcasted_iota(I32, (kc, kc), 0)
    ci = lax.broadcasted_iota(I32, (kc, kc), 1)
    ltri_ref[...] = jnp.where(ci <= ri, 1.0, 0.0).astype(BF16)

    m_ref[...] = jnp.full(m_ref.shape, -1e30, F32)
    acc_ref[...] = jnp.zeros(acc_ref.shape, F32)

    def set_bias(c):
        r0 = pl.multiple_of(c * kc, kc)
        sc = st_ref[pl.ds(r0, kc), :]
        tie = sc == tau1
        seen = jnp.dot(ltri_ref[...], jnp.where(tie, 1.0, 0.0).astype(BF16), preferred_element_type=F32)
        left = left_ref[0:1, :]
        mask = (sc > tau1) | (tie & (seen <= left))
        bias_ref[...] = jnp.where(mask, 0.0, -jnp.inf)
        left_ref[...] = jnp.broadcast_to(left - seen[kc - 1:kc, :], left_ref.shape)

    def logits_stage(c, hh):
        r0 = pl.multiple_of(c * kc, kc)
        kg = k4_ref[0, hh // GROUP, pl.ds(r0, kc), :]
        lg = jnp.dot(kg, qT_ref[0, hh], preferred_element_type=F32) + bias_ref[...]
        lg_ref[hh] = lg
        mx = jnp.max(lg.reshape(kc // SUBLANES, SUBLANES, tq), axis=0)
        m_old = m_ref[hh:hh + 1, :]
        m_new = jnp.maximum(m_old, jnp.max(mx, axis=0, keepdims=True))
        m_ref[hh:hh + 1, :] = m_new
        al_ref[hh:hh + 1, :] = jnp.exp2(m_old - m_new)

    def value_stage(c, hh):
        vt = vT_ref[0, hh // GROUP, c]
        p = jnp.exp2(lg_ref[hh] - m_ref[hh:hh + 1, :]).astype(BF16)
        acc_ref[hh] = acc_ref[hh] * al_ref[hh:hh + 1, :] + jnp.dot(vt, p, preferred_element_type=F32)

    set_bias(0)
    for hh in range(N_HEADS):
        logits_stage(0, hh)

    def attn_step(c):
        set_bias(c)
        for hh in range(N_HEADS):
            value_stage(c - 1, hh)
            logits_stage(c, hh)

    def attn_group(j, carry):
        for u in range(ATTN_GROUP):
            attn_step(ATTN_GROUP * j + 1 + u)
        return carry

    def attn_single(c, carry):
        attn_step(c)
        return carry

    n_grp = (n_c - 1) // ATTN_GROUP
    lax.fori_loop(0, n_grp, attn_group, 0)
    lax.fori_loop(ATTN_GROUP * n_grp + 1, n_c, attn_single, 0)
    for hh in range(N_HEADS):
        value_stage(n_c - 1, hh)

    for hp in range(N_HEADS // 2):
        parts = []
        for hh in (2 * hp, 2 * hp + 1):
            a = acc_ref[hh]
            parts.append(a[0:HEAD_DIM] / a[HEAD_DIM:HEAD_DIM + 1])
        pair = jnp.concatenate(parts, axis=0)
        o_ref[0, :, LANES * hp:LANES * (hp + 1)] = pair.T.astype(BF16)


def _dsa_attention(qT, qiT, wiT, k4, ki, vT, bn, t_len, topk):
    tq = topk
    rb = 1024
    kern = functools.partial(_dsa_kernel, topk=topk, rb=rb)
    return pl.pallas_call(
        kern,
        grid=(bn, t_len // tq),
        in_specs=[pl.BlockSpec((1, N_HEADS, HEAD_DIM, tq), lambda b, i: (b, 0, 0, i)),
                  pl.BlockSpec((1, IDX_HEADS, IDX_DIM, tq), lambda b, i: (b, 0, 0, i)),
                  pl.BlockSpec((1, IDX_HEADS, tq), lambda b, i: (b, 0, i)),
                  pl.BlockSpec((1, N_KV_HEADS, t_len, HEAD_DIM), lambda b, i: (b, 0, 0, 0),
                               pipeline_mode=pl.Buffered(1)),
                  pl.BlockSpec((1, t_len, IDX_DIM), lambda b, i: (b, 0, 0), pipeline_mode=pl.Buffered(1)),
                  pl.BlockSpec((1, N_KV_HEADS, t_len // tq, V_AUG, tq), lambda b, i: (b, 0, 0, 0, 0),
                               pipeline_mode=pl.Buffered(1))],
        out_specs=pl.BlockSpec((1, tq, Q_COLS), lambda b, i: (b, i, 0)),
        out_shape=jax.ShapeDtypeStruct((bn, t_len, Q_COLS), BF16),
        scratch_shapes=[pltpu.VMEM((t_len, tq), F32),
                        pltpu.VMEM((t_len, tq), BF16),
                        pltpu.VMEM((N_HEADS, tq), F32),
                        pltpu.VMEM((N_HEADS, tq), F32),
                        pltpu.VMEM((N_HEADS, V_AUG, tq), F32),
                        pltpu.VMEM((tq, tq), F32),
                        pltpu.VMEM((N_HEADS, tq, tq), F32),
                        pltpu.VMEM((SUBLANES, tq), F32),
                        pltpu.VMEM((tq, tq), BF16)],
        compiler_params=_params(2),
        name="dsa_attention",
    )(qT, qiT, wiT, k4, ki.reshape(bn, t_len, IDX_DIM), vT)


def _conv_pre_kernel(x_ref, mod_ref, g1_ref, w_ref, cw_ref, o_ref, zbuf_ref, *, tpb):
    tm, d = x_ref.shape
    i = pl.program_id(0)

    @pl.when(i % tpb == 0)
    def _():
        zbuf_ref[0:SUBLANES, :] = jnp.zeros((SUBLANES, d), F32)

    x = x_ref[...]
    sh = mod_ref[0, 0:1, :]
    sc = mod_ref[0, 1:2, :]
    ms = jnp.mean(x * x, axis=-1, keepdims=True)
    h = (x * lax.rsqrt(ms + NORM_EPS) * g1_ref[...]) * (1.0 + sc) + sh
    proj = jnp.dot(h.astype(BF16), w_ref[...], preferred_element_type=F32)
    b_gate = proj[:, 0:d]
    z = proj[:, d:2 * d] * proj[:, 2 * d:3 * d]
    zbuf_ref[SUBLANES:SUBLANES + tm, :] = z
    z1 = zbuf_ref[SUBLANES - 1:SUBLANES - 1 + tm, :]
    z2 = zbuf_ref[SUBLANES - 2:SUBLANES - 2 + tm, :]
    zc = cw_ref[0:1, :] * z2 + cw_ref[1:2, :] * z1 + cw_ref[2:3, :] * z
    o_ref[...] = (b_gate * zc).astype(BF16)
    zbuf_ref[0:SUBLANES, :] = z[tm - SUBLANES:tm, :]


def _conv_pre(x2d, mod, g1, w_in, conv_w, t_len, tm):
    n, d = x2d.shape
    tpb = t_len // tm
    row = lambda i: (i, 0)
    return pl.pallas_call(
        functools.partial(_conv_pre_kernel, tpb=tpb),
        grid=(n // tm,),
        in_specs=[pl.BlockSpec((tm, d), row),
                  pl.BlockSpec((1, 6, d), lambda i: (i // tpb, 0, 0)),
                  _const_spec((1, d)),
                  _const_spec((d, 3 * d)),
                  _const_spec((CONV_WIDTH, d))],
        out_specs=pl.BlockSpec((tm, d), row),
        out_shape=jax.ShapeDtypeStruct((n, d), BF16),
        scratch_shapes=[pltpu.VMEM((tm + SUBLANES, d), F32)],
        compiler_params=_params(1),
        name="conv_pre",
    )(x2d, mod, g1.reshape(1, d), w_in.astype(BF16), conv_w.astype(F32))


def _post_kernel(x_ref, mix_ref, mod_ref, g2_ref, wo_ref, wg_ref, wu_ref, wd_ref, o_ref, *, th):
    gate1 = mod_ref[0, 2:3, :]
    sh2 = mod_ref[0, 3:4, :]
    sc2 = mod_ref[0, 4:5, :]
    gate2 = mod_ref[0, 5:6, :]
    y = jnp.dot(mix_ref[...], wo_ref[...], preferred_element_type=F32)
    x1 = x_ref[...] + gate1 * y
    ms = jnp.mean(x1 * x1, axis=-1, keepdims=True)
    h = ((x1 * lax.rsqrt(ms + NORM_EPS) * g2_ref[...]) * (1.0 + sc2) + sh2).astype(BF16)
    hidden = wg_ref.shape[1]
    acc = jnp.zeros(x1.shape, F32)
    for j in range(hidden // th):
        gt = jnp.dot(h, wg_ref[:, th * j:th * (j + 1)], preferred_element_type=F32)
        up = jnp.dot(h, wu_ref[:, th * j:th * (j + 1)], preferred_element_type=F32)
        a = (gt * jax.nn.sigmoid(gt)) * up
        acc = acc + jnp.dot(a.astype(BF16), wd_ref[th * j:th * (j + 1), :], preferred_element_type=F32)
    o_ref[...] = x1 + gate2 * acc


def _post(x2d, mix, mod, g2, w_out, w_gate, w_up, w_down, t_len, tm):
    n, d = x2d.shape
    hidden = w_gate.shape[1]
    tpb = t_len // tm
    row = lambda i: (i, 0)
    return pl.pallas_call(
        functools.partial(_post_kernel, th=256),
        grid=(n // tm,),
        in_specs=[pl.BlockSpec((tm, d), row),
                  pl.BlockSpec((tm, d), row),
                  pl.BlockSpec((1, 6, d), lambda i: (i // tpb, 0, 0)),
                  _const_spec((1, d)),
                  _const_spec((d, d)),
                  _const_spec((d, hidden)),
                  _const_spec((d, hidden)),
                  _const_spec((hidden, d))],
        out_specs=pl.BlockSpec((tm, d), row),
        out_shape=jax.ShapeDtypeStruct((n, d), F32),
        compiler_params=_params(1),
        name="mixer_out_ffn",
    )(x2d, mix, mod, g2.reshape(1, d), w_out.astype(BF16), w_gate.astype(BF16), w_up.astype(BF16),
      w_down.astype(BF16))


def kernel(x, c, positions, ada_w, ada_b, norm1_g, norm2_g, attn_w_in, attn_q_norm_g, attn_k_norm_g,
           idx_k_ln_g, idx_k_ln_b, attn_w_out, conv_w_in, conv_w, conv_w_out, ffn_w_gate, ffn_w_up,
           ffn_w_down):
    bn, t_len, d = x.shape
    depth = ada_w.shape[0]
    topk = min(TOPK_MAX, t_len // 4)
    assert topk == TOPK_MAX and t_len % topk == 0 and (t_len & (t_len - 1)) == 0
    n = bn * t_len
    tm_pre = 256
    tm_post = 512
    mod = _modulation(c, ada_w, ada_b)
    x2d = x.reshape(n, d)
    for i in range(depth):
        j = i // 2
        if i % 2 == 0:
            qT, qiT, wiT, vT, k2, kiP = _attn_pre(
                x2d, mod[i], norm1_g[i], attn_w_in[j], attn_q_norm_g[j], attn_k_norm_g[j],
                idx_k_ln_g[j], idx_k_ln_b[j], positions, bn, t_len, tm_pre)
            mix = _dsa_attention(qT, qiT, wiT, k2, kiP, vT, bn, t_len, topk).reshape(n, d)
            w_mix_out = attn_w_out[j]
        else:
            mix = _conv_pre(x2d, mod[i], norm1_g[i], conv_w_in[j], conv_w[j], t_len, tm_post)
            w_mix_out = conv_w_out[j]
        x2d = _post(x2d, mix, mod[i], norm2_g[i], w_mix_out, ffn_w_gate[i], ffn_w_up[i], ffn_w_down[i],
                    t_len, tm_post)
    return x2d.reshape(bn, t_len, d)
```

```python
import functools
import math

import numpy as np
import jax
import jax.numpy as jnp
from jax import lax
from jax.experimental import pallas as pl
from jax.experimental.pallas import tpu as pltpu

F32 = jnp.float32
BF16 = jnp.bfloat16
I32 = jnp.int32

N_HEADS = 16
HEAD_DIM = 64
N_KV_HEADS = 4
GROUP = N_HEADS // N_KV_HEADS
IDX_HEADS = 8
IDX_DIM = 64
TOPK_MAX = 256
ROPE_THETA = 10000.0
CONV_WIDTH = 3
NORM_EPS = 1e-6
LOG2E = math.log2(math.e)

LANES = 128
SUBLANES = 8
VMEM_LIMIT = 56 * 1024 * 1024

Q_COLS = N_HEADS * HEAD_DIM
KV_COLS = N_KV_HEADS * HEAD_DIM
QI_COLS = IDX_HEADS * IDX_DIM
V_AUG = HEAD_DIM + 16
SCORE_GROUP = 4
ATTN_GROUP = 4

INT_MIN = np.int32(-2**31)
KEY_NEG_INF = np.int32(np.uint32(0x807FFFFF).astype(np.int64) - 2**32)
KEY_HI_NEG_INF = np.int32(np.uint32(0x807F0000).astype(np.int64) - 2**32)


def _const_spec(shape):
    nd = len(shape)
    return pl.BlockSpec(shape, lambda *_: (0,) * nd, pipeline_mode=pl.Buffered(1))


def _params(n_axes):
    return pltpu.CompilerParams(dimension_semantics=("arbitrary",) * n_axes,
                                vmem_limit_bytes=VMEM_LIMIT)


def _mod_kernel(c_ref, w_ref, b_ref, o_ref):
    c = c_ref[...]
    ca = c * jax.nn.sigmoid(c)
    o_ref[0] = jnp.dot(ca.astype(BF16), w_ref[0].astype(BF16), preferred_element_type=F32) + b_ref[0]


def _modulation(c, ada_w, ada_b):
    depth, d, six_d = ada_w.shape
    bn = c.shape[0]
    rows = 16
    cp = jnp.zeros((rows, d), F32).at[:bn].set(c)
    tn = 1536
    out = pl.pallas_call(
        _mod_kernel,
        grid=(depth, six_d // tn),
        in_specs=[pl.BlockSpec((rows, d), lambda i, j: (0, 0)),
                  pl.BlockSpec((1, d, tn), lambda i, j: (i, 0, j)),
                  pl.BlockSpec((1, 1, tn), lambda i, j: (i, 0, j))],
        out_specs=pl.BlockSpec((1, rows, tn), lambda i, j: (i, 0, j)),
        out_shape=jax.ShapeDtypeStruct((depth, rows, six_d), F32),
        compiler_params=_params(2),
        name="adaln_mod",
    )(cp, ada_w, ada_b.reshape(depth, 1, six_d))
    return out[:, :bn].reshape(depth, bn, 6, d)


def _attn_pre_kernel(x_ref, mod_ref, g1_ref, wt_ref, posr_ref, invft_ref, gq_ref, gk_ref, lng_ref, lnb_ref,
                     qT_ref, qiT_ref, wiT_ref, vT_ref, k4_ref, ki_ref):
    tm = x_ref.shape[0]
    x = x_ref[...]
    sh = mod_ref[0, 0:1, :]
    sc = mod_ref[0, 1:2, :]
    ms = jnp.mean(x * x, axis=-1, keepdims=True)
    h = (x * lax.rsqrt(ms + NORM_EPS) * g1_ref[...]) * (1.0 + sc) + sh
    hb = h.astype(BF16)

    pT = lax.dot_general(wt_ref[...], hb, (((1,), (1,)), ((), ())), preferred_element_type=F32)
    angT = invft_ref[...] * posr_ref[0].astype(F32)
    cT = jnp.cos(angT)
    sT = jnp.sin(angT)
    half = HEAD_DIM // 2

    def rope_t(y):
        x1 = y[:half]
        x2 = y[half:]
        return jnp.concatenate([x1 * cT - x2 * sT, x2 * cT + x1 * sT], axis=0)

    gq = gq_ref[...]
    for hh in range(N_HEADS):
        xq = pT[HEAD_DIM * hh:HEAD_DIM * (hh + 1)]
        inv = lax.rsqrt(jnp.mean(xq * xq, axis=0, keepdims=True) + NORM_EPS)
        y = xq * inv * gq
        qT_ref[0, hh] = (rope_t(y) * (HEAD_DIM ** -0.5 * LOG2E)).astype(BF16)
    o0 = Q_COLS
    gk = gk_ref[...]
    for j in range(N_KV_HEADS // 2):
        pair = []
        for g in (2 * j, 2 * j + 1):
            xk = pT[o0 + HEAD_DIM * g:o0 + HEAD_DIM * (g + 1)]
            inv = lax.rsqrt(jnp.mean(xk * xk, axis=0, keepdims=True) + NORM_EPS)
            pair.append(rope_t(xk * inv * gk))
        kr = jnp.concatenate(pair, axis=0).T
        k4_ref[0, 2 * j] = kr[:, 0:HEAD_DIM].astype(BF16)
        k4_ref[0, 2 * j + 1] = kr[:, HEAD_DIM:LANES].astype(BF16)
    o0 += KV_COLS
    ones = jnp.ones((V_AUG - HEAD_DIM, tm), BF16)
    for g in range(N_KV_HEADS):
        vT_ref[0, g, 0, 0:HEAD_DIM, :] = pT[o0 + HEAD_DIM * g:o0 + HEAD_DIM * (g + 1)].astype(BF16)
        vT_ref[0, g, 0, HEAD_DIM:V_AUG, :] = ones
    o0 += KV_COLS
    for hh in range(IDX_HEADS):
        qiT_ref[0, hh] = rope_t(pT[o0 + IDX_DIM * hh:o0 + IDX_DIM * (hh + 1)]).astype(BF16)
    o0 += QI_COLS
    t = pT[o0:o0 + IDX_DIM]
    mu = jnp.mean(t, axis=0, keepdims=True)
    dlt = t - mu
    var = jnp.mean(dlt * dlt, axis=0, keepdims=True)
    kin = rope_t(dlt * lax.rsqrt(var + NORM_EPS) * lng_ref[...] + lnb_ref[...])
    kin = jnp.concatenate([kin, jnp.zeros((LANES - IDX_DIM, tm), F32)], axis=0).T
    ki_ref[...] = kin[:, 0:IDX_DIM].astype(BF16)
    o0 += IDX_DIM
    wiT_ref[0] = pT[o0:o0 + IDX_HEADS] * (IDX_HEADS ** -0.5 * IDX_DIM ** -0.5)


def _attn_pre(x2d, mod, g1, w_in, gq, gk, lng, lnb, positions, bn, t_len, tm):
    n, d = x2d.shape
    tpb = t_len // tm
    half = HEAD_DIM // 2
    ma = w_in.shape[1]
    ma_pad = -(-ma // 16) * 16
    wt = jnp.concatenate([w_in, jnp.zeros((d, ma_pad - ma), F32)], axis=1).T.astype(BF16)
    inv_freq = ROPE_THETA ** (-(jnp.arange(half, dtype=F32) * 2.0 / HEAD_DIM))
    invft = jnp.broadcast_to(inv_freq[:, None], (half, tm))
    col = lambda v: jnp.broadcast_to(v.astype(F32)[:, None], (v.shape[0], tm))
    posr = positions.reshape(bn, 1, t_len)

    row = lambda i: (i, 0)
    outs = pl.pallas_call(
        _attn_pre_kernel,
        grid=(n // tm,),
        in_specs=[pl.BlockSpec((tm, d), row),
                  pl.BlockSpec((1, 6, d), lambda i: (i // tpb, 0, 0)),
                  _const_spec((1, d)),
                  _const_spec((ma_pad, d)),
                  pl.BlockSpec((1, 1, tm), lambda i: (i // tpb, 0, i % tpb)),
                  _const_spec((half, tm)),
                  _const_spec((HEAD_DIM, tm)),
                  _const_spec((HEAD_DIM, tm)),
                  _const_spec((IDX_DIM, tm)),
                  _const_spec((IDX_DIM, tm))],
        out_specs=[pl.BlockSpec((1, N_HEADS, HEAD_DIM, tm), lambda i: (i // tpb, 0, 0, i % tpb)),
                   pl.BlockSpec((1, IDX_HEADS, IDX_DIM, tm), lambda i: (i // tpb, 0, 0, i % tpb)),
                   pl.BlockSpec((1, IDX_HEADS, tm), lambda i: (i // tpb, 0, i % tpb)),
                   pl.BlockSpec((1, N_KV_HEADS, 1, V_AUG, tm), lambda i: (i // tpb, 0, i % tpb, 0, 0)),
                   pl.BlockSpec((1, N_KV_HEADS, tm, HEAD_DIM), lambda i: (i // tpb, 0, i % tpb, 0)),
                   pl.BlockSpec((tm, IDX_DIM), row)],
        out_shape=[jax.ShapeDtypeStruct((bn, N_HEADS, HEAD_DIM, t_len), BF16),
                   jax.ShapeDtypeStruct((bn, IDX_HEADS, IDX_DIM, t_len), BF16),
                   jax.ShapeDtypeStruct((bn, IDX_HEADS, t_len), F32),
                   jax.ShapeDtypeStruct((bn, N_KV_HEADS, tpb, V_AUG, tm), BF16),
                   jax.ShapeDtypeStruct((bn, N_KV_HEADS, t_len, HEAD_DIM), BF16),
                   jax.ShapeDtypeStruct((n, IDX_DIM), BF16)],
        compiler_params=_params(1),
        name="attn_pre",
    )(x2d, mod, g1.reshape(1, d), wt, posr, invft, col(gq), col(gk), col(lng), col(lnb))
    return outs


def _dsa_kernel(qT_ref, qiT_ref, wiT_ref, k4_ref, ki_ref, vT_ref, o_ref,
                st_ref, hi_ref, m_ref, al_ref, acc_ref, bias_ref, lg_ref, left_ref, ltri_ref, *, topk, rb):
    tq = o_ref.shape[1]
    kc = tq
    i = pl.program_id(1)
    n_c = i + 1
    q0 = i * tq

    tcol = q0 + lax.broadcasted_iota(I32, (kc, tq), 1)
    srow0 = lax.broadcasted_iota(I32, (kc, tq), 0)
    wi = wiT_ref[0]
    qi_all = jnp.concatenate([qiT_ref[0, hh] for hh in range(IDX_HEADS)], axis=1)

    def score_chunk(c):
        r0 = pl.multiple_of(c * kc, kc)
        ki = ki_ref[0, pl.ds(r0, kc), :]
        s_all = jnp.dot(ki, qi_all, preferred_element_type=F32)
        acc = jnp.zeros((kc, tq), F32)
        for hh in range(IDX_HEADS):
            acc = acc + jnp.maximum(s_all[:, tq * hh:tq * (hh + 1)], 0.0) * wi[hh:hh + 1, :]
        causal = (srow0 + r0) <= tcol
        sc = jnp.where(causal, acc, -jnp.inf)
        st_ref[pl.ds(r0, kc), :] = sc
        top = lax.bitcast_convert_type(sc, I32) & np.int32(-65536)
        hi_ref[pl.ds(r0, kc), :] = lax.bitcast_convert_type(top, F32).astype(BF16)

    def score_group(j, carry):
        for u in range(SCORE_GROUP):
            score_chunk(SCORE_GROUP * j + u)
        return carry

    def score_single(c, carry):
        score_chunk(c)
        return carry

    n_grp = n_c // SCORE_GROUP
    lax.fori_loop(0, n_grp, score_group, 0)
    lax.fori_loop(n_grp * SCORE_GROUP, n_c, score_single, 0)

    n_big = (n_c * kc) // rb
    n_small = (n_c * kc - n_big * rb) // kc
    n_par = 4
    pack = 2 * SUBLANES

    def all_sublanes(tot):
        for sh in (4, 2, 1):
            tot = tot + pltpu.roll(tot, sh, 0)
        return tot

    def two_level(block_fn, init):
        acc = lax.fori_loop(0, n_big, lambda r, a: block_fn(pl.multiple_of(r * rb, rb), rb, a), init)
        base = n_big * rb
        return lax.fori_loop(0, n_small, lambda r, a: block_fn(pl.multiple_of(base + r * kc, kc), kc, a), acc)

    def count(pred):
        def block(r0, rows, acc):
            blk = st_ref[pl.ds(r0, rows), :].reshape(rows // SUBLANES, SUBLANES, tq)
            hit = pred(blk).astype(I32).reshape(rows // (SUBLANES * n_par), n_par, SUBLANES, tq)
            return acc + jnp.sum(hit, axis=0)
        acc = two_level(block, jnp.zeros((n_par, SUBLANES, tq), I32))
        return all_sublanes(jnp.sum(acc, axis=0))

    def count_hi(cand16):
        def block(r0, rows, acc):
            blk = hi_ref[pl.ds(r0, rows), :].reshape(rows // pack, pack, tq)
            hit = jnp.where(blk >= cand16[None], jnp.ones((), BF16), jnp.zeros((), BF16))
            parts = [hit[g] for g in range(rows // pack)]
            while len(parts) > 1:
                parts = [parts[g] + parts[g + 1] for g in range(0, len(parts), 2)]
            return acc + parts[0].astype(F32)
        acc = two_level(block, jnp.zeros((pack, tq), F32))
        return all_sublanes((acc[0:SUBLANES] + acc[SUBLANES:pack]).astype(I32))

    def key_to_float(u):
        ks = u ^ INT_MIN
        bits = jnp.where(ks < 0, ks ^ np.int32(0x7FFFFFFF), ks)
        return ks, bits

    def hi_body(it, carry):
        tau_u, cnt_tau = carry
        cand_u = tau_u | lax.shift_left(np.int32(1), np.int32(31) - it)
        ks, bits = key_to_float(cand_u)
        cand_top = lax.bitcast_convert_type(bits & np.int32(-65536), F32)
        cand16 = jnp.concatenate([cand_top, cand_top], axis=0).astype(BF16)
        cnt = count_hi(cand16)
        take = (cnt >= topk) | (ks < KEY_HI_NEG_INF)
        return jnp.where(take, cand_u, tau_u), jnp.where(take, cnt, cnt_tau)

    def lo_body(it, carry):
        tau_u, cnt_tau = carry
        cand_u = tau_u | lax.shift_left(np.int32(1), np.int32(15) - it)
        ks, bits = key_to_float(cand_u)
        cand_f = lax.bitcast_convert_type(bits, F32)
        cnt = count(lambda blk: blk >= cand_f[None])
        take = (cnt >= topk) | (ks < KEY_NEG_INF)
        return jnp.where(take, cand_u, tau_u), jnp.where(take, cnt, cnt_tau)

    def lo_stage(state, first, n):
        tau_u, cnt_tau, _ = state
        tau_u, cnt_tau = lax.fori_loop(first, first + n, lo_body, (tau_u, cnt_tau))
        tau8 = lax.bitcast_convert_type(key_to_float(tau_u)[1], F32)
        return tau_u, cnt_tau, count(lambda blk: blk > tau8[None])

    def pending(state):
        tau_u, cnt_tau, cnt_gt = state
        open_lane = ((cnt_tau != topk) & (cnt_gt >= topk)) | ((tau_u ^ INT_MIN) < KEY_NEG_INF)
        return jnp.max(open_lane.astype(I32)) > 0

    zero8 = jnp.zeros((SUBLANES, tq), I32)
    tau_u, cnt_tau = lax.fori_loop(0, 16, hi_body, (zero8, zero8))
    state = lo_stage((tau_u, cnt_tau, zero8), 0, 10)
    state = lax.cond(pending(state), lambda s: lo_stage(s, 10, 2), lambda s: s, state)
    state = lax.cond(pending(state), lambda s: lo_stage(s, 12, 4), lambda s: s, state)
    tau_u, _, cnt_gt = state
    tau8 = lax.bitcast_convert_type(key_to_float(tau_u)[1], F32)
    left_ref[...] = jnp.where(tau8 == -jnp.inf, 0.0, (topk - cnt_gt).astype(F32))
    tau1 = tau8[0:1]
    ri = lax.broadcasted_iota(I32, (kc, kc), 0)
    ci = lax.broadcasted_iota(I32, (kc, kc), 1)
    ltri_ref[...] = jnp.where(ci <= ri, 1.0, 0.0).astype(BF16)

    m_ref[...] = jnp.full(m_ref.shape, -(2.0 ** 100), F32)
    acc_ref[...] = jnp.zeros(acc_ref.shape, F32)

    def set_bias(c):
        r0 = pl.multiple_of(c * kc, kc)
        sc = st_ref[pl.ds(r0, kc), :]
        tie = sc == tau1
        seen = jnp.dot(ltri_ref[...], jnp.where(tie, 1.0, 0.0).astype(BF16), preferred_element_type=F32)
        left = left_ref[0:1, :]
        mask = (sc > tau1) | (tie & (seen <= left))
        bias_ref[...] = jnp.where(mask, 0.0, -jnp.inf).astype(BF16)
        left_ref[...] = jnp.broadcast_to(left - seen[kc - 1:kc, :], left_ref.shape)

    def logits_stage(c, hh):
        r0 = pl.multiple_of(c * kc, kc)
        kg = k4_ref[0, hh // GROUP, pl.ds(r0, kc), :]
        lg = jnp.dot(kg, qT_ref[0, hh], preferred_element_type=F32).astype(BF16) + bias_ref[...]
        lg_ref[hh] = lg
        parts = [lg[pack * g:pack * (g + 1)] for g in range(kc // pack)]
        while len(parts) > 1:
            parts = [jnp.maximum(parts[g], parts[g + 1]) for g in range(0, len(parts), 2)]
        mx = parts[0].astype(F32)
        m_old = m_ref[hh:hh + 1, :]
        m_new = jnp.maximum(m_old, jnp.max(mx, axis=0, keepdims=True))
        m_ref[hh:hh + 1, :] = m_new
        al_ref[hh:hh + 1, :] = jnp.exp2(m_old - m_new)

    def value_stage(c, hh):
        vt = vT_ref[0, hh // GROUP, c]
        m_b = jnp.broadcast_to(m_ref[hh:hh + 1, :], (pack, tq)).astype(BF16)
        p = jnp.exp2(lg_ref[hh].reshape(kc // pack, pack, tq) - m_b[None]).reshape(kc, tq)
        acc_ref[hh] = acc_ref[hh] * al_ref[hh:hh + 1, :] + jnp.dot(vt, p, preferred_element_type=F32)

    set_bias(0)
    for hh in range(N_HEADS):
        logits_stage(0, hh)

    def attn_step(c):
        set_bias(c)
        for hh in range(N_HEADS):
            value_stage(c - 1, hh)
            logits_stage(c, hh)

    def attn_group(j, carry):
        for u in range(ATTN_GROUP):
            attn_step(ATTN_GROUP * j + 1 + u)
        return carry

    def attn_single(c, carry):
        attn_step(c)
        return carry

    n_grp = (n_c - 1) // ATTN_GROUP
    lax.fori_loop(0, n_grp, attn_group, 0)
    lax.fori_loop(ATTN_GROUP * n_grp + 1, n_c, attn_single, 0)
    for hh in range(N_HEADS):
        value_stage(n_c - 1, hh)

    for hp in range(N_HEADS // 2):
        parts = []
        for hh in (2 * hp, 2 * hp + 1):
            a = acc_ref[hh]
            parts.append(a[0:HEAD_DIM] / a[HEAD_DIM:HEAD_DIM + 1])
        pair = jnp.concatenate(parts, axis=0)
        o_ref[0, :, LANES * hp:LANES * (hp + 1)] = pair.T.astype(BF16)


def _dsa_attention(qT, qiT, wiT, k4, ki, vT, bn, t_len, topk):
    tq = topk
    rb = 1024
    kern = functools.partial(_dsa_kernel, topk=topk, rb=rb)
    return pl.pallas_call(
        kern,
        grid=(bn, t_len // tq),
        in_specs=[pl.BlockSpec((1, N_HEADS, HEAD_DIM, tq), lambda b, i: (b, 0, 0, i)),
                  pl.BlockSpec((1, IDX_HEADS, IDX_DIM, tq), lambda b, i: (b, 0, 0, i)),
                  pl.BlockSpec((1, IDX_HEADS, tq), lambda b, i: (b, 0, i)),
                  pl.BlockSpec((1, N_KV_HEADS, t_len, HEAD_DIM), lambda b, i: (b, 0, 0, 0),
                               pipeline_mode=pl.Buffered(1)),
                  pl.BlockSpec((1, t_len, IDX_DIM), lambda b, i: (b, 0, 0), pipeline_mode=pl.Buffered(1)),
                  pl.BlockSpec((1, N_KV_HEADS, t_len // tq, V_AUG, tq), lambda b, i: (b, 0, 0, 0, 0),
                               pipeline_mode=pl.Buffered(1))],
        out_specs=pl.BlockSpec((1, tq, Q_COLS), lambda b, i: (b, i, 0)),
        out_shape=jax.ShapeDtypeStruct((bn, t_len, Q_COLS), BF16),
        scratch_shapes=[pltpu.VMEM((t_len, tq), F32),
                        pltpu.VMEM((t_len, tq), BF16),
                        pltpu.VMEM((N_HEADS, tq), F32),
                        pltpu.VMEM((N_HEADS, tq), F32),
                        pltpu.VMEM((N_HEADS, V_AUG, tq), F32),
                        pltpu.VMEM((tq, tq), BF16),
                        pltpu.VMEM((N_HEADS, tq, tq), BF16),
                        pltpu.VMEM((SUBLANES, tq), F32),
                        pltpu.VMEM((tq, tq), BF16)],
        compiler_params=_params(2),
        name="dsa_attention",
    )(qT, qiT, wiT, k4, ki.reshape(bn, t_len, IDX_DIM), vT)


def _conv_pre_kernel(x_ref, mod_ref, g1_ref, w_ref, cw_ref, o_ref, zbuf_ref, *, tpb):
    tm, d = x_ref.shape
    i = pl.program_id(0)

    @pl.when(i % tpb == 0)
    def _():
        zbuf_ref[0:SUBLANES, :] = jnp.zeros((SUBLANES, d), F32)

    x = x_ref[...]
    sh = mod_ref[0, 0:1, :]
    sc = mod_ref[0, 1:2, :]
    ms = jnp.mean(x * x, axis=-1, keepdims=True)
    h = (x * lax.rsqrt(ms + NORM_EPS) * g1_ref[...]) * (1.0 + sc) + sh
    proj = jnp.dot(h.astype(BF16), w_ref[...], preferred_element_type=F32)
    b_gate = proj[:, 0:d]
    z = proj[:, d:2 * d] * proj[:, 2 * d:3 * d]
    zbuf_ref[SUBLANES:SUBLANES + tm, :] = z
    z1 = zbuf_ref[SUBLANES - 1:SUBLANES - 1 + tm, :]
    z2 = zbuf_ref[SUBLANES - 2:SUBLANES - 2 + tm, :]
    zc = cw_ref[0:1, :] * z2 + cw_ref[1:2, :] * z1 + cw_ref[2:3, :] * z
    o_ref[...] = (b_gate * zc).astype(BF16)
    zbuf_ref[0:SUBLANES, :] = z[tm - SUBLANES:tm, :]


def _conv_pre(x2d, mod, g1, w_in, conv_w, t_len, tm):
    n, d = x2d.shape
    tpb = t_len // tm
    row = lambda i: (i, 0)
    return pl.pallas_call(
        functools.partial(_conv_pre_kernel, tpb=tpb),
        grid=(n // tm,),
        in_specs=[pl.BlockSpec((tm, d), row),
                  pl.BlockSpec((1, 6, d), lambda i: (i // tpb, 0, 0)),
                  _const_spec((1, d)),
                  _const_spec((d, 3 * d)),
                  _const_spec((CONV_WIDTH, d))],
        out_specs=pl.BlockSpec((tm, d), row),
        out_shape=jax.ShapeDtypeStruct((n, d), BF16),
        scratch_shapes=[pltpu.VMEM((tm + SUBLANES, d), F32)],
        compiler_params=_params(1),
        name="conv_pre",
    )(x2d, mod, g1.reshape(1, d), w_in.astype(BF16), conv_w.astype(F32))


def _post_kernel(x_ref, mix_ref, mod_ref, g2_ref, wo_ref, wg_ref, wu_ref, wd_ref, o_ref, *, th):
    gate1 = mod_ref[0, 2:3, :]
    sh2 = mod_ref[0, 3:4, :]
    sc2 = mod_ref[0, 4:5, :]
    gate2 = mod_ref[0, 5:6, :]
    y = jnp.dot(mix_ref[...], wo_ref[...], preferred_element_type=F32)
    x1 = x_ref[...] + gate1 * y
    ms = jnp.mean(x1 * x1, axis=-1, keepdims=True)
    h = ((x1 * lax.rsqrt(ms + NORM_EPS) * g2_ref[...]) * (1.0 + sc2) + sh2).astype(BF16)
    hidden = wg_ref.shape[1]
    acc = jnp.zeros(x1.shape, F32)
    for j in range(hidden // th):
        gt = jnp.dot(h, wg_ref[:, th * j:th * (j + 1)], preferred_element_type=F32)
        up = jnp.dot(h, wu_ref[:, th * j:th * (j + 1)], preferred_element_type=F32)
        a = (gt * jax.nn.sigmoid(gt)) * up
        acc = acc + jnp.dot(a.astype(BF16), wd_ref[th * j:th * (j + 1), :], preferred_element_type=F32)
    o_ref[...] = x1 + gate2 * acc


def _post(x2d, mix, mod, g2, w_out, w_gate, w_up, w_down, t_len, tm):
    n, d = x2d.shape
    hidden = w_gate.shape[1]
    tpb = t_len // tm
    row = lambda i: (i, 0)
    return pl.pallas_call(
        functools.partial(_post_kernel, th=256),
        grid=(n // tm,),
        in_specs=[pl.BlockSpec((tm, d), row),
                  pl.BlockSpec((tm, d), row),
                  pl.BlockSpec((1, 6, d), lambda i: (i // tpb, 0, 0)),
                  _const_spec((1, d)),
                  _const_spec((d, d)),
                  _const_spec((d, hidden)),
                  _const_spec((d, hidden)),
                  _const_spec((hidden, d))],
        out_specs=pl.BlockSpec((tm, d), row),
        out_shape=jax.ShapeDtypeStruct((n, d), F32),
        compiler_params=_params(1),
        name="mixer_out_ffn",
    )(x2d, mix, mod, g2.reshape(1, d), w_out.astype(BF16), w_gate.astype(BF16), w_up.astype(BF16),
      w_down.astype(BF16))


def kernel(x, c, positions, ada_w, ada_b, norm1_g, norm2_g, attn_w_in, attn_q_norm_g, attn_k_norm_g,
           idx_k_ln_g, idx_k_ln_b, attn_w_out, conv_w_in, conv_w, conv_w_out, ffn_w_gate, ffn_w_up,
           ffn_w_down):
    bn, t_len, d = x.shape
    depth = ada_w.shape[0]
    topk = min(TOPK_MAX, t_len // 4)
    assert topk == TOPK_MAX and t_len % topk == 0 and (t_len & (t_len - 1)) == 0
    n = bn * t_len
    tm_pre = 256
    tm_post = 512
    mod = _modulation(c, ada_w, ada_b)
    x2d = x.reshape(n, d)
    for i in range(depth):
        j = i // 2
        if i % 2 == 0:
            qT, qiT, wiT, vT, k2, kiP = _attn_pre(
                x2d, mod[i], norm1_g[i], attn_w_in[j], attn_q_norm_g[j], attn_k_norm_g[j],
                idx_k_ln_g[j], idx_k_ln_b[j], positions, bn, t_len, tm_pre)
            mix = _dsa_attention(qT, qiT, wiT, k2, kiP, vT, bn, t_len, topk).reshape(n, d)
            w_mix_out = attn_w_out[j]
        else:
            mix = _conv_pre(x2d, mod[i], norm1_g[i], conv_w_in[j], conv_w[j], t_len, tm_post)
            w_mix_out = conv_w_out[j]
        x2d = _post(x2d, mix, mod[i], norm2_g[i], w_mix_out, ffn_w_gate[i], ffn_w_up[i], ffn_w_down[i],
                    t_len, tm_post)
    return x2d.reshape(bn, t_len, d)
```

```python
import functools
import math

import numpy as np
import jax
import jax.numpy as jnp
from jax import lax
from jax.experimental import pallas as pl
from jax.experimental.pallas import tpu as pltpu

F32 = jnp.float32
BF16 = jnp.bfloat16
I32 = jnp.int32

N_HEADS = 16
HEAD_DIM = 64
N_KV_HEADS = 4
GROUP = N_HEADS // N_KV_HEADS
IDX_HEADS = 8
IDX_DIM = 64
TOPK_MAX = 256
ROPE_THETA = 10000.0
CONV_WIDTH = 3
NORM_EPS = 1e-6
LOG2E = math.log2(math.e)

LANES = 128
SUBLANES = 8
VMEM_LIMIT = 56 * 1024 * 1024

Q_COLS = N_HEADS * HEAD_DIM
KV_COLS = N_KV_HEADS * HEAD_DIM
QI_COLS = IDX_HEADS * IDX_DIM
V_AUG = HEAD_DIM + 16
SCORE_GROUP = 4
ATTN_GROUP = 4

INT_MIN = np.int32(-2**31)
KEY_NEG_INF = np.int32(np.uint32(0x807FFFFF).astype(np.int64) - 2**32)
KEY_HI_NEG_INF = np.int32(np.uint32(0x807F0000).astype(np.int64) - 2**32)


def _const_spec(shape):
    nd = len(shape)
    return pl.BlockSpec(shape, lambda *_: (0,) * nd, pipeline_mode=pl.Buffered(1))


def _params(n_axes):
    return pltpu.CompilerParams(dimension_semantics=("arbitrary",) * n_axes,
                                vmem_limit_bytes=VMEM_LIMIT)


def _mod_kernel(c_ref, w_ref, b_ref, o_ref):
    c = c_ref[...]
    ca = c * jax.nn.sigmoid(c)
    o_ref[0] = jnp.dot(ca.astype(BF16), w_ref[0].astype(BF16), preferred_element_type=F32) + b_ref[0]


def _modulation(c, ada_w, ada_b):
    depth, d, six_d = ada_w.shape
    bn = c.shape[0]
    rows = 16
    cp = jnp.zeros((rows, d), F32).at[:bn].set(c)
    tn = 1536
    out = pl.pallas_call(
        _mod_kernel,
        grid=(depth, six_d // tn),
        in_specs=[pl.BlockSpec((rows, d), lambda i, j: (0, 0)),
                  pl.BlockSpec((1, d, tn), lambda i, j: (i, 0, j)),
                  pl.BlockSpec((1, 1, tn), lambda i, j: (i, 0, j))],
        out_specs=pl.BlockSpec((1, rows, tn), lambda i, j: (i, 0, j)),
        out_shape=jax.ShapeDtypeStruct((depth, rows, six_d), F32),
        compiler_params=_params(2),
        name="adaln_mod",
    )(cp, ada_w, ada_b.reshape(depth, 1, six_d))
    return out[:, :bn].reshape(depth, bn, 6, d)


def _attn_pre_kernel(x_ref, mod_ref, g1_ref, wt_ref, posr_ref, invft_ref, gq_ref, gk_ref, lng_ref, lnb_ref,
                     qT_ref, qiT_ref, wiT_ref, vT_ref, k4_ref, ki_ref):
    tm = x_ref.shape[0]
    x = x_ref[...]
    sh = mod_ref[0, 0:1, :]
    sc = mod_ref[0, 1:2, :]
    ms = jnp.mean(x * x, axis=-1, keepdims=True)
    h = (x * lax.rsqrt(ms + NORM_EPS) * g1_ref[...]) * (1.0 + sc) + sh
    hb = h.astype(BF16)

    pT = lax.dot_general(wt_ref[...], hb, (((1,), (1,)), ((), ())), preferred_element_type=F32)
    angT = invft_ref[...] * posr_ref[0].astype(F32)
    cT = jnp.cos(angT)
    sT = jnp.sin(angT)
    half = HEAD_DIM // 2

    def rope_t(y):
        x1 = y[:half]
        x2 = y[half:]
        return jnp.concatenate([x1 * cT - x2 * sT, x2 * cT + x1 * sT], axis=0)

    gq = gq_ref[...]
    for hh in range(N_HEADS):
        xq = pT[HEAD_DIM * hh:HEAD_DIM * (hh + 1)]
        inv = lax.rsqrt(jnp.mean(xq * xq, axis=0, keepdims=True) + NORM_EPS)
        y = xq * inv * gq
        qT_ref[0, hh] = (rope_t(y) * (HEAD_DIM ** -0.5 * LOG2E)).astype(BF16)
    o0 = Q_COLS
    gk = gk_ref[...]
    for j in range(N_KV_HEADS // 2):
        pair = []
        for g in (2 * j, 2 * j + 1):
            xk = pT[o0 + HEAD_DIM * g:o0 + HEAD_DIM * (g + 1)]
            inv = lax.rsqrt(jnp.mean(xk * xk, axis=0, keepdims=True) + NORM_EPS)
            pair.append(rope_t(xk * inv * gk))
        kr = jnp.concatenate(pair, axis=0).T
        k4_ref[0, 2 * j] = kr[:, 0:HEAD_DIM].astype(BF16)
        k4_ref[0, 2 * j + 1] = kr[:, HEAD_DIM:LANES].astype(BF16)
    o0 += KV_COLS
    vc = vT_ref.shape[-1]
    ones = jnp.ones((V_AUG - HEAD_DIM, vc), BF16)
    for g in range(N_KV_HEADS):
        vg = pT[o0 + HEAD_DIM * g:o0 + HEAD_DIM * (g + 1)].astype(BF16)
        for u in range(tm // vc):
            vT_ref[0, g, u, 0:HEAD_DIM, :] = vg[:, vc * u:vc * (u + 1)]
            vT_ref[0, g, u, HEAD_DIM:V_AUG, :] = ones
    o0 += KV_COLS
    for hh in range(IDX_HEADS):
        qiT_ref[0, hh] = rope_t(pT[o0 + IDX_DIM * hh:o0 + IDX_DIM * (hh + 1)]).astype(BF16)
    o0 += QI_COLS
    t = pT[o0:o0 + IDX_DIM]
    mu = jnp.mean(t, axis=0, keepdims=True)
    dlt = t - mu
    var = jnp.mean(dlt * dlt, axis=0, keepdims=True)
    kin = rope_t(dlt * lax.rsqrt(var + NORM_EPS) * lng_ref[...] + lnb_ref[...])
    kin = jnp.concatenate([kin, jnp.zeros((LANES - IDX_DIM, tm), F32)], axis=0).T
    ki_ref[...] = kin[:, 0:IDX_DIM].astype(BF16)
    o0 += IDX_DIM
    wiT_ref[0] = pT[o0:o0 + IDX_HEADS] * (IDX_HEADS ** -0.5 * IDX_DIM ** -0.5)


def _attn_pre(x2d, mod, g1, w_in, gq, gk, lng, lnb, positions, bn, t_len, tm, vc):
    n, d = x2d.shape
    tpb = t_len // tm
    half = HEAD_DIM // 2
    ma = w_in.shape[1]
    ma_pad = -(-ma // 16) * 16
    wt = jnp.concatenate([w_in, jnp.zeros((d, ma_pad - ma), F32)], axis=1).T.astype(BF16)
    inv_freq = ROPE_THETA ** (-(jnp.arange(half, dtype=F32) * 2.0 / HEAD_DIM))
    invft = jnp.broadcast_to(inv_freq[:, None], (half, tm))
    col = lambda v: jnp.broadcast_to(v.astype(F32)[:, None], (v.shape[0], tm))
    posr = positions.reshape(bn, 1, t_len)

    row = lambda i: (i, 0)
    outs = pl.pallas_call(
        _attn_pre_kernel,
        grid=(n // tm,),
        in_specs=[pl.BlockSpec((tm, d), row),
                  pl.BlockSpec((1, 6, d), lambda i: (i // tpb, 0, 0)),
                  _const_spec((1, d)),
                  _const_spec((ma_pad, d)),
                  pl.BlockSpec((1, 1, tm), lambda i: (i // tpb, 0, i % tpb)),
                  _const_spec((half, tm)),
                  _const_spec((HEAD_DIM, tm)),
                  _const_spec((HEAD_DIM, tm)),
                  _const_spec((IDX_DIM, tm)),
                  _const_spec((IDX_DIM, tm))],
        out_specs=[pl.BlockSpec((1, N_HEADS, HEAD_DIM, tm), lambda i: (i // tpb, 0, 0, i % tpb)),
                   pl.BlockSpec((1, IDX_HEADS, IDX_DIM, tm), lambda i: (i // tpb, 0, 0, i % tpb)),
                   pl.BlockSpec((1, IDX_HEADS, tm), lambda i: (i // tpb, 0, i % tpb)),
                   pl.BlockSpec((1, N_KV_HEADS, tm // vc, V_AUG, vc), lambda i: (i // tpb, 0, i % tpb, 0, 0)),
                   pl.BlockSpec((1, N_KV_HEADS, tm, HEAD_DIM), lambda i: (i // tpb, 0, i % tpb, 0)),
                   pl.BlockSpec((tm, IDX_DIM), row)],
        out_shape=[jax.ShapeDtypeStruct((bn, N_HEADS, HEAD_DIM, t_len), BF16),
                   jax.ShapeDtypeStruct((bn, IDX_HEADS, IDX_DIM, t_len), BF16),
                   jax.ShapeDtypeStruct((bn, IDX_HEADS, t_len), F32),
                   jax.ShapeDtypeStruct((bn, N_KV_HEADS, t_len // vc, V_AUG, vc), BF16),
                   jax.ShapeDtypeStruct((bn, N_KV_HEADS, t_len, HEAD_DIM), BF16),
                   jax.ShapeDtypeStruct((n, IDX_DIM), BF16)],
        compiler_params=_params(1),
        name="attn_pre",
    )(x2d, mod, g1.reshape(1, d), wt, posr, invft, col(gq), col(gk), col(lng), col(lnb))
    return outs


def _dsa_kernel(qT_ref, qiT_ref, wiT_ref, k4_ref, ki_ref, vT_ref, o_ref,
                st_ref, hi_ref, m_ref, al_ref, acc_ref, bias_ref, lg_ref, left_ref, ltri_ref, *, topk, rb):
    tq = o_ref.shape[1]
    kc = tq
    i = pl.program_id(1)
    n_c = i + 1
    q0 = i * tq

    tcol = q0 + lax.broadcasted_iota(I32, (kc, tq), 1)
    srow0 = lax.broadcasted_iota(I32, (kc, tq), 0)
    wi = wiT_ref[0]
    qi_all = jnp.concatenate([qiT_ref[0, hh] for hh in range(IDX_HEADS)], axis=1)

    def score_chunk(c):
        r0 = pl.multiple_of(c * kc, kc)
        ki = ki_ref[0, pl.ds(r0, kc), :]
        s_all = jnp.dot(ki, qi_all, preferred_element_type=F32)
        acc = jnp.zeros((kc, tq), F32)
        for hh in range(IDX_HEADS):
            acc = acc + jnp.maximum(s_all[:, tq * hh:tq * (hh + 1)], 0.0) * wi[hh:hh + 1, :]
        causal = (srow0 + r0) <= tcol
        sc = jnp.where(causal, acc, -jnp.inf)
        st_ref[pl.ds(r0, kc), :] = sc
        top = lax.bitcast_convert_type(sc, I32) & np.int32(-65536)
        hi_ref[pl.ds(r0, kc), :] = lax.bitcast_convert_type(top, F32).astype(BF16)

    def score_group(j, carry):
        for u in range(SCORE_GROUP):
            score_chunk(SCORE_GROUP * j + u)
        return carry

    def score_single(c, carry):
        score_chunk(c)
        return carry

    n_grp = n_c // SCORE_GROUP
    lax.fori_loop(0, n_grp, score_group, 0)
    lax.fori_loop(n_grp * SCORE_GROUP, n_c, score_single, 0)

    n_big = (n_c * kc) // rb
    n_small = (n_c * kc - n_big * rb) // kc
    n_par = 4
    pack = 2 * SUBLANES

    def all_sublanes(tot):
        for sh in (4, 2, 1):
            tot = tot + pltpu.roll(tot, sh, 0)
        return tot

    def two_level(block_fn, init):
        acc = lax.fori_loop(0, n_big, lambda r, a: block_fn(pl.multiple_of(r * rb, rb), rb, a), init)
        base = n_big * rb
        return lax.fori_loop(0, n_small, lambda r, a: block_fn(pl.multiple_of(base + r * kc, kc), kc, a), acc)

    def count(pred):
        def block(r0, rows, acc):
            blk = st_ref[pl.ds(r0, rows), :].reshape(rows // SUBLANES, SUBLANES, tq)
            hit = pred(blk).astype(I32).reshape(rows // (SUBLANES * n_par), n_par, SUBLANES, tq)
            return acc + jnp.sum(hit, axis=0)
        acc = two_level(block, jnp.zeros((n_par, SUBLANES, tq), I32))
        return all_sublanes(jnp.sum(acc, axis=0))

    def count_hi(cand16):
        def block(r0, rows, acc):
            blk = hi_ref[pl.ds(r0, rows), :].reshape(rows // pack, pack, tq)
            hit = jnp.where(blk >= cand16[None], jnp.ones((), BF16), jnp.zeros((), BF16))
            chains = [hit[k] for k in range(n_par)]
            for g in range(n_par, rows // pack):
                chains[g % n_par] = chains[g % n_par] + hit[g]
            return acc + ((chains[0] + chains[1]) + (chains[2] + chains[3])).astype(F32)
        acc = two_level(block, jnp.zeros((pack, tq), F32))
        return all_sublanes((acc[0:SUBLANES] + acc[SUBLANES:pack]).astype(I32))

    def key_to_float(u):
        ks = u ^ INT_MIN
        bits = jnp.where(ks < 0, ks ^ np.int32(0x7FFFFFFF), ks)
        return ks, bits

    def hi_body(it, carry):
        tau_u, cnt_tau = carry
        cand_u = tau_u | lax.shift_left(np.int32(1), np.int32(31) - it)
        ks, bits = key_to_float(cand_u)
        cand_top = lax.bitcast_convert_type(bits & np.int32(-65536), F32)
        cand16 = jnp.concatenate([cand_top, cand_top], axis=0).astype(BF16)
        cnt = count_hi(cand16)
        take = (cnt >= topk) | (ks < KEY_HI_NEG_INF)
        return jnp.where(take, cand_u, tau_u), jnp.where(take, cnt, cnt_tau)

    def lo_body(it, carry):
        tau_u, cnt_tau = carry
        cand_u = tau_u | lax.shift_left(np.int32(1), np.int32(15) - it)
        ks, bits = key_to_float(cand_u)
        cand_f = lax.bitcast_convert_type(bits, F32)
        cnt = count(lambda blk: blk >= cand_f[None])
        take = (cnt >= topk) | (ks < KEY_NEG_INF)
        return jnp.where(take, cand_u, tau_u), jnp.where(take, cnt, cnt_tau)

    def lo_stage(state, first, n):
        tau_u, cnt_tau, _ = state
        tau_u, cnt_tau = lax.fori_loop(first, first + n, lo_body, (tau_u, cnt_tau))
        tau8 = lax.bitcast_convert_type(key_to_float(tau_u)[1], F32)
        return tau_u, cnt_tau, count(lambda blk: blk > tau8[None])

    def pending(state):
        tau_u, cnt_tau, cnt_gt = state
        open_lane = ((cnt_tau != topk) & (cnt_gt >= topk)) | ((tau_u ^ INT_MIN) < KEY_NEG_INF)
        return jnp.max(open_lane.astype(I32)) > 0

    zero8 = jnp.zeros((SUBLANES, tq), I32)
    tau_u, cnt_tau = lax.fori_loop(0, 16, hi_body, (zero8, zero8))
    state = lo_stage((tau_u, cnt_tau, zero8), 0, 10)
    state = lax.cond(pending(state), lambda s: lo_stage(s, 10, 2), lambda s: s, state)
    state = lax.cond(pending(state), lambda s: lo_stage(s, 12, 4), lambda s: s, state)
    tau_u, _, cnt_gt = state
    tau8 = lax.bitcast_convert_type(key_to_float(tau_u)[1], F32)
    left_ref[...] = jnp.where(tau8 == -jnp.inf, 0.0, (topk - cnt_gt).astype(F32))
    tau1 = tau8[0:1]
    ri = lax.broadcasted_iota(I32, (kc, kc), 0)
    ci = lax.broadcasted_iota(I32, (kc, kc), 1)
    ltri_ref[...] = jnp.where(ci <= ri, 1.0, 0.0).astype(BF16)

    m_ref[...] = jnp.full(m_ref.shape, -(2.0 ** 100), F32)
    acc_ref[...] = jnp.zeros(acc_ref.shape, F32)

    def set_bias(c):
        r0 = pl.multiple_of(c * kc, kc)
        sc = st_ref[pl.ds(r0, kc), :]
        tie = sc == tau1
        seen = jnp.dot(ltri_ref[...], jnp.where(tie, 1.0, 0.0).astype(BF16), preferred_element_type=F32)
        left = left_ref[0:1, :]
        mask = (sc > tau1) | (tie & (seen <= left))
        bias_ref[...] = jnp.where(mask, 0.0, -jnp.inf).astype(BF16)
        left_ref[...] = jnp.broadcast_to(left - seen[kc - 1:kc, :], left_ref.shape)

    def logits_stage(c, hh):
        r0 = pl.multiple_of(c * kc, kc)
        kg = k4_ref[0, hh // GROUP, pl.ds(r0, kc), :]
        lg = jnp.dot(kg, qT_ref[0, hh], preferred_element_type=F32).astype(BF16) + bias_ref[...]
        lg_ref[hh] = lg
        parts = [lg[pack * g:pack * (g + 1)] for g in range(kc // pack)]
        while len(parts) > 1:
            parts = [jnp.maximum(parts[g], parts[g + 1]) for g in range(0, len(parts), 2)]
        mx = parts[0].astype(F32)
        m_old = m_ref[hh:hh + 1, :]
        m_new = jnp.maximum(m_old, jnp.max(mx, axis=0, keepdims=True))
        m_ref[hh:hh + 1, :] = m_new
        al_ref[hh:hh + 1, :] = jnp.exp2(m_old - m_new)

    def value_stage(c, hh):
        vt = vT_ref[0, hh // GROUP, c]
        m_b = jnp.broadcast_to(m_ref[hh:hh + 1, :], (pack, tq)).astype(BF16)
        p = jnp.exp2(lg_ref[hh].reshape(kc // pack, pack, tq) - m_b[None]).reshape(kc, tq)
        acc_ref[hh] = acc_ref[hh] * al_ref[hh:hh + 1, :] + jnp.dot(vt, p, preferred_element_type=F32)

    set_bias(0)
    for hh in range(N_HEADS):
        logits_stage(0, hh)

    def attn_step(c):
        set_bias(c)
        for hh in range(N_HEADS):
            value_stage(c - 1, hh)
            logits_stage(c, hh)

    def attn_group(j, carry):
        for u in range(ATTN_GROUP):
            attn_step(ATTN_GROUP * j + 1 + u)
        return carry

    def attn_single(c, carry):
        attn_step(c)
        return carry

    n_grp = (n_c - 1) // ATTN_GROUP
    lax.fori_loop(0, n_grp, attn_group, 0)
    lax.fori_loop(ATTN_GROUP * n_grp + 1, n_c, attn_single, 0)
    for hh in range(N_HEADS):
        value_stage(n_c - 1, hh)

    for hp in range(N_HEADS // 2):
        parts = []
        for hh in (2 * hp, 2 * hp + 1):
            a = acc_ref[hh]
            parts.append(a[0:HEAD_DIM] / a[HEAD_DIM:HEAD_DIM + 1])
        pair = jnp.concatenate(parts, axis=0)
        o_ref[0, :, LANES * hp:LANES * (hp + 1)] = pair.T.astype(BF16)


def _dsa_attention(qT, qiT, wiT, k4, ki, vT, bn, t_len, topk):
    tq = topk
    rb = 1024
    kern = functools.partial(_dsa_kernel, topk=topk, rb=rb)
    return pl.pallas_call(
        kern,
        grid=(bn, t_len // tq),
        in_specs=[pl.BlockSpec((1, N_HEADS, HEAD_DIM, tq), lambda b, i: (b, 0, 0, i)),
                  pl.BlockSpec((1, IDX_HEADS, IDX_DIM, tq), lambda b, i: (b, 0, 0, i)),
                  pl.BlockSpec((1, IDX_HEADS, tq), lambda b, i: (b, 0, i)),
                  pl.BlockSpec((1, N_KV_HEADS, t_len, HEAD_DIM), lambda b, i: (b, 0, 0, 0),
                               pipeline_mode=pl.Buffered(1)),
                  pl.BlockSpec((1, t_len, IDX_DIM), lambda b, i: (b, 0, 0), pipeline_mode=pl.Buffered(1)),
                  pl.BlockSpec((1, N_KV_HEADS, t_len // tq, V_AUG, tq), lambda b, i: (b, 0, 0, 0, 0),
                               pipeline_mode=pl.Buffered(1))],
        out_specs=pl.BlockSpec((1, tq, Q_COLS), lambda b, i: (b, i, 0)),
        out_shape=jax.ShapeDtypeStruct((bn, t_len, Q_COLS), BF16),
        scratch_shapes=[pltpu.VMEM((t_len, tq), F32),
                        pltpu.VMEM((t_len, tq), BF16),
                        pltpu.VMEM((N_HEADS, tq), F32),
                        pltpu.VMEM((N_HEADS, tq), F32),
                        pltpu.VMEM((N_HEADS, V_AUG, tq), F32),
                        pltpu.VMEM((tq, tq), BF16),
                        pltpu.VMEM((N_HEADS, tq, tq), BF16),
                        pltpu.VMEM((SUBLANES, tq), F32),
                        pltpu.VMEM((tq, tq), BF16)],
        compiler_params=_params(2),
        name="dsa_attention",
    )(qT, qiT, wiT, k4, ki.reshape(bn, t_len, IDX_DIM), vT)


def _conv_pre_kernel(x_ref, mod_ref, g1_ref, w_ref, cw_ref, o_ref, zbuf_ref, *, tpb):
    tm, d = x_ref.shape
    i = pl.program_id(0)

    @pl.when(i % tpb == 0)
    def _():
        zbuf_ref[0:SUBLANES, :] = jnp.zeros((SUBLANES, d), F32)

    x = x_ref[...]
    sh = mod_ref[0, 0:1, :]
    sc = mod_ref[0, 1:2, :]
    ms = jnp.mean(x * x, axis=-1, keepdims=True)
    h = (x * lax.rsqrt(ms + NORM_EPS) * g1_ref[...]) * (1.0 + sc) + sh
    proj = jnp.dot(h.astype(BF16), w_ref[...], preferred_element_type=F32)
    b_gate = proj[:, 0:d]
    z = proj[:, d:2 * d] * proj[:, 2 * d:3 * d]
    zbuf_ref[SUBLANES:SUBLANES + tm, :] = z
    z1 = zbuf_ref[SUBLANES - 1:SUBLANES - 1 + tm, :]
    z2 = zbuf_ref[SUBLANES - 2:SUBLANES - 2 + tm, :]
    zc = cw_ref[0:1, :] * z2 + cw_ref[1:2, :] * z1 + cw_ref[2:3, :] * z
    o_ref[...] = (b_gate * zc).astype(BF16)
    zbuf_ref[0:SUBLANES, :] = z[tm - SUBLANES:tm, :]


def _conv_pre(x2d, mod, g1, w_in, conv_w, t_len, tm):
    n, d = x2d.shape
    tpb = t_len // tm
    row = lambda i: (i, 0)
    return pl.pallas_call(
        functools.partial(_conv_pre_kernel, tpb=tpb),
        grid=(n // tm,),
        in_specs=[pl.BlockSpec((tm, d), row),
                  pl.BlockSpec((1, 6, d), lambda i: (i // tpb, 0, 0)),
                  _const_spec((1, d)),
                  _const_spec((d, 3 * d)),
                  _const_spec((CONV_WIDTH, d))],
        out_specs=pl.BlockSpec((tm, d), row),
        out_shape=jax.ShapeDtypeStruct((n, d), BF16),
        scratch_shapes=[pltpu.VMEM((tm + SUBLANES, d), F32)],
        compiler_params=_params(1),
        name="conv_pre",
    )(x2d, mod, g1.reshape(1, d), w_in.astype(BF16), conv_w.astype(F32))


def _post_kernel(x_ref, mix_ref, mod_ref, g2_ref, wo_ref, wg_ref, wu_ref, wd_ref, o_ref, *, th):
    gate1 = mod_ref[0, 2:3, :]
    sh2 = mod_ref[0, 3:4, :]
    sc2 = mod_ref[0, 4:5, :]
    gate2 = mod_ref[0, 5:6, :]
    y = jnp.dot(mix_ref[...], wo_ref[...], preferred_element_type=F32)
    x1 = x_ref[...] + gate1 * y
    ms = jnp.mean(x1 * x1, axis=-1, keepdims=True)
    h = ((x1 * lax.rsqrt(ms + NORM_EPS) * g2_ref[...]) * (1.0 + sc2) + sh2).astype(BF16)
    hidden = wg_ref.shape[1]
    acc = jnp.zeros(x1.shape, F32)
    for j in range(hidden // th):
        gt = jnp.dot(h, wg_ref[:, th * j:th * (j + 1)], preferred_element_type=F32)
        up = jnp.dot(h, wu_ref[:, th * j:th * (j + 1)], preferred_element_type=F32)
        a = (gt * jax.nn.sigmoid(gt)) * up
        acc = acc + jnp.dot(a.astype(BF16), wd_ref[th * j:th * (j + 1), :], preferred_element_type=F32)
    o_ref[...] = x1 + gate2 * acc


def _post(x2d, mix, mod, g2, w_out, w_gate, w_up, w_down, t_len, tm):
    n, d = x2d.shape
    hidden = w_gate.shape[1]
    tpb = t_len // tm
    row = lambda i: (i, 0)
    return pl.pallas_call(
        functools.partial(_post_kernel, th=256),
        grid=(n // tm,),
        in_specs=[pl.BlockSpec((tm, d), row),
                  pl.BlockSpec((tm, d), row),
                  pl.BlockSpec((1, 6, d), lambda i: (i // tpb, 0, 0)),
                  _const_spec((1, d)),
                  _const_spec((d, d)),
                  _const_spec((d, hidden)),
                  _const_spec((d, hidden)),
                  _const_spec((hidden, d))],
        out_specs=pl.BlockSpec((tm, d), row),
        out_shape=jax.ShapeDtypeStruct((n, d), F32),
        compiler_params=_params(1),
        name="mixer_out_ffn",
    )(x2d, mix, mod, g2.reshape(1, d), w_out.astype(BF16), w_gate.astype(BF16), w_up.astype(BF16),
      w_down.astype(BF16))


def kernel(x, c, positions, ada_w, ada_b, norm1_g, norm2_g, attn_w_in, attn_q_norm_g, attn_k_norm_g,
           idx_k_ln_g, idx_k_ln_b, attn_w_out, conv_w_in, conv_w, conv_w_out, ffn_w_gate, ffn_w_up,
           ffn_w_down):
    bn, t_len, d = x.shape
    depth = ada_w.shape[0]
    topk = min(TOPK_MAX, t_len // 4)
    tm_pre = 512
    tm_post = 512
    assert topk == TOPK_MAX and t_len % tm_pre == 0 and t_len % tm_post == 0 and tm_pre % topk == 0
    n = bn * t_len
    mod = _modulation(c, ada_w, ada_b)
    x2d = x.reshape(n, d)
    for i in range(depth):
        j = i // 2
        if i % 2 == 0:
            qT, qiT, wiT, vT, k4, ki = _attn_pre(
                x2d, mod[i], norm1_g[i], attn_w_in[j], attn_q_norm_g[j], attn_k_norm_g[j],
                idx_k_ln_g[j], idx_k_ln_b[j], positions, bn, t_len, tm_pre, topk)
            mix = _dsa_attention(qT, qiT, wiT, k4, ki, vT, bn, t_len, topk).reshape(n, d)
            w_mix_out = attn_w_out[j]
        else:
            mix = _conv_pre(x2d, mod[i], norm1_g[i], conv_w_in[j], conv_w[j], t_len, tm_post)
            w_mix_out = conv_w_out[j]
        x2d = _post(x2d, mix, mod[i], norm2_g[i], w_mix_out, ffn_w_gate[i], ffn_w_up[i], ffn_w_down[i],
                    t_len, tm_post)
    return x2d.reshape(bn, t_len, d)
```

```python
import functools
import math

import numpy as np
import jax
import jax.numpy as jnp
from jax import lax
from jax.experimental import pallas as pl
from jax.experimental.pallas import tpu as pltpu

F32 = jnp.float32
BF16 = jnp.bfloat16
I32 = jnp.int32

N_HEADS = 16
HEAD_DIM = 64
N_KV_HEADS = 4
GROUP = N_HEADS // N_KV_HEADS
IDX_HEADS = 8
IDX_DIM = 64
TOPK_MAX = 256
ROPE_THETA = 10000.0
CONV_WIDTH = 3
NORM_EPS = 1e-6
LOG2E = math.log2(math.e)

LANES = 128
SUBLANES = 8
VMEM_LIMIT = 56 * 1024 * 1024

Q_COLS = N_HEADS * HEAD_DIM
KV_COLS = N_KV_HEADS * HEAD_DIM
QI_COLS = IDX_HEADS * IDX_DIM
V_AUG = HEAD_DIM + 16
SCORE_GROUP = 4
ATTN_GROUP = 4

INT_MIN = np.int32(-2**31)
KEY_NEG_INF = np.int32(np.uint32(0x807FFFFF).astype(np.int64) - 2**32)
KEY_HI_NEG_INF = np.int32(np.uint32(0x807F0000).astype(np.int64) - 2**32)


def _const_spec(shape):
    nd = len(shape)
    return pl.BlockSpec(shape, lambda *_: (0,) * nd, pipeline_mode=pl.Buffered(1))


def _params(n_axes):
    return pltpu.CompilerParams(dimension_semantics=("arbitrary",) * n_axes,
                                vmem_limit_bytes=VMEM_LIMIT)


def _mod_kernel(c_ref, w_ref, b_ref, o_ref):
    c = c_ref[...]
    ca = c * jax.nn.sigmoid(c)
    o_ref[0] = jnp.dot(ca.astype(BF16), w_ref[0].astype(BF16), preferred_element_type=F32) + b_ref[0]


def _modulation(c, ada_w, ada_b):
    depth, d, six_d = ada_w.shape
    bn = c.shape[0]
    rows = 16
    cp = jnp.zeros((rows, d), F32).at[:bn].set(c)
    tn = 1536
    out = pl.pallas_call(
        _mod_kernel,
        grid=(depth, six_d // tn),
        in_specs=[pl.BlockSpec((rows, d), lambda i, j: (0, 0)),
                  pl.BlockSpec((1, d, tn), lambda i, j: (i, 0, j)),
                  pl.BlockSpec((1, 1, tn), lambda i, j: (i, 0, j))],
        out_specs=pl.BlockSpec((1, rows, tn), lambda i, j: (i, 0, j)),
        out_shape=jax.ShapeDtypeStruct((depth, rows, six_d), F32),
        compiler_params=_params(2),
        name="adaln_mod",
    )(cp, ada_w, ada_b.reshape(depth, 1, six_d))
    return out[:, :bn].reshape(depth, bn, 6, d)


def _attn_pre_kernel(x_ref, mod_ref, g1_ref, wt_ref, posr_ref, invft_ref, gq_ref, gk_ref, lng_ref, lnb_ref,
                     qT_ref, qiT_ref, wiT_ref, vT_ref, k4_ref, ki_ref):
    tm = x_ref.shape[0]
    x = x_ref[...]
    sh = mod_ref[0, 0:1, :]
    sc = mod_ref[0, 1:2, :]
    ms = jnp.mean(x * x, axis=-1, keepdims=True)
    h = (x * lax.rsqrt(ms + NORM_EPS) * g1_ref[...]) * (1.0 + sc) + sh
    hb = h.astype(BF16)

    pT = lax.dot_general(wt_ref[...], hb, (((1,), (1,)), ((), ())), preferred_element_type=F32)
    angT = invft_ref[...] * posr_ref[0].astype(F32)
    cT = jnp.cos(angT)
    sT = jnp.sin(angT)
    half = HEAD_DIM // 2

    def rope_t(y):
        x1 = y[:half]
        x2 = y[half:]
        return jnp.concatenate([x1 * cT - x2 * sT, x2 * cT + x1 * sT], axis=0)

    gq = gq_ref[...]
    for hh in range(N_HEADS):
        xq = pT[HEAD_DIM * hh:HEAD_DIM * (hh + 1)]
        inv = lax.rsqrt(jnp.mean(xq * xq, axis=0, keepdims=True) + NORM_EPS)
        y = xq * inv * gq
        qT_ref[0, hh] = (rope_t(y) * (HEAD_DIM ** -0.5 * LOG2E)).astype(BF16)
    o0 = Q_COLS
    gk = gk_ref[...]
    for j in range(N_KV_HEADS // 2):
        pair = []
        for g in (2 * j, 2 * j + 1):
            xk = pT[o0 + HEAD_DIM * g:o0 + HEAD_DIM * (g + 1)]
            inv = lax.rsqrt(jnp.mean(xk * xk, axis=0, keepdims=True) + NORM_EPS)
            pair.append(rope_t(xk * inv * gk))
        kr = jnp.concatenate(pair, axis=0).T
        k4_ref[0, 2 * j] = kr[:, 0:HEAD_DIM].astype(BF16)
        k4_ref[0, 2 * j + 1] = kr[:, HEAD_DIM:LANES].astype(BF16)
    o0 += KV_COLS
    vc = vT_ref.shape[-1]
    ones = jnp.ones((V_AUG - HEAD_DIM, vc), BF16)
    for g in range(N_KV_HEADS):
        vg = pT[o0 + HEAD_DIM * g:o0 + HEAD_DIM * (g + 1)].astype(BF16)
        for u in range(tm // vc):
            vT_ref[0, g, u, 0:HEAD_DIM, :] = vg[:, vc * u:vc * (u + 1)]
            vT_ref[0, g, u, HEAD_DIM:V_AUG, :] = ones
    o0 += KV_COLS
    for hh in range(IDX_HEADS):
        qiT_ref[0, hh] = rope_t(pT[o0 + IDX_DIM * hh:o0 + IDX_DIM * (hh + 1)]).astype(BF16)
    o0 += QI_COLS
    t = pT[o0:o0 + IDX_DIM]
    mu = jnp.mean(t, axis=0, keepdims=True)
    dlt = t - mu
    var = jnp.mean(dlt * dlt, axis=0, keepdims=True)
    kin = rope_t(dlt * lax.rsqrt(var + NORM_EPS) * lng_ref[...] + lnb_ref[...])
    kin = jnp.concatenate([kin, jnp.zeros((LANES - IDX_DIM, tm), F32)], axis=0).T
    ki_ref[...] = kin[:, 0:IDX_DIM].astype(BF16)
    o0 += IDX_DIM
    wiT_ref[0] = pT[o0:o0 + IDX_HEADS] * (IDX_HEADS ** -0.5 * IDX_DIM ** -0.5)


def _attn_pre(x2d, mod, g1, w_in, gq, gk, lng, lnb, positions, bn, t_len, tm, vc):
    n, d = x2d.shape
    tpb = t_len // tm
    half = HEAD_DIM // 2
    ma = w_in.shape[1]
    ma_pad = -(-ma // 16) * 16
    wt = jnp.concatenate([w_in, jnp.zeros((d, ma_pad - ma), F32)], axis=1).T.astype(BF16)
    inv_freq = ROPE_THETA ** (-(jnp.arange(half, dtype=F32) * 2.0 / HEAD_DIM))
    invft = jnp.broadcast_to(inv_freq[:, None], (half, tm))
    col = lambda v: jnp.broadcast_to(v.astype(F32)[:, None], (v.shape[0], tm))
    posr = positions.reshape(bn, 1, t_len)

    row = lambda i: (i, 0)
    outs = pl.pallas_call(
        _attn_pre_kernel,
        grid=(n // tm,),
        in_specs=[pl.BlockSpec((tm, d), row),
                  pl.BlockSpec((1, 6, d), lambda i: (i // tpb, 0, 0)),
                  _const_spec((1, d)),
                  _const_spec((ma_pad, d)),
                  pl.BlockSpec((1, 1, tm), lambda i: (i // tpb, 0, i % tpb)),
                  _const_spec((half, tm)),
                  _const_spec((HEAD_DIM, tm)),
                  _const_spec((HEAD_DIM, tm)),
                  _const_spec((IDX_DIM, tm)),
                  _const_spec((IDX_DIM, tm))],
        out_specs=[pl.BlockSpec((1, N_HEADS, HEAD_DIM, tm), lambda i: (i // tpb, 0, 0, i % tpb)),
                   pl.BlockSpec((1, IDX_HEADS, IDX_DIM, tm), lambda i: (i // tpb, 0, 0, i % tpb)),
                   pl.BlockSpec((1, IDX_HEADS, tm), lambda i: (i // tpb, 0, i % tpb)),
                   pl.BlockSpec((1, N_KV_HEADS, tm // vc, V_AUG, vc), lambda i: (i // tpb, 0, i % tpb, 0, 0)),
                   pl.BlockSpec((1, N_KV_HEADS, tm, HEAD_DIM), lambda i: (i // tpb, 0, i % tpb, 0)),
                   pl.BlockSpec((tm, IDX_DIM), row)],
        out_shape=[jax.ShapeDtypeStruct((bn, N_HEADS, HEAD_DIM, t_len), BF16),
                   jax.ShapeDtypeStruct((bn, IDX_HEADS, IDX_DIM, t_len), BF16),
                   jax.ShapeDtypeStruct((bn, IDX_HEADS, t_len), F32),
                   jax.ShapeDtypeStruct((bn, N_KV_HEADS, t_len // vc, V_AUG, vc), BF16),
                   jax.ShapeDtypeStruct((bn, N_KV_HEADS, t_len, HEAD_DIM), BF16),
                   jax.ShapeDtypeStruct((n, IDX_DIM), BF16)],
        compiler_params=_params(1),
        name="attn_pre",
    )(x2d, mod, g1.reshape(1, d), wt, posr, invft, col(gq), col(gk), col(lng), col(lnb))
    return outs


def _dsa_kernel(qT_ref, qiT_ref, wiT_ref, k4_ref, ki_ref, vT_ref, o_ref,
                st_ref, hi_ref, m_ref, al_ref, acc_ref, bias_ref, lg_ref, left_ref, ltri_ref, *, topk, rb):
    tq = o_ref.shape[1]
    kc = tq
    i = pl.program_id(1)
    n_c = i + 1
    q0 = i * tq

    tcol = q0 + lax.broadcasted_iota(I32, (kc, tq), 1)
    srow0 = lax.broadcasted_iota(I32, (kc, tq), 0)
    wi = wiT_ref[0]
    qi_all = jnp.concatenate([qiT_ref[0, hh] for hh in range(IDX_HEADS)], axis=1)

    def score_chunk(c):
        r0 = pl.multiple_of(c * kc, kc)
        ki = ki_ref[0, pl.ds(r0, kc), :]
        s_all = jnp.dot(ki, qi_all, preferred_element_type=F32)
        acc = jnp.zeros((kc, tq), F32)
        for hh in range(IDX_HEADS):
            acc = acc + jnp.maximum(s_all[:, tq * hh:tq * (hh + 1)], 0.0) * wi[hh:hh + 1, :]
        causal = (srow0 + r0) <= tcol
        sc = jnp.where(causal, acc, -jnp.inf)
        st_ref[pl.ds(r0, kc), :] = sc
        top = lax.bitcast_convert_type(sc, I32) & np.int32(-65536)
        hi_ref[pl.ds(r0, kc), :] = lax.bitcast_convert_type(top, F32).astype(BF16)

    def score_group(j, carry):
        for u in range(SCORE_GROUP):
            score_chunk(SCORE_GROUP * j + u)
        return carry

    def score_single(c, carry):
        score_chunk(c)
        return carry

    n_grp = n_c // SCORE_GROUP
    lax.fori_loop(0, n_grp, score_group, 0)
    lax.fori_loop(n_grp * SCORE_GROUP, n_c, score_single, 0)

    n_big = (n_c * kc) // rb
    n_small = (n_c * kc - n_big * rb) // kc
    n_par = 4
    pack = 2 * SUBLANES

    def all_sublanes(tot):
        for sh in (4, 2, 1):
            tot = tot + pltpu.roll(tot, sh, 0)
        return tot

    def two_level(block_fn, init):
        acc = lax.fori_loop(0, n_big, lambda r, a: block_fn(pl.multiple_of(r * rb, rb), rb, a), init)
        base = n_big * rb
        return lax.fori_loop(0, n_small, lambda r, a: block_fn(pl.multiple_of(base + r * kc, kc), kc, a), acc)

    def count(pred):
        def block(r0, rows, acc):
            blk = st_ref[pl.ds(r0, rows), :].reshape(rows // SUBLANES, SUBLANES, tq)
            hit = pred(blk).astype(I32).reshape(rows // (SUBLANES * n_par), n_par, SUBLANES, tq)
            return acc + jnp.sum(hit, axis=0)
        acc = two_level(block, jnp.zeros((n_par, SUBLANES, tq), I32))
        return all_sublanes(jnp.sum(acc, axis=0))

    def count_hi(cand16):
        def block(r0, rows, acc):
            blk = hi_ref[pl.ds(r0, rows), :].reshape(rows // pack, pack, tq)
            hit = jnp.where(blk >= cand16[None], jnp.ones((), BF16), jnp.zeros((), BF16))
            chains = [hit[k] for k in range(n_par)]
            for g in range(n_par, rows // pack):
                chains[g % n_par] = chains[g % n_par] + hit[g]
            return acc + ((chains[0] + chains[1]) + (chains[2] + chains[3])).astype(F32)
        acc = two_level(block, jnp.zeros((pack, tq), F32))
        return all_sublanes((acc[0:SUBLANES] + acc[SUBLANES:pack]).astype(I32))

    def key_to_float(u):
        ks = u ^ INT_MIN
        bits = jnp.where(ks < 0, ks ^ np.int32(0x7FFFFFFF), ks)
        return ks, bits

    def hi_body(it, carry):
        tau_u, cnt_tau = carry
        cand_u = tau_u | lax.shift_left(np.int32(1), np.int32(31) - it)
        ks, bits = key_to_float(cand_u)
        cand_top = lax.bitcast_convert_type(bits & np.int32(-65536), F32)
        cand16 = jnp.concatenate([cand_top, cand_top], axis=0).astype(BF16)
        cnt = count_hi(cand16)
        take = (cnt >= topk) | (ks < KEY_HI_NEG_INF)
        return jnp.where(take, cand_u, tau_u), jnp.where(take, cnt, cnt_tau)

    def lo_body(it, carry):
        tau_u, cnt_tau = carry
        cand_u = tau_u | lax.shift_left(np.int32(1), np.int32(15) - it)
        ks, bits = key_to_float(cand_u)
        cand_f = lax.bitcast_convert_type(bits, F32)
        cnt = count(lambda blk: blk >= cand_f[None])
        take = (cnt >= topk) | (ks < KEY_NEG_INF)
        return jnp.where(take, cand_u, tau_u), jnp.where(take, cnt, cnt_tau)

    def lo_stage(state, first, n):
        tau_u, cnt_tau, _ = state
        tau_u, cnt_tau = lax.fori_loop(first, first + n, lo_body, (tau_u, cnt_tau))
        tau8 = lax.bitcast_convert_type(key_to_float(tau_u)[1], F32)
        return tau_u, cnt_tau, count(lambda blk: blk > tau8[None])

    def pending(state):
        tau_u, cnt_tau, cnt_gt = state
        open_lane = ((cnt_tau != topk) & (cnt_gt >= topk)) | ((tau_u ^ INT_MIN) < KEY_NEG_INF)
        return jnp.max(open_lane.astype(I32)) > 0

    zero8 = jnp.zeros((SUBLANES, tq), I32)
    tau_u, cnt_tau = lax.fori_loop(0, 16, hi_body, (zero8, zero8))
    state = lo_stage((tau_u, cnt_tau, zero8), 0, 10)
    state = lax.cond(pending(state), lambda s: lo_stage(s, 10, 2), lambda s: s, state)
    state = lax.cond(pending(state), lambda s: lo_stage(s, 12, 4), lambda s: s, state)
    tau_u, _, cnt_gt = state
    tau8 = lax.bitcast_convert_type(key_to_float(tau_u)[1], F32)
    left_ref[...] = jnp.where(tau8 == -jnp.inf, 0.0, (topk - cnt_gt).astype(F32))
    tau1 = tau8[0:1]
    ri = lax.broadcasted_iota(I32, (kc, kc), 0)
    ci = lax.broadcasted_iota(I32, (kc, kc), 1)
    ltri_ref[...] = jnp.where(ci <= ri, 1.0, 0.0).astype(BF16)

    m_ref[...] = jnp.full(m_ref.shape, -(2.0 ** 100), F32)
    acc_ref[...] = jnp.zeros(acc_ref.shape, F32)

    def set_bias(c):
        r0 = pl.multiple_of(c * kc, kc)
        sc = st_ref[pl.ds(r0, kc), :]
        tie = sc == tau1
        seen = jnp.dot(ltri_ref[...], jnp.where(tie, 1.0, 0.0).astype(BF16), preferred_element_type=F32)
        left = left_ref[0:1, :]
        mask = (sc > tau1) | (tie & (seen <= left))
        bias_ref[...] = jnp.where(mask, 0.0, -jnp.inf).astype(BF16)
        left_ref[...] = jnp.broadcast_to(left - seen[kc - 1:kc, :], left_ref.shape)

    def logits_stage(c, hh):
        r0 = pl.multiple_of(c * kc, kc)
        kg = k4_ref[0, hh // GROUP, pl.ds(r0, kc), :]
        lg = jnp.dot(kg, qT_ref[0, hh], preferred_element_type=F32).astype(BF16) + bias_ref[...]
        lg_ref[hh] = lg
        parts = [lg[pack * g:pack * (g + 1)] for g in range(kc // pack)]
        while len(parts) > 1:
            parts = [jnp.maximum(parts[g], parts[g + 1]) for g in range(0, len(parts), 2)]
        mx = parts[0].astype(F32)
        m_old = m_ref[hh:hh + 1, :]
        m_new = jnp.maximum(m_old, jnp.max(mx, axis=0, keepdims=True))
        m_ref[hh:hh + 1, :] = m_new
        al_ref[hh:hh + 1, :] = jnp.exp2(m_old - m_new)

    def value_stage(c, hh):
        vt = vT_ref[0, hh // GROUP, c]
        m_b = jnp.broadcast_to(m_ref[hh:hh + 1, :], (pack, tq)).astype(BF16)
        p = jnp.exp2(lg_ref[hh].reshape(kc // pack, pack, tq) - m_b[None]).reshape(kc, tq)
        acc_ref[hh] = acc_ref[hh] * al_ref[hh:hh + 1, :] + jnp.dot(vt, p, preferred_element_type=F32)

    set_bias(0)
    for hh in range(N_HEADS):
        logits_stage(0, hh)

    def attn_step(c):
        set_bias(c)
        for hh in range(N_HEADS):
            value_stage(c - 1, hh)
            logits_stage(c, hh)

    def attn_group(j, carry):
        for u in range(ATTN_GROUP):
            attn_step(ATTN_GROUP * j + 1 + u)
        return carry

    def attn_single(c, carry):
        attn_step(c)
        return carry

    n_grp = (n_c - 1) // ATTN_GROUP
    lax.fori_loop(0, n_grp, attn_group, 0)
    lax.fori_loop(ATTN_GROUP * n_grp + 1, n_c, attn_single, 0)
    for hh in range(N_HEADS):
        value_stage(n_c - 1, hh)

    for hp in range(N_HEADS // 2):
        parts = []
        for hh in (2 * hp, 2 * hp + 1):
            a = acc_ref[hh]
            parts.append(a[0:HEAD_DIM] / a[HEAD_DIM:HEAD_DIM + 1])
        pair = jnp.concatenate(parts, axis=0)
        o_ref[0, :, LANES * hp:LANES * (hp + 1)] = pair.T.astype(BF16)


def _dsa_attention(qT, qiT, wiT, k4, ki, vT, bn, t_len, topk):
    tq = topk
    rb = 1024
    kern = functools.partial(_dsa_kernel, topk=topk, rb=rb)
    return pl.pallas_call(
        kern,
        grid=(bn, t_len // tq),
        in_specs=[pl.BlockSpec((1, N_HEADS, HEAD_DIM, tq), lambda b, i: (b, 0, 0, i)),
                  pl.BlockSpec((1, IDX_HEADS, IDX_DIM, tq), lambda b, i: (b, 0, 0, i)),
                  pl.BlockSpec((1, IDX_HEADS, tq), lambda b, i: (b, 0, i)),
                  pl.BlockSpec((1, N_KV_HEADS, t_len, HEAD_DIM), lambda b, i: (b, 0, 0, 0),
                               pipeline_mode=pl.Buffered(1)),
                  pl.BlockSpec((1, t_len, IDX_DIM), lambda b, i: (b, 0, 0), pipeline_mode=pl.Buffered(1)),
                  pl.BlockSpec((1, N_KV_HEADS, t_len // tq, V_AUG, tq), lambda b, i: (b, 0, 0, 0, 0),
                               pipeline_mode=pl.Buffered(1))],
        out_specs=pl.BlockSpec((1, tq, Q_COLS), lambda b, i: (b, i, 0)),
        out_shape=jax.ShapeDtypeStruct((bn, t_len, Q_COLS), BF16),
        scratch_shapes=[pltpu.VMEM((t_len, tq), F32),
                        pltpu.VMEM((t_len, tq), BF16),
                        pltpu.VMEM((N_HEADS, tq), F32),
                        pltpu.VMEM((N_HEADS, tq), F32),
                        pltpu.VMEM((N_HEADS, V_AUG, tq), F32),
                        pltpu.VMEM((tq, tq), BF16),
                        pltpu.VMEM((N_HEADS, tq, tq), BF16),
                        pltpu.VMEM((SUBLANES, tq), F32),
                        pltpu.VMEM((tq, tq), BF16)],
        compiler_params=_params(2),
        name="dsa_attention",
    )(qT, qiT, wiT, k4, ki.reshape(bn, t_len, IDX_DIM), vT)


def _conv_pre_kernel(x_ref, mod_ref, g1_ref, w_ref, cw_ref, o_ref, zbuf_ref, *, tpb):
    tm, d = x_ref.shape
    i = pl.program_id(0)

    @pl.when(i % tpb == 0)
    def _():
        zbuf_ref[0:SUBLANES, :] = jnp.zeros((SUBLANES, d), F32)

    x = x_ref[...]
    sh = mod_ref[0, 0:1, :]
    sc = mod_ref[0, 1:2, :]
    ms = jnp.mean(x * x, axis=-1, keepdims=True)
    h = (x * lax.rsqrt(ms + NORM_EPS) * g1_ref[...]) * (1.0 + sc) + sh
    proj = jnp.dot(h.astype(BF16), w_ref[...], preferred_element_type=F32)
    b_gate = proj[:, 0:d]
    z = proj[:, d:2 * d] * proj[:, 2 * d:3 * d]
    zbuf_ref[SUBLANES:SUBLANES + tm, :] = z
    z1 = zbuf_ref[SUBLANES - 1:SUBLANES - 1 + tm, :]
    z2 = zbuf_ref[SUBLANES - 2:SUBLANES - 2 + tm, :]
    zc = cw_ref[0:1, :] * z2 + cw_ref[1:2, :] * z1 + cw_ref[2:3, :] * z
    o_ref[...] = (b_gate * zc).astype(BF16)
    zbuf_ref[0:SUBLANES, :] = z[tm - SUBLANES:tm, :]


def _conv_pre(x2d, mod, g1, w_in, conv_w, t_len, tm):
    n, d = x2d.shape
    tpb = t_len // tm
    row = lambda i: (i, 0)
    return pl.pallas_call(
        functools.partial(_conv_pre_kernel, tpb=tpb),
        grid=(n // tm,),
        in_specs=[pl.BlockSpec((tm, d), row),
                  pl.BlockSpec((1, 6, d), lambda i: (i // tpb, 0, 0)),
                  _const_spec((1, d)),
                  _const_spec((d, 3 * d)),
                  _const_spec((CONV_WIDTH, d))],
        out_specs=pl.BlockSpec((tm, d), row),
        out_shape=jax.ShapeDtypeStruct((n, d), BF16),
        scratch_shapes=[pltpu.VMEM((tm + SUBLANES, d), F32)],
        compiler_params=_params(1),
        name="conv_pre",
    )(x2d, mod, g1.reshape(1, d), w_in.astype(BF16), conv_w.astype(F32))


def _post_kernel(x_ref, mix_ref, mod_ref, g2_ref, wo_ref, wg_ref, wu_ref, wd_ref, o_ref, *, th):
    gate1 = mod_ref[0, 2:3, :]
    sh2 = mod_ref[0, 3:4, :]
    sc2 = mod_ref[0, 4:5, :]
    gate2 = mod_ref[0, 5:6, :]
    y = jnp.dot(mix_ref[...], wo_ref[...], preferred_element_type=F32)
    x1 = x_ref[...] + gate1 * y
    ms = jnp.mean(x1 * x1, axis=-1, keepdims=True)
    h = ((x1 * lax.rsqrt(ms + NORM_EPS) * g2_ref[...]) * (1.0 + sc2) + sh2).astype(BF16)
    hidden = wg_ref.shape[1]
    acc = jnp.zeros(x1.shape, F32)
    for j in range(hidden // th):
        gt = jnp.dot(h, wg_ref[:, th * j:th * (j + 1)], preferred_element_type=F32)
        up = jnp.dot(h, wu_ref[:, th * j:th * (j + 1)], preferred_element_type=F32)
        a = (gt * jax.nn.sigmoid(gt)) * up
        acc = acc + jnp.dot(a.astype(BF16), wd_ref[th * j:th * (j + 1), :], preferred_element_type=F32)
    o_ref[...] = x1 + gate2 * acc


def _post(x2d, mix, mod, g2, w_out, w_gate, w_up, w_down, t_len, tm):
    n, d = x2d.shape
    hidden = w_gate.shape[1]
    tpb = t_len // tm
    row = lambda i: (i, 0)
    return pl.pallas_call(
        functools.partial(_post_kernel, th=256),
        grid=(n // tm,),
        in_specs=[pl.BlockSpec((tm, d), row),
                  pl.BlockSpec((tm, d), row),
                  pl.BlockSpec((1, 6, d), lambda i: (i // tpb, 0, 0)),
                  _const_spec((1, d)),
                  _const_spec((d, d)),
                  _const_spec((d, hidden)),
                  _const_spec((d, hidden)),
                  _const_spec((hidden, d))],
        out_specs=pl.BlockSpec((tm, d), row),
        out_shape=jax.ShapeDtypeStruct((n, d), F32),
        compiler_params=_params(1),
        name="mixer_out_ffn",
    )(x2d, mix, mod, g2.reshape(1, d), w_out.astype(BF16), w_gate.astype(BF16), w_up.astype(BF16),
      w_down.astype(BF16))


def kernel(x, c, positions, ada_w, ada_b, norm1_g, norm2_g, attn_w_in, attn_q_norm_g, attn_k_norm_g,
           idx_k_ln_g, idx_k_ln_b, attn_w_out, conv_w_in, conv_w, conv_w_out, ffn_w_gate, ffn_w_up,
           ffn_w_down):
    bn, t_len, d = x.shape
    depth = ada_w.shape[0]
    topk = min(TOPK_MAX, t_len // 4)
    tm_pre = 512
    tm_conv = 1024
    tm_post = 1024
    assert topk == TOPK_MAX and t_len % tm_pre == 0 and t_len % tm_post == 0 and tm_pre % topk == 0
    n = bn * t_len
    mod = _modulation(c, ada_w, ada_b)
    x2d = x.reshape(n, d)
    for i in range(depth):
        j = i // 2
        if i % 2 == 0:
            qT, qiT, wiT, vT, k4, ki = _attn_pre(
                x2d, mod[i], norm1_g[i], attn_w_in[j], attn_q_norm_g[j], attn_k_norm_g[j],
                idx_k_ln_g[j], idx_k_ln_b[j], positions, bn, t_len, tm_pre, topk)
            mix = _dsa_attention(qT, qiT, wiT, k4, ki, vT, bn, t_len, topk).reshape(n, d)
            w_mix_out = attn_w_out[j]
        else:
            mix = _conv_pre(x2d, mod[i], norm1_g[i], conv_w_in[j], conv_w[j], t_len, tm_conv)
            w_mix_out = conv_w_out[j]
        x2d = _post(x2d, mix, mod[i], norm2_g[i], w_mix_out, ffn_w_gate[i], ffn_w_up[i], ffn_w_down[i],
                    t_len, tm_post)
    return x2d.reshape(bn, t_len, d)
```

```python
import functools
import math

import numpy as np
import jax
import jax.numpy as jnp
from jax import lax
from jax.experimental import pallas as pl
from jax.experimental.pallas import tpu as pltpu

F32 = jnp.float32
BF16 = jnp.bfloat16
I32 = jnp.int32

N_HEADS = 16
HEAD_DIM = 64
N_KV_HEADS = 4
GROUP = N_HEADS // N_KV_HEADS
IDX_HEADS = 8
IDX_DIM = 64
TOPK_MAX = 256
ROPE_THETA = 10000.0
CONV_WIDTH = 3
NORM_EPS = 1e-6
LOG2E = math.log2(math.e)

LANES = 128
SUBLANES = 8
BF16_ROWS = 2 * SUBLANES
MXU_DIM = 256
VMEM_LIMIT = 56 * 1024 * 1024

Q_COLS = N_HEADS * HEAD_DIM
KV_COLS = N_KV_HEADS * HEAD_DIM
QI_COLS = IDX_HEADS * IDX_DIM
V_AUG = HEAD_DIM + BF16_ROWS
SCORE_GROUP = 4
ATTN_GROUP = 4
COUNT_ROWS = 1024
TM_PRE = 512
TM_ROWS = 1024
FFN_CHUNK = MXU_DIM

INT_MIN = np.int32(-2**31)
KEY_NEG_INF = np.int32(np.uint32(0x807FFFFF).astype(np.int64) - 2**32)
KEY_HI_NEG_INF = np.int32(np.uint32(0x807F0000).astype(np.int64) - 2**32)


def _const_spec(shape):
    nd = len(shape)
    return pl.BlockSpec(shape, lambda *_: (0,) * nd, pipeline_mode=pl.Buffered(1))


def _params(n_axes):
    return pltpu.CompilerParams(dimension_semantics=("arbitrary",) * n_axes,
                                vmem_limit_bytes=VMEM_LIMIT)


def _mod_kernel(c_ref, w_ref, b_ref, o_ref):
    c = c_ref[...]
    ca = c * jax.nn.sigmoid(c)
    o_ref[0] = jnp.dot(ca.astype(BF16), w_ref[0].astype(BF16), preferred_element_type=F32) + b_ref[0]


def _modulation(c, ada_w, ada_b):
    depth, d, six_d = ada_w.shape
    bn = c.shape[0]
    rows = BF16_ROWS
    assert bn <= rows
    cp = jnp.zeros((rows, d), F32).at[:bn].set(c)
    tn = six_d // 4
    out = pl.pallas_call(
        _mod_kernel,
        grid=(depth, six_d // tn),
        in_specs=[pl.BlockSpec((rows, d), lambda i, j: (0, 0)),
                  pl.BlockSpec((1, d, tn), lambda i, j: (i, 0, j)),
                  pl.BlockSpec((1, 1, tn), lambda i, j: (i, 0, j))],
        out_specs=pl.BlockSpec((1, rows, tn), lambda i, j: (i, 0, j)),
        out_shape=jax.ShapeDtypeStruct((depth, rows, six_d), F32),
        compiler_params=_params(2),
        name="adaln_mod",
    )(cp, ada_w, ada_b.reshape(depth, 1, six_d))
    return out[:, :bn].reshape(depth, bn, 6, d)


def _attn_pre_kernel(x_ref, mod_ref, g1_ref, wt_ref, posr_ref, invft_ref, gq_ref, gk_ref, lng_ref, lnb_ref,
                     qT_ref, qiT_ref, wiT_ref, vT_ref, k4_ref, ki_ref):
    tm = x_ref.shape[0]
    x = x_ref[...]
    sh = mod_ref[0, 0:1, :]
    sc = mod_ref[0, 1:2, :]
    ms = jnp.mean(x * x, axis=-1, keepdims=True)
    h = (x * lax.rsqrt(ms + NORM_EPS) * g1_ref[...]) * (1.0 + sc) + sh
    hb = h.astype(BF16)

    pT = lax.dot_general(wt_ref[...], hb, (((1,), (1,)), ((), ())), preferred_element_type=F32)
    angT = invft_ref[...] * posr_ref[0].astype(F32)
    cT = jnp.cos(angT)
    sT = jnp.sin(angT)
    half = HEAD_DIM // 2

    def rope_t(y):
        x1 = y[:half]
        x2 = y[half:]
        return jnp.concatenate([x1 * cT - x2 * sT, x2 * cT + x1 * sT], axis=0)

    gq = gq_ref[...]
    for hh in range(N_HEADS):
        xq = pT[HEAD_DIM * hh:HEAD_DIM * (hh + 1)]
        inv = lax.rsqrt(jnp.mean(xq * xq, axis=0, keepdims=True) + NORM_EPS)
        y = xq * inv * gq
        qT_ref[0, hh] = (rope_t(y) * (HEAD_DIM ** -0.5 * LOG2E)).astype(BF16)
    o0 = Q_COLS
    gk = gk_ref[...]
    for j in range(N_KV_HEADS // 2):
        pair = []
        for g in (2 * j, 2 * j + 1):
            xk = pT[o0 + HEAD_DIM * g:o0 + HEAD_DIM * (g + 1)]
            inv = lax.rsqrt(jnp.mean(xk * xk, axis=0, keepdims=True) + NORM_EPS)
            pair.append(rope_t(xk * inv * gk))
        kr = jnp.concatenate(pair, axis=0).T
        k4_ref[0, 2 * j] = kr[:, 0:HEAD_DIM].astype(BF16)
        k4_ref[0, 2 * j + 1] = kr[:, HEAD_DIM:LANES].astype(BF16)
    o0 += KV_COLS
    vc = vT_ref.shape[-1]
    ones = jnp.ones((V_AUG - HEAD_DIM, vc), BF16)
    for g in range(N_KV_HEADS):
        vg = pT[o0 + HEAD_DIM * g:o0 + HEAD_DIM * (g + 1)].astype(BF16)
        for u in range(tm // vc):
            vT_ref[0, g, u, 0:HEAD_DIM, :] = vg[:, vc * u:vc * (u + 1)]
            vT_ref[0, g, u, HEAD_DIM:V_AUG, :] = ones
    o0 += KV_COLS
    for hh in range(IDX_HEADS):
        qiT_ref[0, hh] = rope_t(pT[o0 + IDX_DIM * hh:o0 + IDX_DIM * (hh + 1)]).astype(BF16)
    o0 += QI_COLS
    t = pT[o0:o0 + IDX_DIM]
    mu = jnp.mean(t, axis=0, keepdims=True)
    dlt = t - mu
    var = jnp.mean(dlt * dlt, axis=0, keepdims=True)
    kin = rope_t(dlt * lax.rsqrt(var + NORM_EPS) * lng_ref[...] + lnb_ref[...])
    kin = jnp.concatenate([kin, jnp.zeros((LANES - IDX_DIM, tm), F32)], axis=0).T
    ki_ref[...] = kin[:, 0:IDX_DIM].astype(BF16)
    o0 += IDX_DIM
    wiT_ref[0] = pT[o0:o0 + IDX_HEADS] * (IDX_HEADS ** -0.5 * IDX_DIM ** -0.5)


def _attn_pre(x2d, mod, g1, w_in, gq, gk, lng, lnb, positions, bn, t_len, tm, vc):
    n, d = x2d.shape
    tpb = t_len // tm
    half = HEAD_DIM // 2
    ma = w_in.shape[1]
    ma_pad = -(-ma // BF16_ROWS) * BF16_ROWS
    wt = jnp.concatenate([w_in, jnp.zeros((d, ma_pad - ma), F32)], axis=1).T.astype(BF16)
    inv_freq = ROPE_THETA ** (-(jnp.arange(half, dtype=F32) * 2.0 / HEAD_DIM))
    invft = jnp.broadcast_to(inv_freq[:, None], (half, tm))
    col = lambda v: jnp.broadcast_to(v.astype(F32)[:, None], (v.shape[0], tm))
    posr = positions.reshape(bn, 1, t_len)

    row = lambda i: (i, 0)
    outs = pl.pallas_call(
        _attn_pre_kernel,
        grid=(n // tm,),
        in_specs=[pl.BlockSpec((tm, d), row),
                  pl.BlockSpec((1, 6, d), lambda i: (i // tpb, 0, 0)),
                  _const_spec((1, d)),
                  _const_spec((ma_pad, d)),
                  pl.BlockSpec((1, 1, tm), lambda i: (i // tpb, 0, i % tpb)),
                  _const_spec((half, tm)),
                  _const_spec((HEAD_DIM, tm)),
                  _const_spec((HEAD_DIM, tm)),
                  _const_spec((IDX_DIM, tm)),
                  _const_spec((IDX_DIM, tm))],
        out_specs=[pl.BlockSpec((1, N_HEADS, HEAD_DIM, tm), lambda i: (i // tpb, 0, 0, i % tpb)),
                   pl.BlockSpec((1, IDX_HEADS, IDX_DIM, tm), lambda i: (i // tpb, 0, 0, i % tpb)),
                   pl.BlockSpec((1, IDX_HEADS, tm), lambda i: (i // tpb, 0, i % tpb)),
                   pl.BlockSpec((1, N_KV_HEADS, tm // vc, V_AUG, vc), lambda i: (i // tpb, 0, i % tpb, 0, 0)),
                   pl.BlockSpec((1, N_KV_HEADS, tm, HEAD_DIM), lambda i: (i // tpb, 0, i % tpb, 0)),
                   pl.BlockSpec((tm, IDX_DIM), row)],
        out_shape=[jax.ShapeDtypeStruct((bn, N_HEADS, HEAD_DIM, t_len), BF16),
                   jax.ShapeDtypeStruct((bn, IDX_HEADS, IDX_DIM, t_len), BF16),
                   jax.ShapeDtypeStruct((bn, IDX_HEADS, t_len), F32),
                   jax.ShapeDtypeStruct((bn, N_KV_HEADS, t_len // vc, V_AUG, vc), BF16),
                   jax.ShapeDtypeStruct((bn, N_KV_HEADS, t_len, HEAD_DIM), BF16),
                   jax.ShapeDtypeStruct((n, IDX_DIM), BF16)],
        compiler_params=_params(1),
        name="attn_pre",
    )(x2d, mod, g1.reshape(1, d), wt, posr, invft, col(gq), col(gk), col(lng), col(lnb))
    return outs


def _dsa_kernel(qT_ref, qiT_ref, wiT_ref, k4_ref, ki_ref, vT_ref, o_ref,
                st_ref, hi_ref, m_ref, al_ref, acc_ref, bias_ref, lg_ref, left_ref, ltri_ref, *, topk, rb):
    tq = o_ref.shape[1]
    kc = tq
    i = pl.program_id(1)
    n_c = i + 1
    q0 = i * tq

    tcol = q0 + lax.broadcasted_iota(I32, (kc, tq), 1)
    srow0 = lax.broadcasted_iota(I32, (kc, tq), 0)
    wi = wiT_ref[0]
    qi_all = jnp.concatenate([qiT_ref[0, hh] for hh in range(IDX_HEADS)], axis=1)

    def score_chunk(c):
        r0 = pl.multiple_of(c * kc, kc)
        ki = ki_ref[0, pl.ds(r0, kc), :]
        s_all = jnp.dot(ki, qi_all, preferred_element_type=F32)
        acc = jnp.zeros((kc, tq), F32)
        for hh in range(IDX_HEADS):
            acc = acc + jnp.maximum(s_all[:, tq * hh:tq * (hh + 1)], 0.0) * wi[hh:hh + 1, :]
        causal = (srow0 + r0) <= tcol
        sc = jnp.where(causal, acc, -jnp.inf)
        st_ref[pl.ds(r0, kc), :] = sc
        top = lax.bitcast_convert_type(sc, I32) & np.int32(-65536)
        hi_ref[pl.ds(r0, kc), :] = lax.bitcast_convert_type(top, F32).astype(BF16)

    def score_group(j, carry):
        for u in range(SCORE_GROUP):
            score_chunk(SCORE_GROUP * j + u)
        return carry

    def score_single(c, carry):
        score_chunk(c)
        return carry

    n_grp = n_c // SCORE_GROUP
    lax.fori_loop(0, n_grp, score_group, 0)
    lax.fori_loop(n_grp * SCORE_GROUP, n_c, score_single, 0)

    n_big = (n_c * kc) // rb
    n_small = (n_c * kc - n_big * rb) // kc
    n_par = 4
    pack = BF16_ROWS

    def all_sublanes(tot):
        for sh in (4, 2, 1):
            tot = tot + pltpu.roll(tot, sh, 0)
        return tot

    def two_level(block_fn, init):
        acc = lax.fori_loop(0, n_big, lambda r, a: block_fn(pl.multiple_of(r * rb, rb), rb, a), init)
        base = n_big * rb
        return lax.fori_loop(0, n_small, lambda r, a: block_fn(pl.multiple_of(base + r * kc, kc), kc, a), acc)

    def count(pred):
        def block(r0, rows, acc):
            blk = st_ref[pl.ds(r0, rows), :].reshape(rows // SUBLANES, SUBLANES, tq)
            hit = pred(blk).astype(I32).reshape(rows // (SUBLANES * n_par), n_par, SUBLANES, tq)
            return acc + jnp.sum(hit, axis=0)
        acc = two_level(block, jnp.zeros((n_par, SUBLANES, tq), I32))
        return all_sublanes(jnp.sum(acc, axis=0))

    def count_hi(cand16):
        def block(r0, rows, acc):
            blk = hi_ref[pl.ds(r0, rows), :].reshape(rows // pack, pack, tq)
            hit = jnp.where(blk >= cand16[None], jnp.ones((), BF16), jnp.zeros((), BF16))
            chains = [hit[k] for k in range(n_par)]
            for g in range(n_par, rows // pack):
                chains[g % n_par] = chains[g % n_par] + hit[g]
            return acc + ((chains[0] + chains[1]) + (chains[2] + chains[3])).astype(F32)
        acc = two_level(block, jnp.zeros((pack, tq), F32))
        return all_sublanes((acc[0:SUBLANES] + acc[SUBLANES:pack]).astype(I32))

    def key_to_float(u):
        ks = u ^ INT_MIN
        bits = jnp.where(ks < 0, ks ^ np.int32(0x7FFFFFFF), ks)
        return ks, bits

    def hi_body(it, carry):
        tau_u, cnt_tau = carry
        cand_u = tau_u | lax.shift_left(np.int32(1), np.int32(31) - it)
        ks, bits = key_to_float(cand_u)
        cand_top = lax.bitcast_convert_type(bits & np.int32(-65536), F32)
        cand16 = jnp.concatenate([cand_top, cand_top], axis=0).astype(BF16)
        cnt = count_hi(cand16)
        take = (cnt >= topk) | (ks < KEY_HI_NEG_INF)
        return jnp.where(take, cand_u, tau_u), jnp.where(take, cnt, cnt_tau)

    def lo_body(it, carry):
        tau_u, cnt_tau = carry
        cand_u = tau_u | lax.shift_left(np.int32(1), np.int32(15) - it)
        ks, bits = key_to_float(cand_u)
        cand_f = lax.bitcast_convert_type(bits, F32)
        cnt = count(lambda blk: blk >= cand_f[None])
        take = (cnt >= topk) | (ks < KEY_NEG_INF)
        return jnp.where(take, cand_u, tau_u), jnp.where(take, cnt, cnt_tau)

    def lo_stage(state, first, n):
        tau_u, cnt_tau, _ = state
        tau_u, cnt_tau = lax.fori_loop(first, first + n, lo_body, (tau_u, cnt_tau))
        tau8 = lax.bitcast_convert_type(key_to_float(tau_u)[1], F32)
        return tau_u, cnt_tau, count(lambda blk: blk > tau8[None])

    def pending(state):
        tau_u, cnt_tau, cnt_gt = state
        open_lane = ((cnt_tau != topk) & (cnt_gt >= topk)) | ((tau_u ^ INT_MIN) < KEY_NEG_INF)
        return jnp.max(open_lane.astype(I32)) > 0

    zero8 = jnp.zeros((SUBLANES, tq), I32)
    tau_u, cnt_tau = lax.fori_loop(0, 16, hi_body, (zero8, zero8))
    state = lo_stage((tau_u, cnt_tau, zero8), 0, 10)
    state = lax.cond(pending(state), lambda s: lo_stage(s, 10, 2), lambda s: s, state)
    state = lax.cond(pending(state), lambda s: lo_stage(s, 12, 4), lambda s: s, state)
    tau_u, _, cnt_gt = state
    tau8 = lax.bitcast_convert_type(key_to_float(tau_u)[1], F32)
    left_ref[...] = jnp.where(tau8 == -jnp.inf, 0.0, (topk - cnt_gt).astype(F32))
    tau1 = tau8[0:1]
    ri = lax.broadcasted_iota(I32, (kc, kc), 0)
    ci = lax.broadcasted_iota(I32, (kc, kc), 1)
    ltri_ref[...] = jnp.where(ci <= ri, 1.0, 0.0).astype(BF16)

    m_ref[...] = jnp.full(m_ref.shape, -(2.0 ** 100), F32)
    acc_ref[...] = jnp.zeros(acc_ref.shape, F32)

    def set_bias(c):
        r0 = pl.multiple_of(c * kc, kc)
        sc = st_ref[pl.ds(r0, kc), :]
        tie = sc == tau1
        seen = jnp.dot(ltri_ref[...], jnp.where(tie, 1.0, 0.0).astype(BF16), preferred_element_type=F32)
        left = left_ref[0:1, :]
        mask = (sc > tau1) | (tie & (seen <= left))
        bias_ref[...] = jnp.where(mask, 0.0, -jnp.inf).astype(BF16)
        left_ref[...] = jnp.broadcast_to(left - seen[kc - 1:kc, :], left_ref.shape)

    def logits_stage(c, hh):
        r0 = pl.multiple_of(c * kc, kc)
        kg = k4_ref[0, hh // GROUP, pl.ds(r0, kc), :]
        lg = jnp.dot(kg, qT_ref[0, hh], preferred_element_type=F32).astype(BF16) + bias_ref[...]
        lg_ref[hh] = lg
        parts = [lg[pack * g:pack * (g + 1)] for g in range(kc // pack)]
        while len(parts) > 1:
            parts = [jnp.maximum(parts[g], parts[g + 1]) for g in range(0, len(parts), 2)]
        mx = parts[0].astype(F32)
        m_old = m_ref[hh:hh + 1, :]
        m_new = jnp.maximum(m_old, jnp.max(mx, axis=0, keepdims=True))
        m_ref[hh:hh + 1, :] = m_new
        al_ref[hh:hh + 1, :] = jnp.exp2(m_old - m_new)

    def value_stage(c, hh):
        vt = vT_ref[0, hh // GROUP, c]
        m_b = jnp.broadcast_to(m_ref[hh:hh + 1, :], (pack, tq)).astype(BF16)
        p = jnp.exp2(lg_ref[hh].reshape(kc // pack, pack, tq) - m_b[None]).reshape(kc, tq)
        acc_ref[hh] = acc_ref[hh] * al_ref[hh:hh + 1, :] + jnp.dot(vt, p, preferred_element_type=F32)

    set_bias(0)
    for hh in range(N_HEADS):
        logits_stage(0, hh)

    def attn_step(c):
        set_bias(c)
        for hh in range(N_HEADS):
            value_stage(c - 1, hh)
            logits_stage(c, hh)

    def attn_group(j, carry):
        for u in range(ATTN_GROUP):
            attn_step(ATTN_GROUP * j + 1 + u)
        return carry

    def attn_single(c, carry):
        attn_step(c)
        return carry

    n_grp = (n_c - 1) // ATTN_GROUP
    lax.fori_loop(0, n_grp, attn_group, 0)
    lax.fori_loop(ATTN_GROUP * n_grp + 1, n_c, attn_single, 0)
    for hh in range(N_HEADS):
        value_stage(n_c - 1, hh)

    for hp in range(N_HEADS // 2):
        parts = []
        for hh in (2 * hp, 2 * hp + 1):
            a = acc_ref[hh]
            parts.append(a[0:HEAD_DIM] / a[HEAD_DIM:HEAD_DIM + 1])
        pair = jnp.concatenate(parts, axis=0)
        o_ref[0, :, LANES * hp:LANES * (hp + 1)] = pair.T.astype(BF16)


def _dsa_attention(qT, qiT, wiT, k4, ki, vT, bn, t_len, topk):
    tq = topk
    kern = functools.partial(_dsa_kernel, topk=topk, rb=COUNT_ROWS)
    return pl.pallas_call(
        kern,
        grid=(bn, t_len // tq),
        in_specs=[pl.BlockSpec((1, N_HEADS, HEAD_DIM, tq), lambda b, i: (b, 0, 0, i)),
                  pl.BlockSpec((1, IDX_HEADS, IDX_DIM, tq), lambda b, i: (b, 0, 0, i)),
                  pl.BlockSpec((1, IDX_HEADS, tq), lambda b, i: (b, 0, i)),
                  pl.BlockSpec((1, N_KV_HEADS, t_len, HEAD_DIM), lambda b, i: (b, 0, 0, 0),
                               pipeline_mode=pl.Buffered(1)),
                  pl.BlockSpec((1, t_len, IDX_DIM), lambda b, i: (b, 0, 0), pipeline_mode=pl.Buffered(1)),
                  pl.BlockSpec((1, N_KV_HEADS, t_len // tq, V_AUG, tq), lambda b, i: (b, 0, 0, 0, 0),
                               pipeline_mode=pl.Buffered(1))],
        out_specs=pl.BlockSpec((1, tq, Q_COLS), lambda b, i: (b, i, 0)),
        out_shape=jax.ShapeDtypeStruct((bn, t_len, Q_COLS), BF16),
        scratch_shapes=[pltpu.VMEM((t_len, tq), F32),
                        pltpu.VMEM((t_len, tq), BF16),
                        pltpu.VMEM((N_HEADS, tq), F32),
                        pltpu.VMEM((N_HEADS, tq), F32),
                        pltpu.VMEM((N_HEADS, V_AUG, tq), F32),
                        pltpu.VMEM((tq, tq), BF16),
                        pltpu.VMEM((N_HEADS, tq, tq), BF16),
                        pltpu.VMEM((SUBLANES, tq), F32),
                        pltpu.VMEM((tq, tq), BF16)],
        compiler_params=_params(2),
        name="dsa_attention",
    )(qT, qiT, wiT, k4, ki.reshape(bn, t_len, IDX_DIM), vT)


def _conv_pre_kernel(x_ref, mod_ref, g1_ref, w_ref, cw_ref, o_ref, zbuf_ref, *, tpb):
    tm, d = x_ref.shape
    i = pl.program_id(0)

    @pl.when(i % tpb == 0)
    def _():
        zbuf_ref[0:SUBLANES, :] = jnp.zeros((SUBLANES, d), F32)

    x = x_ref[...]
    sh = mod_ref[0, 0:1, :]
    sc = mod_ref[0, 1:2, :]
    ms = jnp.mean(x * x, axis=-1, keepdims=True)
    h = (x * lax.rsqrt(ms + NORM_EPS) * g1_ref[...]) * (1.0 + sc) + sh
    proj = jnp.dot(h.astype(BF16), w_ref[...], preferred_element_type=F32)
    b_gate = proj[:, 0:d]
    z = proj[:, d:2 * d] * proj[:, 2 * d:3 * d]
    zbuf_ref[SUBLANES:SUBLANES + tm, :] = z
    z1 = zbuf_ref[SUBLANES - 1:SUBLANES - 1 + tm, :]
    z2 = zbuf_ref[SUBLANES - 2:SUBLANES - 2 + tm, :]
    zc = cw_ref[0:1, :] * z2 + cw_ref[1:2, :] * z1 + cw_ref[2:3, :] * z
    o_ref[...] = (b_gate * zc).astype(BF16)
    zbuf_ref[0:SUBLANES, :] = z[tm - SUBLANES:tm, :]


def _conv_pre(x2d, mod, g1, w_in, conv_w, t_len, tm):
    n, d = x2d.shape
    tpb = t_len // tm
    row = lambda i: (i, 0)
    return pl.pallas_call(
        functools.partial(_conv_pre_kernel, tpb=tpb),
        grid=(n // tm,),
        in_specs=[pl.BlockSpec((tm, d), row),
                  pl.BlockSpec((1, 6, d), lambda i: (i // tpb, 0, 0)),
                  _const_spec((1, d)),
                  _const_spec((d, 3 * d)),
                  _const_spec((CONV_WIDTH, d))],
        out_specs=pl.BlockSpec((tm, d), row),
        out_shape=jax.ShapeDtypeStruct((n, d), BF16),
        scratch_shapes=[pltpu.VMEM((tm + SUBLANES, d), F32)],
        compiler_params=_params(1),
        name="conv_pre",
    )(x2d, mod, g1.reshape(1, d), w_in.astype(BF16), conv_w.astype(F32))


def _post_kernel(x_ref, mix_ref, mod_ref, g2_ref, wo_ref, wg_ref, wu_ref, wd_ref, o_ref, *, th):
    gate1 = mod_ref[0, 2:3, :]
    sh2 = mod_ref[0, 3:4, :]
    sc2 = mod_ref[0, 4:5, :]
    gate2 = mod_ref[0, 5:6, :]
    y = jnp.dot(mix_ref[...], wo_ref[...], preferred_element_type=F32)
    x1 = x_ref[...] + gate1 * y
    ms = jnp.mean(x1 * x1, axis=-1, keepdims=True)
    h = ((x1 * lax.rsqrt(ms + NORM_EPS) * g2_ref[...]) * (1.0 + sc2) + sh2).astype(BF16)
    hidden = wg_ref.shape[1]
    acc = jnp.zeros(x1.shape, F32)
    for j in range(hidden // th):
        gt = jnp.dot(h, wg_ref[:, th * j:th * (j + 1)], preferred_element_type=F32)
        up = jnp.dot(h, wu_ref[:, th * j:th * (j + 1)], preferred_element_type=F32)
        a = (gt * jax.nn.sigmoid(gt)) * up
        acc = acc + jnp.dot(a.astype(BF16), wd_ref[th * j:th * (j + 1), :], preferred_element_type=F32)
    o_ref[...] = x1 + gate2 * acc


def _post(x2d, mix, mod, g2, w_out, w_gate, w_up, w_down, t_len, tm):
    n, d = x2d.shape
    hidden = w_gate.shape[1]
    tpb = t_len // tm
    row = lambda i: (i, 0)
    return pl.pallas_call(
        functools.partial(_post_kernel, th=FFN_CHUNK),
        grid=(n // tm,),
        in_specs=[pl.BlockSpec((tm, d), row),
                  pl.BlockSpec((tm, d), row),
                  pl.BlockSpec((1, 6, d), lambda i: (i // tpb, 0, 0)),
                  _const_spec((1, d)),
                  _const_spec((d, d)),
                  _const_spec((d, hidden)),
                  _const_spec((d, hidden)),
                  _const_spec((hidden, d))],
        out_specs=pl.BlockSpec((tm, d), row),
        out_shape=jax.ShapeDtypeStruct((n, d), F32),
        compiler_params=_params(1),
        name="mixer_out_ffn",
    )(x2d, mix, mod, g2.reshape(1, d), w_out.astype(BF16), w_gate.astype(BF16), w_up.astype(BF16),
      w_down.astype(BF16))


def kernel(x, c, positions, ada_w, ada_b, norm1_g, norm2_g, attn_w_in, attn_q_norm_g, attn_k_norm_g,
           idx_k_ln_g, idx_k_ln_b, attn_w_out, conv_w_in, conv_w, conv_w_out, ffn_w_gate, ffn_w_up,
           ffn_w_down):
    bn, t_len, d = x.shape
    depth = ada_w.shape[0]
    topk = min(TOPK_MAX, t_len // 4)
    assert topk == TOPK_MAX == MXU_DIM and d == Q_COLS
    assert t_len % TM_PRE == 0 and t_len % TM_ROWS == 0 and TM_PRE % topk == 0 and COUNT_ROWS % topk == 0
    n = bn * t_len
    mod = _modulation(c, ada_w, ada_b)
    x2d = x.reshape(n, d)
    for i in range(depth):
        j = i // 2
        if i % 2 == 0:
            qT, qiT, wiT, vT, k4, ki = _attn_pre(
                x2d, mod[i], norm1_g[i], attn_w_in[j], attn_q_norm_g[j], attn_k_norm_g[j],
                idx_k_ln_g[j], idx_k_ln_b[j], positions, bn, t_len, TM_PRE, topk)
            mix = _dsa_attention(qT, qiT, wiT, k4, ki, vT, bn, t_len, topk).reshape(n, d)
            w_mix_out = attn_w_out[j]
        else:
            mix = _conv_pre(x2d, mod[i], norm1_g[i], conv_w_in[j], conv_w[j], t_len, TM_ROWS)
            w_mix_out = conv_w_out[j]
        x2d = _post(x2d, mix, mod[i], norm2_g[i], w_mix_out, ffn_w_gate[i], ffn_w_up[i], ffn_w_down[i],
                    t_len, TM_ROWS)
    return x2d.reshape(bn, t_len, d)
```

```python
import functools
import math

import numpy as np
import jax
import jax.numpy as jnp
from jax import lax
from jax.experimental import pallas as pl
from jax.experimental.pallas import tpu as pltpu

F32 = jnp.float32
BF16 = jnp.bfloat16
I32 = jnp.int32

N_HEADS = 16
HEAD_DIM = 64
N_KV_HEADS = 4
GROUP = N_HEADS // N_KV_HEADS
IDX_HEADS = 8
IDX_DIM = 64
TOPK_MAX = 256
ROPE_THETA = 10000.0
CONV_WIDTH = 3
NORM_EPS = 1e-6
LOG2E = math.log2(math.e)

LANES = 128
SUBLANES = 8
BF16_ROWS = 2 * SUBLANES
MXU_DIM = 256
VMEM_LIMIT = 56 * 1024 * 1024

Q_COLS = N_HEADS * HEAD_DIM
KV_COLS = N_KV_HEADS * HEAD_DIM
QI_COLS = IDX_HEADS * IDX_DIM
V_AUG = HEAD_DIM + BF16_ROWS
SCORE_GROUP = 4
ATTN_GROUP = 4
COUNT_ROWS = 1024
TM_PRE = 512
TM_ROWS = 1024
FFN_CHUNK = MXU_DIM

INT_MIN = np.int32(-2**31)
KEY_NEG_INF = np.int32(np.uint32(0x807FFFFF).astype(np.int64) - 2**32)
KEY_HI_NEG_INF = np.int32(np.uint32(0x807F0000).astype(np.int64) - 2**32)


def _const_spec(shape):
    nd = len(shape)
    return pl.BlockSpec(shape, lambda *_: (0,) * nd, pipeline_mode=pl.Buffered(1))


def _params(n_axes):
    return pltpu.CompilerParams(dimension_semantics=("arbitrary",) * n_axes,
                                vmem_limit_bytes=VMEM_LIMIT)


def _mod_kernel(c_ref, w_ref, b_ref, o_ref):
    c = c_ref[...]
    ca = c * jax.nn.sigmoid(c)
    o_ref[0] = jnp.dot(ca.astype(BF16), w_ref[0].astype(BF16), preferred_element_type=F32) + b_ref[0]


def _modulation(c, ada_w, ada_b):
    depth, d, six_d = ada_w.shape
    bn = c.shape[0]
    rows = BF16_ROWS
    assert bn <= rows
    cp = jnp.zeros((rows, d), F32).at[:bn].set(c)
    tn = six_d // 4
    out = pl.pallas_call(
        _mod_kernel,
        grid=(depth, six_d // tn),
        in_specs=[pl.BlockSpec((rows, d), lambda i, j: (0, 0)),
                  pl.BlockSpec((1, d, tn), lambda i, j: (i, 0, j)),
                  pl.BlockSpec((1, 1, tn), lambda i, j: (i, 0, j))],
        out_specs=pl.BlockSpec((1, rows, tn), lambda i, j: (i, 0, j)),
        out_shape=jax.ShapeDtypeStruct((depth, rows, six_d), F32),
        compiler_params=_params(2),
        name="adaln_mod",
    )(cp, ada_w, ada_b.reshape(depth, 1, six_d))
    return out[:, :bn].reshape(depth, bn, 6, d)


def _attn_pre_kernel(x_ref, mod_ref, g1_ref, wt_ref, posr_ref, invft_ref, gq_ref, gk_ref, lng_ref, lnb_ref,
                     qT_ref, qiT_ref, wiT_ref, vT_ref, k4_ref, ki_ref):
    tm = x_ref.shape[0]
    x = x_ref[...]
    sh = mod_ref[0, 0:1, :]
    sc = mod_ref[0, 1:2, :]
    ms = jnp.mean(x * x, axis=-1, keepdims=True)
    h = (x * lax.rsqrt(ms + NORM_EPS) * g1_ref[...]) * (1.0 + sc) + sh
    hb = h.astype(BF16)

    def proj(first, rows):
        return lax.dot_general(wt_ref[first:first + rows, :], hb, (((1,), (1,)), ((), ())),
                               preferred_element_type=F32)

    angT = invft_ref[...] * posr_ref[0].astype(F32)
    cT = jnp.cos(angT)
    sT = jnp.sin(angT)
    half = HEAD_DIM // 2

    def rope_t(y):
        x1 = y[:half]
        x2 = y[half:]
        return jnp.concatenate([x1 * cT - x2 * sT, x2 * cT + x1 * sT], axis=0)

    gq = gq_ref[...]
    per_sec = MXU_DIM // HEAD_DIM
    for sec in range(N_HEADS // per_sec):
        pq = proj(MXU_DIM * sec, MXU_DIM)
        for hl in range(per_sec):
            xq = pq[HEAD_DIM * hl:HEAD_DIM * (hl + 1)]
            inv = lax.rsqrt(jnp.mean(xq * xq, axis=0, keepdims=True) + NORM_EPS)
            y = xq * inv * gq
            qT_ref[0, per_sec * sec + hl] = (rope_t(y) * (HEAD_DIM ** -0.5 * LOG2E)).astype(BF16)
    o0 = Q_COLS
    gk = gk_ref[...]
    pk = proj(o0, KV_COLS)
    for j in range(N_KV_HEADS // 2):
        pair = []
        for g in (2 * j, 2 * j + 1):
            xk = pk[HEAD_DIM * g:HEAD_DIM * (g + 1)]
            inv = lax.rsqrt(jnp.mean(xk * xk, axis=0, keepdims=True) + NORM_EPS)
            pair.append(rope_t(xk * inv * gk))
        kr = jnp.concatenate(pair, axis=0).T
        k4_ref[0, 2 * j] = kr[:, 0:HEAD_DIM].astype(BF16)
        k4_ref[0, 2 * j + 1] = kr[:, HEAD_DIM:LANES].astype(BF16)
    o0 += KV_COLS
    vc = vT_ref.shape[-1]
    ones = jnp.ones((V_AUG - HEAD_DIM, vc), BF16)
    pv = proj(o0, KV_COLS).astype(BF16)
    for g in range(N_KV_HEADS):
        for u in range(tm // vc):
            vT_ref[0, g, u, 0:HEAD_DIM, :] = pv[HEAD_DIM * g:HEAD_DIM * (g + 1), vc * u:vc * (u + 1)]
            vT_ref[0, g, u, HEAD_DIM:V_AUG, :] = ones
    o0 += KV_COLS
    for sec in range(QI_COLS // MXU_DIM):
        pqi = proj(o0 + MXU_DIM * sec, MXU_DIM)
        for hl in range(per_sec):
            qiT_ref[0, per_sec * sec + hl] = rope_t(pqi[IDX_DIM * hl:IDX_DIM * (hl + 1)]).astype(BF16)
    o0 += QI_COLS
    tail = proj(o0, wt_ref.shape[0] - o0)
    t = tail[0:IDX_DIM]
    mu = jnp.mean(t, axis=0, keepdims=True)
    dlt = t - mu
    var = jnp.mean(dlt * dlt, axis=0, keepdims=True)
    kin = rope_t(dlt * lax.rsqrt(var + NORM_EPS) * lng_ref[...] + lnb_ref[...])
    kin = jnp.concatenate([kin, jnp.zeros((LANES - IDX_DIM, tm), F32)], axis=0).T
    ki_ref[...] = kin[:, 0:IDX_DIM].astype(BF16)
    wiT_ref[0] = tail[IDX_DIM:IDX_DIM + IDX_HEADS] * (IDX_HEADS ** -0.5 * IDX_DIM ** -0.5)


def _attn_pre(x2d, mod, g1, w_in, gq, gk, lng, lnb, positions, bn, t_len, tm, vc):
    n, d = x2d.shape
    tpb = t_len // tm
    half = HEAD_DIM // 2
    ma = w_in.shape[1]
    ma_pad = -(-ma // BF16_ROWS) * BF16_ROWS
    wt = jnp.concatenate([w_in, jnp.zeros((d, ma_pad - ma), F32)], axis=1).T.astype(BF16)
    inv_freq = ROPE_THETA ** (-(jnp.arange(half, dtype=F32) * 2.0 / HEAD_DIM))
    invft = jnp.broadcast_to(inv_freq[:, None], (half, tm))
    col = lambda v: jnp.broadcast_to(v.astype(F32)[:, None], (v.shape[0], tm))
    posr = positions.reshape(bn, 1, t_len)

    row = lambda i: (i, 0)
    outs = pl.pallas_call(
        _attn_pre_kernel,
        grid=(n // tm,),
        in_specs=[pl.BlockSpec((tm, d), row),
                  pl.BlockSpec((1, 6, d), lambda i: (i // tpb, 0, 0)),
                  _const_spec((1, d)),
                  _const_spec((ma_pad, d)),
                  pl.BlockSpec((1, 1, tm), lambda i: (i // tpb, 0, i % tpb)),
                  _const_spec((half, tm)),
                  _const_spec((HEAD_DIM, tm)),
                  _const_spec((HEAD_DIM, tm)),
                  _const_spec((IDX_DIM, tm)),
                  _const_spec((IDX_DIM, tm))],
        out_specs=[pl.BlockSpec((1, N_HEADS, HEAD_DIM, tm), lambda i: (i // tpb, 0, 0, i % tpb)),
                   pl.BlockSpec((1, IDX_HEADS, IDX_DIM, tm), lambda i: (i // tpb, 0, 0, i % tpb)),
                   pl.BlockSpec((1, IDX_HEADS, tm), lambda i: (i // tpb, 0, i % tpb)),
                   pl.BlockSpec((1, N_KV_HEADS, tm // vc, V_AUG, vc), lambda i: (i // tpb, 0, i % tpb, 0, 0)),
                   pl.BlockSpec((1, N_KV_HEADS, tm, HEAD_DIM), lambda i: (i // tpb, 0, i % tpb, 0)),
                   pl.BlockSpec((tm, IDX_DIM), row)],
        out_shape=[jax.ShapeDtypeStruct((bn, N_HEADS, HEAD_DIM, t_len), BF16),
                   jax.ShapeDtypeStruct((bn, IDX_HEADS, IDX_DIM, t_len), BF16),
                   jax.ShapeDtypeStruct((bn, IDX_HEADS, t_len), F32),
                   jax.ShapeDtypeStruct((bn, N_KV_HEADS, t_len // vc, V_AUG, vc), BF16),
                   jax.ShapeDtypeStruct((bn, N_KV_HEADS, t_len, HEAD_DIM), BF16),
                   jax.ShapeDtypeStruct((n, IDX_DIM), BF16)],
        compiler_params=_params(1),
        name="attn_pre",
    )(x2d, mod, g1.reshape(1, d), wt, posr, invft, col(gq), col(gk), col(lng), col(lnb))
    return outs


def _dsa_kernel(qT_ref, qiT_ref, wiT_ref, k4_ref, ki_ref, vT_ref, o_ref,
                st_ref, hi_ref, m_ref, al_ref, acc_ref, bias_ref, lg_ref, left_ref, ltri_ref, *, topk, rb):
    tq = o_ref.shape[1]
    kc = tq
    i = pl.program_id(1)
    n_c = i + 1
    q0 = i * tq

    tcol = q0 + lax.broadcasted_iota(I32, (kc, tq), 1)
    srow0 = lax.broadcasted_iota(I32, (kc, tq), 0)
    wi = wiT_ref[0]
    qi_all = jnp.concatenate([qiT_ref[0, hh] for hh in range(IDX_HEADS)], axis=1)

    def score_chunk(c):
        r0 = pl.multiple_of(c * kc, kc)
        ki = ki_ref[0, pl.ds(r0, kc), :]
        s_all = jnp.dot(ki, qi_all, preferred_element_type=F32)
        acc = jnp.zeros((kc, tq), F32)
        for hh in range(IDX_HEADS):
            acc = acc + jnp.maximum(s_all[:, tq * hh:tq * (hh + 1)], 0.0) * wi[hh:hh + 1, :]
        causal = (srow0 + r0) <= tcol
        sc = jnp.where(causal, acc, -jnp.inf)
        st_ref[pl.ds(r0, kc), :] = sc
        top = lax.bitcast_convert_type(sc, I32) & np.int32(-65536)
        hi_ref[pl.ds(r0, kc), :] = lax.bitcast_convert_type(top, F32).astype(BF16)

    def score_group(j, carry):
        for u in range(SCORE_GROUP):
            score_chunk(SCORE_GROUP * j + u)
        return carry

    def score_single(c, carry):
        score_chunk(c)
        return carry

    n_grp = n_c // SCORE_GROUP
    lax.fori_loop(0, n_grp, score_group, 0)
    lax.fori_loop(n_grp * SCORE_GROUP, n_c, score_single, 0)

    n_big = (n_c * kc) // rb
    n_small = (n_c * kc - n_big * rb) // kc
    n_par = 4
    pack = BF16_ROWS

    def all_sublanes(tot):
        for sh in (4, 2, 1):
            tot = tot + pltpu.roll(tot, sh, 0)
        return tot

    def two_level(block_fn, init):
        acc = lax.fori_loop(0, n_big, lambda r, a: block_fn(pl.multiple_of(r * rb, rb), rb, a), init)
        base = n_big * rb
        return lax.fori_loop(0, n_small, lambda r, a: block_fn(pl.multiple_of(base + r * kc, kc), kc, a), acc)

    def count(pred):
        def block(r0, rows, acc):
            blk = st_ref[pl.ds(r0, rows), :].reshape(rows // SUBLANES, SUBLANES, tq)
            hit = pred(blk).astype(I32).reshape(rows // (SUBLANES * n_par), n_par, SUBLANES, tq)
            return acc + jnp.sum(hit, axis=0)
        acc = two_level(block, jnp.zeros((n_par, SUBLANES, tq), I32))
        return all_sublanes(jnp.sum(acc, axis=0))

    def count_hi(cand16):
        def block(r0, rows, acc):
            blk = hi_ref[pl.ds(r0, rows), :].reshape(rows // pack, pack, tq)
            hit = jnp.where(blk >= cand16[None], jnp.ones((), BF16), jnp.zeros((), BF16))
            chains = [hit[k] for k in range(n_par)]
            for g in range(n_par, rows // pack):
                chains[g % n_par] = chains[g % n_par] + hit[g]
            return acc + ((chains[0] + chains[1]) + (chains[2] + chains[3])).astype(F32)
        acc = two_level(block, jnp.zeros((pack, tq), F32))
        return all_sublanes((acc[0:SUBLANES] + acc[SUBLANES:pack]).astype(I32))

    def key_to_float(u):
        ks = u ^ INT_MIN
        bits = jnp.where(ks < 0, ks ^ np.int32(0x7FFFFFFF), ks)
        return ks, bits

    def hi_body(it, carry):
        tau_u, cnt_tau = carry
        cand_u = tau_u | lax.shift_left(np.int32(1), np.int32(31) - it)
        ks, bits = key_to_float(cand_u)
        cand_top = lax.bitcast_convert_type(bits & np.int32(-65536), F32)
        cand16 = jnp.concatenate([cand_top, cand_top], axis=0).astype(BF16)
        cnt = count_hi(cand16)
        take = (cnt >= topk) | (ks < KEY_HI_NEG_INF)
        return jnp.where(take, cand_u, tau_u), jnp.where(take, cnt, cnt_tau)

    def lo_body(it, carry):
        tau_u, cnt_tau = carry
        cand_u = tau_u | lax.shift_left(np.int32(1), np.int32(15) - it)
        ks, bits = key_to_float(cand_u)
        cand_f = lax.bitcast_convert_type(bits, F32)
        cnt = count(lambda blk: blk >= cand_f[None])
        take = (cnt >= topk) | (ks < KEY_NEG_INF)
        return jnp.where(take, cand_u, tau_u), jnp.where(take, cnt, cnt_tau)

    def lo_stage(state, first, n):
        tau_u, cnt_tau, _ = state
        tau_u, cnt_tau = lax.fori_loop(first, first + n, lo_body, (tau_u, cnt_tau))
        tau8 = lax.bitcast_convert_type(key_to_float(tau_u)[1], F32)
        return tau_u, cnt_tau, count(lambda blk: blk > tau8[None])

    def pending(state):
        tau_u, cnt_tau, cnt_gt = state
        open_lane = ((cnt_tau != topk) & (cnt_gt >= topk)) | ((tau_u ^ INT_MIN) < KEY_NEG_INF)
        return jnp.max(open_lane.astype(I32)) > 0

    zero8 = jnp.zeros((SUBLANES, tq), I32)
    tau_u, cnt_tau = lax.fori_loop(0, 16, hi_body, (zero8, zero8))
    state = lo_stage((tau_u, cnt_tau, zero8), 0, 10)
    state = lax.cond(pending(state), lambda s: lo_stage(s, 10, 2), lambda s: s, state)
    state = lax.cond(pending(state), lambda s: lo_stage(s, 12, 4), lambda s: s, state)
    tau_u, _, cnt_gt = state
    tau8 = lax.bitcast_convert_type(key_to_float(tau_u)[1], F32)
    left_ref[...] = jnp.where(tau8 == -jnp.inf, 0.0, (topk - cnt_gt).astype(F32))
    tau1 = tau8[0:1]
    ri = lax.broadcasted_iota(I32, (kc, kc), 0)
    ci = lax.broadcasted_iota(I32, (kc, kc), 1)
    ltri_ref[...] = jnp.where(ci <= ri, 1.0, 0.0).astype(BF16)

    m_ref[...] = jnp.full(m_ref.shape, -(2.0 ** 100), F32)
    acc_ref[...] = jnp.zeros(acc_ref.shape, F32)

    def set_bias(c):
        r0 = pl.multiple_of(c * kc, kc)
        sc = st_ref[pl.ds(r0, kc), :]
        tie = sc == tau1
        seen = jnp.dot(ltri_ref[...], jnp.where(tie, 1.0, 0.0).astype(BF16), preferred_element_type=F32)
        left = left_ref[0:1, :]
        mask = (sc > tau1) | (tie & (seen <= left))
        bias_ref[...] = jnp.where(mask, 0.0, -jnp.inf).astype(BF16)
        left_ref[...] = jnp.broadcast_to(left - seen[kc - 1:kc, :], left_ref.shape)

    def logits_stage(c, hh):
        r0 = pl.multiple_of(c * kc, kc)
        kg = k4_ref[0, hh // GROUP, pl.ds(r0, kc), :]
        lg = jnp.dot(kg, qT_ref[0, hh], preferred_element_type=F32).astype(BF16) + bias_ref[...]
        lg_ref[hh] = lg
        parts = [lg[pack * g:pack * (g + 1)] for g in range(kc // pack)]
        while len(parts) > 1:
            parts = [jnp.maximum(parts[g], parts[g + 1]) for g in range(0, len(parts), 2)]
        mx = parts[0].astype(F32)
        m_old = m_ref[hh:hh + 1, :]
        m_new = jnp.maximum(m_old, jnp.max(mx, axis=0, keepdims=True))
        m_ref[hh:hh + 1, :] = m_new
        al_ref[hh:hh + 1, :] = jnp.exp2(m_old - m_new)

    def value_stage(c, hh):
        vt = vT_ref[0, hh // GROUP, c]
        m_b = jnp.broadcast_to(m_ref[hh:hh + 1, :], (pack, tq)).astype(BF16)
        p = jnp.exp2(lg_ref[hh].reshape(kc // pack, pack, tq) - m_b[None]).reshape(kc, tq)
        acc_ref[hh] = acc_ref[hh] * al_ref[hh:hh + 1, :] + jnp.dot(vt, p, preferred_element_type=F32)

    set_bias(0)
    for hh in range(N_HEADS):
        logits_stage(0, hh)

    def attn_step(c):
        set_bias(c)
        for hh in range(N_HEADS):
            value_stage(c - 1, hh)
            logits_stage(c, hh)

    def attn_group(j, carry):
        for u in range(ATTN_GROUP):
            attn_step(ATTN_GROUP * j + 1 + u)
        return carry

    def attn_single(c, carry):
        attn_step(c)
        return carry

    n_grp = (n_c - 1) // ATTN_GROUP
    lax.fori_loop(0, n_grp, attn_group, 0)
    lax.fori_loop(ATTN_GROUP * n_grp + 1, n_c, attn_single, 0)
    for hh in range(N_HEADS):
        value_stage(n_c - 1, hh)

    for hp in range(N_HEADS // 2):
        parts = []
        for hh in (2 * hp, 2 * hp + 1):
            a = acc_ref[hh]
            parts.append(a[0:HEAD_DIM] / a[HEAD_DIM:HEAD_DIM + 1])
        pair = jnp.concatenate(parts, axis=0)
        o_ref[0, :, LANES * hp:LANES * (hp + 1)] = pair.T.astype(BF16)


def _dsa_attention(qT, qiT, wiT, k4, ki, vT, bn, t_len, topk):
    tq = topk
    kern = functools.partial(_dsa_kernel, topk=topk, rb=COUNT_ROWS)
    return pl.pallas_call(
        kern,
        grid=(bn, t_len // tq),
        in_specs=[pl.BlockSpec((1, N_HEADS, HEAD_DIM, tq), lambda b, i: (b, 0, 0, i)),
                  pl.BlockSpec((1, IDX_HEADS, IDX_DIM, tq), lambda b, i: (b, 0, 0, i)),
                  pl.BlockSpec((1, IDX_HEADS, tq), lambda b, i: (b, 0, i)),
                  pl.BlockSpec((1, N_KV_HEADS, t_len, HEAD_DIM), lambda b, i: (b, 0, 0, 0),
                               pipeline_mode=pl.Buffered(1)),
                  pl.BlockSpec((1, t_len, IDX_DIM), lambda b, i: (b, 0, 0), pipeline_mode=pl.Buffered(1)),
                  pl.BlockSpec((1, N_KV_HEADS, t_len // tq, V_AUG, tq), lambda b, i: (b, 0, 0, 0, 0),
                               pipeline_mode=pl.Buffered(1))],
        out_specs=pl.BlockSpec((1, tq, Q_COLS), lambda b, i: (b, i, 0)),
        out_shape=jax.ShapeDtypeStruct((bn, t_len, Q_COLS), BF16),
        scratch_shapes=[pltpu.VMEM((t_len, tq), F32),
                        pltpu.VMEM((t_len, tq), BF16),
                        pltpu.VMEM((N_HEADS, tq), F32),
                        pltpu.VMEM((N_HEADS, tq), F32),
                        pltpu.VMEM((N_HEADS, V_AUG, tq), F32),
                        pltpu.VMEM((tq, tq), BF16),
                        pltpu.VMEM((N_HEADS, tq, tq), BF16),
                        pltpu.VMEM((SUBLANES, tq), F32),
                        pltpu.VMEM((tq, tq), BF16)],
        compiler_params=_params(2),
        name="dsa_attention",
    )(qT, qiT, wiT, k4, ki.reshape(bn, t_len, IDX_DIM), vT)


def _conv_pre_kernel(x_ref, mod_ref, g1_ref, w_ref, cw_ref, o_ref, zbuf_ref, *, tpb):
    tm, d = x_ref.shape
    i = pl.program_id(0)

    @pl.when(i % tpb == 0)
    def _():
        zbuf_ref[0:SUBLANES, :] = jnp.zeros((SUBLANES, d), F32)

    x = x_ref[...]
    sh = mod_ref[0, 0:1, :]
    sc = mod_ref[0, 1:2, :]
    ms = jnp.mean(x * x, axis=-1, keepdims=True)
    h = (x * lax.rsqrt(ms + NORM_EPS) * g1_ref[...]) * (1.0 + sc) + sh
    hb = h.astype(BF16)
    for j in range(d // MXU_DIM):
        cols = slice(MXU_DIM * j, MXU_DIM * (j + 1))
        part = [jnp.dot(hb, w_ref[:, s * d + MXU_DIM * j:s * d + MXU_DIM * (j + 1)], preferred_element_type=F32)
                for s in range(3)]
        z = part[1] * part[2]
        zbuf_ref[SUBLANES:SUBLANES + tm, cols] = z
        z1 = zbuf_ref[SUBLANES - 1:SUBLANES - 1 + tm, cols]
        z2 = zbuf_ref[SUBLANES - 2:SUBLANES - 2 + tm, cols]
        zc = cw_ref[0:1, cols] * z2 + cw_ref[1:2, cols] * z1 + cw_ref[2:3, cols] * z
        o_ref[:, cols] = (part[0] * zc).astype(BF16)
        zbuf_ref[0:SUBLANES, cols] = z[tm - SUBLANES:tm, :]


def _conv_pre(x2d, mod, g1, w_in, conv_w, t_len, tm):
    n, d = x2d.shape
    tpb = t_len // tm
    row = lambda i: (i, 0)
    return pl.pallas_call(
        functools.partial(_conv_pre_kernel, tpb=tpb),
        grid=(n // tm,),
        in_specs=[pl.BlockSpec((tm, d), row),
                  pl.BlockSpec((1, 6, d), lambda i: (i // tpb, 0, 0)),
                  _const_spec((1, d)),
                  _const_spec((d, 3 * d)),
                  _const_spec((CONV_WIDTH, d))],
        out_specs=pl.BlockSpec((tm, d), row),
        out_shape=jax.ShapeDtypeStruct((n, d), BF16),
        scratch_shapes=[pltpu.VMEM((tm + SUBLANES, d), F32)],
        compiler_params=_params(1),
        name="conv_pre",
    )(x2d, mod, g1.reshape(1, d), w_in.astype(BF16), conv_w.astype(F32))


def _post_kernel(x_ref, mix_ref, mod_ref, g2_ref, wo_ref, wg_ref, wu_ref, wd_ref, o_ref, *, th):
    gate1 = mod_ref[0, 2:3, :]
    sh2 = mod_ref[0, 3:4, :]
    sc2 = mod_ref[0, 4:5, :]
    gate2 = mod_ref[0, 5:6, :]
    y = jnp.dot(mix_ref[...], wo_ref[...], preferred_element_type=F32)
    x1 = x_ref[...] + gate1 * y
    ms = jnp.mean(x1 * x1, axis=-1, keepdims=True)
    h = ((x1 * lax.rsqrt(ms + NORM_EPS) * g2_ref[...]) * (1.0 + sc2) + sh2).astype(BF16)
    hidden = wg_ref.shape[1]
    acc = jnp.zeros(x1.shape, F32)
    for j in range(hidden // th):
        gt = jnp.dot(h, wg_ref[:, th * j:th * (j + 1)], preferred_element_type=F32)
        up = jnp.dot(h, wu_ref[:, th * j:th * (j + 1)], preferred_element_type=F32)
        a = (gt * jax.nn.sigmoid(gt)) * up
        acc = acc + jnp.dot(a.astype(BF16), wd_ref[th * j:th * (j + 1), :], preferred_element_type=F32)
    o_ref[...] = x1 + gate2 * acc


def _post(x2d, mix, mod, g2, w_out, w_gate, w_up, w_down, t_len, tm):
    n, d = x2d.shape
    hidden = w_gate.shape[1]
    tpb = t_len // tm
    row = lambda i: (i, 0)
    return pl.pallas_call(
        functools.partial(_post_kernel, th=FFN_CHUNK),
        grid=(n // tm,),
        in_specs=[pl.BlockSpec((tm, d), row),
                  pl.BlockSpec((tm, d), row),
                  pl.BlockSpec((1, 6, d), lambda i: (i // tpb, 0, 0)),
                  _const_spec((1, d)),
                  _const_spec((d, d)),
                  _const_spec((d, hidden)),
                  _const_spec((d, hidden)),
                  _const_spec((hidden, d))],
        out_specs=pl.BlockSpec((tm, d), row),
        out_shape=jax.ShapeDtypeStruct((n, d), F32),
        compiler_params=_params(1),
        name="mixer_out_ffn",
    )(x2d, mix, mod, g2.reshape(1, d), w_out.astype(BF16), w_gate.astype(BF16), w_up.astype(BF16),
      w_down.astype(BF16))


def kernel(x, c, positions, ada_w, ada_b, norm1_g, norm2_g, attn_w_in, attn_q_norm_g, attn_k_norm_g,
           idx_k_ln_g, idx_k_ln_b, attn_w_out, conv_w_in, conv_w, conv_w_out, ffn_w_gate, ffn_w_up,
           ffn_w_down):
    bn, t_len, d = x.shape
    depth = ada_w.shape[0]
    topk = min(TOPK_MAX, t_len // 4)
    assert topk == TOPK_MAX == MXU_DIM and d == Q_COLS
    assert t_len % TM_PRE == 0 and t_len % TM_ROWS == 0 and TM_PRE % topk == 0 and COUNT_ROWS % topk == 0
    n = bn * t_len
    mod = _modulation(c, ada_w, ada_b)
    x2d = x.reshape(n, d)
    for i in range(depth):
        j = i // 2
        if i % 2 == 0:
            qT, qiT, wiT, vT, k4, ki = _attn_pre(
                x2d, mod[i], norm1_g[i], attn_w_in[j], attn_q_norm_g[j], attn_k_norm_g[j],
                idx_k_ln_g[j], idx_k_ln_b[j], positions, bn, t_len, TM_PRE, topk)
            mix = _dsa_attention(qT, qiT, wiT, k4, ki, vT, bn, t_len, topk).reshape(n, d)
            w_mix_out = attn_w_out[j]
        else:
            mix = _conv_pre(x2d, mod[i], norm1_g[i], conv_w_in[j], conv_w[j], t_len, TM_ROWS)
            w_mix_out = conv_w_out[j]
        x2d = _post(x2d, mix, mod[i], norm2_g[i], w_mix_out, ffn_w_gate[i], ffn_w_up[i], ffn_w_down[i],
                    t_len, TM_ROWS)
    return x2d.reshape(bn, t_len, d)
```

```python
import functools
import math

import numpy as np
import jax
import jax.numpy as jnp
from jax import lax
from jax.experimental import pallas as pl
from jax.experimental.pallas import tpu as pltpu

F32 = jnp.float32
BF16 = jnp.bfloat16
I32 = jnp.int32

N_HEADS = 16
HEAD_DIM = 64
N_KV_HEADS = 4
GROUP = N_HEADS // N_KV_HEADS
IDX_HEADS = 8
IDX_DIM = 64
TOPK_MAX = 256
ROPE_THETA = 10000.0
CONV_WIDTH = 3
NORM_EPS = 1e-6
LOG2E = math.log2(math.e)

LANES = 128
SUBLANES = 8
BF16_ROWS = 2 * SUBLANES
MXU_DIM = 256
VMEM_LIMIT = 56 * 1024 * 1024

Q_COLS = N_HEADS * HEAD_DIM
KV_COLS = N_KV_HEADS * HEAD_DIM
QI_COLS = IDX_HEADS * IDX_DIM
V_AUG = HEAD_DIM + BF16_ROWS
SCORE_GROUP = 4
ATTN_GROUP = 4
COUNT_ROWS = 1024
TM_PRE = 512
TM_ROWS = 1024
FFN_CHUNK = MXU_DIM

INT_MIN = np.int32(-2**31)
KEY_NEG_INF = np.int32(np.uint32(0x807FFFFF).astype(np.int64) - 2**32)
KEY_HI_NEG_INF = np.int32(np.uint32(0x807F0000).astype(np.int64) - 2**32)


def _const_spec(shape):
    nd = len(shape)
    return pl.BlockSpec(shape, lambda *_: (0,) * nd, pipeline_mode=pl.Buffered(1))


def _params(n_axes):
    return pltpu.CompilerParams(dimension_semantics=("arbitrary",) * n_axes,
                                vmem_limit_bytes=VMEM_LIMIT)


def _mod_kernel(c_ref, w_ref, b_ref, o_ref):
    c = c_ref[...]
    ca = c * jax.nn.sigmoid(c)
    o_ref[0] = jnp.dot(ca.astype(BF16), w_ref[0].astype(BF16), preferred_element_type=F32) + b_ref[0]


def _modulation(c, ada_w, ada_b):
    depth, d, six_d = ada_w.shape
    bn = c.shape[0]
    rows = BF16_ROWS
    assert bn <= rows
    cp = jnp.zeros((rows, d), F32).at[:bn].set(c)
    tn = six_d // 4
    out = pl.pallas_call(
        _mod_kernel,
        grid=(depth, six_d // tn),
        in_specs=[pl.BlockSpec((rows, d), lambda i, j: (0, 0)),
                  pl.BlockSpec((1, d, tn), lambda i, j: (i, 0, j)),
                  pl.BlockSpec((1, 1, tn), lambda i, j: (i, 0, j))],
        out_specs=pl.BlockSpec((1, rows, tn), lambda i, j: (i, 0, j)),
        out_shape=jax.ShapeDtypeStruct((depth, rows, six_d), F32),
        compiler_params=_params(2),
        name="adaln_mod",
    )(cp, ada_w, ada_b.reshape(depth, 1, six_d))
    return out[:, :bn].reshape(depth, bn, 6, d)


def _attn_pre_kernel(x_ref, mod_ref, g1_ref, wt_ref, posr_ref, invft_ref, gq_ref, gk_ref, lng_ref, lnb_ref,
                     qT_ref, qiT_ref, wiT_ref, vT_ref, k4_ref, ki_ref):
    tm = x_ref.shape[0]
    x = x_ref[...]
    sh = mod_ref[0, 0:1, :]
    sc = mod_ref[0, 1:2, :]
    ms = jnp.mean(x * x, axis=-1, keepdims=True)
    h = (x * lax.rsqrt(ms + NORM_EPS) * g1_ref[...]) * (1.0 + sc) + sh
    hb = h.astype(BF16)

    def proj(first, rows):
        return lax.dot_general(wt_ref[first:first + rows, :], hb, (((1,), (1,)), ((), ())),
                               preferred_element_type=F32)

    angT = invft_ref[...] * posr_ref[0].astype(F32)
    cT = jnp.cos(angT)
    sT = jnp.sin(angT)
    half = HEAD_DIM // 2

    def rope_t(y):
        x1 = y[:half]
        x2 = y[half:]
        return jnp.concatenate([x1 * cT - x2 * sT, x2 * cT + x1 * sT], axis=0)

    gq = gq_ref[...]
    per_sec = MXU_DIM // HEAD_DIM
    for sec in range(N_HEADS // per_sec):
        pq = proj(MXU_DIM * sec, MXU_DIM)
        for hl in range(per_sec):
            xq = pq[HEAD_DIM * hl:HEAD_DIM * (hl + 1)]
            inv = lax.rsqrt(jnp.mean(xq * xq, axis=0, keepdims=True) + NORM_EPS)
            y = xq * inv * gq
            qT_ref[0, per_sec * sec + hl] = (rope_t(y) * (HEAD_DIM ** -0.5 * LOG2E)).astype(BF16)
    o0 = Q_COLS
    gk = gk_ref[...]
    pk = proj(o0, KV_COLS)
    for j in range(N_KV_HEADS // 2):
        pair = []
        for g in (2 * j, 2 * j + 1):
            xk = pk[HEAD_DIM * g:HEAD_DIM * (g + 1)]
            inv = lax.rsqrt(jnp.mean(xk * xk, axis=0, keepdims=True) + NORM_EPS)
            pair.append(rope_t(xk * inv * gk))
        kr = jnp.concatenate(pair, axis=0).T
        k4_ref[0, 2 * j] = kr[:, 0:HEAD_DIM].astype(BF16)
        k4_ref[0, 2 * j + 1] = kr[:, HEAD_DIM:LANES].astype(BF16)
    o0 += KV_COLS
    vc = vT_ref.shape[-1]
    ones = jnp.ones((V_AUG - HEAD_DIM, vc), BF16)
    pv = proj(o0, KV_COLS).astype(BF16)
    for g in range(N_KV_HEADS):
        for u in range(tm // vc):
            vT_ref[0, g, u, 0:HEAD_DIM, :] = pv[HEAD_DIM * g:HEAD_DIM * (g + 1), vc * u:vc * (u + 1)]
            vT_ref[0, g, u, HEAD_DIM:V_AUG, :] = ones
    o0 += KV_COLS
    for sec in range(QI_COLS // MXU_DIM):
        pqi = proj(o0 + MXU_DIM * sec, MXU_DIM)
        for hl in range(per_sec):
            qiT_ref[0, per_sec * sec + hl] = rope_t(pqi[IDX_DIM * hl:IDX_DIM * (hl + 1)]).astype(BF16)
    o0 += QI_COLS
    tail = proj(o0, wt_ref.shape[0] - o0)
    t = tail[0:IDX_DIM]
    mu = jnp.mean(t, axis=0, keepdims=True)
    dlt = t - mu
    var = jnp.mean(dlt * dlt, axis=0, keepdims=True)
    kin = rope_t(dlt * lax.rsqrt(var + NORM_EPS) * lng_ref[...] + lnb_ref[...])
    kin = jnp.concatenate([kin, jnp.zeros((LANES - IDX_DIM, tm), F32)], axis=0).T
    ki_ref[...] = kin[:, 0:IDX_DIM].astype(BF16)
    wiT_ref[0] = tail[IDX_DIM:IDX_DIM + IDX_HEADS] * (IDX_HEADS ** -0.5 * IDX_DIM ** -0.5)


def _attn_pre(x2d, mod, g1, w_in, gq, gk, lng, lnb, positions, bn, t_len, tm, vc):
    n, d = x2d.shape
    tpb = t_len // tm
    half = HEAD_DIM // 2
    ma = w_in.shape[1]
    ma_pad = -(-ma // BF16_ROWS) * BF16_ROWS
    wt = jnp.concatenate([w_in, jnp.zeros((d, ma_pad - ma), F32)], axis=1).T.astype(BF16)
    inv_freq = ROPE_THETA ** (-(jnp.arange(half, dtype=F32) * 2.0 / HEAD_DIM))
    invft = jnp.broadcast_to(inv_freq[:, None], (half, tm))
    col = lambda v: jnp.broadcast_to(v.astype(F32)[:, None], (v.shape[0], tm))
    posr = positions.reshape(bn, 1, t_len)

    row = lambda i: (i, 0)
    outs = pl.pallas_call(
        _attn_pre_kernel,
        grid=(n // tm,),
        in_specs=[pl.BlockSpec((tm, d), row),
                  pl.BlockSpec((1, 6, d), lambda i: (i // tpb, 0, 0)),
                  _const_spec((1, d)),
                  _const_spec((ma_pad, d)),
                  pl.BlockSpec((1, 1, tm), lambda i: (i // tpb, 0, i % tpb)),
                  _const_spec((half, tm)),
                  _const_spec((HEAD_DIM, tm)),
                  _const_spec((HEAD_DIM, tm)),
                  _const_spec((IDX_DIM, tm)),
                  _const_spec((IDX_DIM, tm))],
        out_specs=[pl.BlockSpec((1, N_HEADS, HEAD_DIM, tm), lambda i: (i // tpb, 0, 0, i % tpb)),
                   pl.BlockSpec((1, IDX_HEADS, IDX_DIM, tm), lambda i: (i // tpb, 0, 0, i % tpb)),
                   pl.BlockSpec((1, IDX_HEADS, tm), lambda i: (i // tpb, 0, i % tpb)),
                   pl.BlockSpec((1, N_KV_HEADS, tm // vc, V_AUG, vc), lambda i: (i // tpb, 0, i % tpb, 0, 0)),
                   pl.BlockSpec((1, N_KV_HEADS, tm, HEAD_DIM), lambda i: (i // tpb, 0, i % tpb, 0)),
                   pl.BlockSpec((tm, IDX_DIM), row)],
        out_shape=[jax.ShapeDtypeStruct((bn, N_HEADS, HEAD_DIM, t_len), BF16),
                   jax.ShapeDtypeStruct((bn, IDX_HEADS, IDX_DIM, t_len), BF16),
                   jax.ShapeDtypeStruct((bn, IDX_HEADS, t_len), F32),
                   jax.ShapeDtypeStruct((bn, N_KV_HEADS, t_len // vc, V_AUG, vc), BF16),
                   jax.ShapeDtypeStruct((bn, N_KV_HEADS, t_len, HEAD_DIM), BF16),
                   jax.ShapeDtypeStruct((n, IDX_DIM), BF16)],
        compiler_params=_params(1),
        name="attn_pre",
    )(x2d, mod, g1.reshape(1, d), wt, posr, invft, col(gq), col(gk), col(lng), col(lnb))
    return outs


def _dsa_kernel(qT_ref, qiT_ref, wiT_ref, k4_ref, ki_ref, vT_ref, o_ref,
                st_ref, hi_ref, m_ref, al_ref, acc_ref, bias_ref, lg_ref, left_ref, ltri_ref, *, topk, rb):
    tq = o_ref.shape[1]
    kc = tq
    i = pl.program_id(1)
    n_c = i + 1
    q0 = i * tq

    tcol = q0 + lax.broadcasted_iota(I32, (kc, tq), 1)
    srow0 = lax.broadcasted_iota(I32, (kc, tq), 0)
    wi = wiT_ref[0]
    qi_all = jnp.concatenate([qiT_ref[0, hh] for hh in range(IDX_HEADS)], axis=1)

    def score_chunk(c):
        r0 = pl.multiple_of(c * kc, kc)
        ki = ki_ref[0, pl.ds(r0, kc), :]
        s_all = jnp.dot(ki, qi_all, preferred_element_type=F32)
        acc = jnp.zeros((kc, tq), F32)
        for hh in range(IDX_HEADS):
            acc = acc + jnp.maximum(s_all[:, tq * hh:tq * (hh + 1)], 0.0) * wi[hh:hh + 1, :]
        causal = (srow0 + r0) <= tcol
        sc = jnp.where(causal, acc, -jnp.inf)
        st_ref[pl.ds(r0, kc), :] = sc
        top = lax.bitcast_convert_type(sc, I32) & np.int32(-65536)
        hi_ref[pl.ds(r0, kc), :] = lax.bitcast_convert_type(top, F32).astype(BF16)

    def score_group(j, carry):
        for u in range(SCORE_GROUP):
            score_chunk(SCORE_GROUP * j + u)
        return carry

    def score_single(c, carry):
        score_chunk(c)
        return carry

    n_grp = n_c // SCORE_GROUP
    lax.fori_loop(0, n_grp, score_group, 0)
    lax.fori_loop(n_grp * SCORE_GROUP, n_c, score_single, 0)

    n_big = (n_c * kc) // rb
    n_small = (n_c * kc - n_big * rb) // kc
    n_par = 4
    pack = BF16_ROWS

    def all_sublanes(tot):
        for sh in (4, 2, 1):
            tot = tot + pltpu.roll(tot, sh, 0)
        return tot

    def two_level(block_fn, init):
        acc = lax.fori_loop(0, n_big, lambda r, a: block_fn(pl.multiple_of(r * rb, rb), rb, a), init)
        base = n_big * rb
        return lax.fori_loop(0, n_small, lambda r, a: block_fn(pl.multiple_of(base + r * kc, kc), kc, a), acc)

    def count(pred):
        def block(r0, rows, acc):
            blk = st_ref[pl.ds(r0, rows), :].reshape(rows // SUBLANES, SUBLANES, tq)
            hit = pred(blk).astype(I32).reshape(rows // (SUBLANES * n_par), n_par, SUBLANES, tq)
            return acc + jnp.sum(hit, axis=0)
        acc = two_level(block, jnp.zeros((n_par, SUBLANES, tq), I32))
        return all_sublanes(jnp.sum(acc, axis=0))

    def count_hi(cand16):
        def block(r0, rows, acc):
            blk = hi_ref[pl.ds(r0, rows), :].reshape(rows // pack, pack, tq)
            hit = jnp.where(blk >= cand16[None], jnp.ones((), BF16), jnp.zeros((), BF16))
            chains = [hit[k] for k in range(n_par)]
            for g in range(n_par, rows // pack):
                chains[g % n_par] = chains[g % n_par] + hit[g]
            return acc + ((chains[0] + chains[1]) + (chains[2] + chains[3])).astype(F32)
        acc = two_level(block, jnp.zeros((pack, tq), F32))
        return all_sublanes((acc[0:SUBLANES] + acc[SUBLANES:pack]).astype(I32))

    def key_to_float(u):
        ks = u ^ INT_MIN
        bits = jnp.where(ks < 0, ks ^ np.int32(0x7FFFFFFF), ks)
        return ks, bits

    def hi_body(it, carry):
        tau_u, cnt_tau = carry
        cand_u = tau_u | lax.shift_left(np.int32(1), np.int32(31) - it)
        ks, bits = key_to_float(cand_u)
        cand_top = lax.bitcast_convert_type(bits & np.int32(-65536), F32)
        cand16 = jnp.concatenate([cand_top, cand_top], axis=0).astype(BF16)
        cnt = count_hi(cand16)
        take = (cnt >= topk) | (ks < KEY_HI_NEG_INF)
        return jnp.where(take, cand_u, tau_u), jnp.where(take, cnt, cnt_tau)

    def lo_body(it, carry):
        tau_u, cnt_tau = carry
        cand_u = tau_u | lax.shift_left(np.int32(1), np.int32(15) - it)
        ks, bits = key_to_float(cand_u)
        cand_f = lax.bitcast_convert_type(bits, F32)
        cnt = count(lambda blk: blk >= cand_f[None])
        take = (cnt >= topk) | (ks < KEY_NEG_INF)
        return jnp.where(take, cand_u, tau_u), jnp.where(take, cnt, cnt_tau)

    def lo_stage(state, first, n):
        tau_u, cnt_tau, _ = state
        tau_u, cnt_tau = lax.fori_loop(first, first + n, lo_body, (tau_u, cnt_tau))
        tau8 = lax.bitcast_convert_type(key_to_float(tau_u)[1], F32)
        return tau_u, cnt_tau, count(lambda blk: blk > tau8[None])

    def pending(state):
        tau_u, cnt_tau, cnt_gt = state
        open_lane = ((cnt_tau != topk) & (cnt_gt >= topk)) | ((tau_u ^ INT_MIN) < KEY_NEG_INF)
        return jnp.max(open_lane.astype(I32)) > 0

    zero8 = jnp.zeros((SUBLANES, tq), I32)
    tau_u, cnt_tau = lax.fori_loop(0, 16, hi_body, (zero8, zero8))
    state = lo_stage((tau_u, cnt_tau, zero8), 0, 10)
    state = lax.cond(pending(state), lambda s: lo_stage(s, 10, 2), lambda s: s, state)
    state = lax.cond(pending(state), lambda s: lo_stage(s, 12, 4), lambda s: s, state)
    tau_u, _, cnt_gt = state
    tau8 = lax.bitcast_convert_type(key_to_float(tau_u)[1], F32)
    left_ref[...] = jnp.where(tau8 == -jnp.inf, 0.0, (topk - cnt_gt).astype(F32))
    tau1 = tau8[0:1]
    ri = lax.broadcasted_iota(I32, (kc, kc), 0)
    ci = lax.broadcasted_iota(I32, (kc, kc), 1)
    ltri_ref[...] = jnp.where(ci <= ri, 1.0, 0.0).astype(BF16)

    m_ref[...] = jnp.full(m_ref.shape, -(2.0 ** 100), F32)
    acc_ref[...] = jnp.zeros(acc_ref.shape, F32)

    def set_bias(c):
        r0 = pl.multiple_of(c * kc, kc)
        sc = st_ref[pl.ds(r0, kc), :]
        tie = sc == tau1
        seen = jnp.dot(ltri_ref[...], jnp.where(tie, 1.0, 0.0).astype(BF16), preferred_element_type=F32)
        left = left_ref[0:1, :]
        mask = (sc > tau1) | (tie & (seen <= left))
        bias_ref[...] = jnp.where(mask, 0.0, -jnp.inf).astype(BF16)
        left_ref[...] = jnp.broadcast_to(left - seen[kc - 1:kc, :], left_ref.shape)

    def logits_stage(c, hh):
        r0 = pl.multiple_of(c * kc, kc)
        kg = k4_ref[0, hh // GROUP, pl.ds(r0, kc), :]
        lg = jnp.dot(kg, qT_ref[0, hh], preferred_element_type=F32).astype(BF16) + bias_ref[...]
        lg_ref[hh] = lg
        parts = [lg[pack * g:pack * (g + 1)] for g in range(kc // pack)]
        while len(parts) > 1:
            parts = [jnp.maximum(parts[g], parts[g + 1]) for g in range(0, len(parts), 2)]
        mx = parts[0].astype(F32)
        m_old = m_ref[hh:hh + 1, :]
        m_new = jnp.maximum(m_old, jnp.max(mx, axis=0, keepdims=True))
        m_ref[hh:hh + 1, :] = m_new
        al_ref[hh:hh + 1, :] = jnp.exp2(m_old - m_new)

    def value_stage(c, hh):
        vt = vT_ref[0, hh // GROUP, c]
        m_b = jnp.broadcast_to(m_ref[hh:hh + 1, :], (pack, tq)).astype(BF16)
        p = jnp.exp2(lg_ref[hh].reshape(kc // pack, pack, tq) - m_b[None]).reshape(kc, tq)
        acc_ref[hh] = acc_ref[hh] * al_ref[hh:hh + 1, :] + jnp.dot(vt, p, preferred_element_type=F32)

    set_bias(0)
    for hh in range(N_HEADS):
        logits_stage(0, hh)

    def attn_step(c):
        set_bias(c)
        for hh in range(N_HEADS):
            value_stage(c - 1, hh)
            logits_stage(c, hh)

    def attn_group(j, carry):
        for u in range(ATTN_GROUP):
            attn_step(ATTN_GROUP * j + 1 + u)
        return carry

    def attn_single(c, carry):
        attn_step(c)
        return carry

    n_grp = (n_c - 1) // ATTN_GROUP
    lax.fori_loop(0, n_grp, attn_group, 0)
    lax.fori_loop(ATTN_GROUP * n_grp + 1, n_c, attn_single, 0)
    for hh in range(N_HEADS):
        value_stage(n_c - 1, hh)

    for hp in range(N_HEADS // 2):
        parts = []
        for hh in (2 * hp, 2 * hp + 1):
            a = acc_ref[hh]
            parts.append(a[0:HEAD_DIM] / a[HEAD_DIM:HEAD_DIM + 1])
        pair = jnp.concatenate(parts, axis=0)
        o_ref[0, :, LANES * hp:LANES * (hp + 1)] = pair.T.astype(BF16)


def _dsa_attention(qT, qiT, wiT, k4, ki, vT, bn, t_len, topk):
    tq = topk
    kern = functools.partial(_dsa_kernel, topk=topk, rb=COUNT_ROWS)
    return pl.pallas_call(
        kern,
        grid=(bn, t_len // tq),
        in_specs=[pl.BlockSpec((1, N_HEADS, HEAD_DIM, tq), lambda b, i: (b, 0, 0, i)),
                  pl.BlockSpec((1, IDX_HEADS, IDX_DIM, tq), lambda b, i: (b, 0, 0, i)),
                  pl.BlockSpec((1, IDX_HEADS, tq), lambda b, i: (b, 0, i)),
                  pl.BlockSpec((1, N_KV_HEADS, t_len, HEAD_DIM), lambda b, i: (b, 0, 0, 0),
                               pipeline_mode=pl.Buffered(1)),
                  pl.BlockSpec((1, t_len, IDX_DIM), lambda b, i: (b, 0, 0), pipeline_mode=pl.Buffered(1)),
                  pl.BlockSpec((1, N_KV_HEADS, t_len // tq, V_AUG, tq), lambda b, i: (b, 0, 0, 0, 0),
                               pipeline_mode=pl.Buffered(1))],
        out_specs=pl.BlockSpec((1, tq, Q_COLS), lambda b, i: (b, i, 0)),
        out_shape=jax.ShapeDtypeStruct((bn, t_len, Q_COLS), BF16),
        scratch_shapes=[pltpu.VMEM((t_len, tq), F32),
                        pltpu.VMEM((t_len, tq), BF16),
                        pltpu.VMEM((N_HEADS, tq), F32),
                        pltpu.VMEM((N_HEADS, tq), F32),
                        pltpu.VMEM((N_HEADS, V_AUG, tq), F32),
                        pltpu.VMEM((tq, tq), BF16),
                        pltpu.VMEM((N_HEADS, tq, tq), BF16),
                        pltpu.VMEM((SUBLANES, tq), F32),
                        pltpu.VMEM((tq, tq), BF16)],
        compiler_params=_params(2),
        name="dsa_attention",
    )(qT, qiT, wiT, k4, ki.reshape(bn, t_len, IDX_DIM), vT)


def _conv_pre_kernel(x_ref, mod_ref, g1_ref, w_ref, cw_ref, o_ref, zbuf_ref, *, tpb):
    tm, d = x_ref.shape
    i = pl.program_id(0)

    @pl.when(i % tpb == 0)
    def _():
        zbuf_ref[0:SUBLANES, :] = jnp.zeros((SUBLANES, d), F32)

    x = x_ref[...]
    sh = mod_ref[0, 0:1, :]
    sc = mod_ref[0, 1:2, :]
    ms = jnp.mean(x * x, axis=-1, keepdims=True)
    h = (x * lax.rsqrt(ms + NORM_EPS) * g1_ref[...]) * (1.0 + sc) + sh
    hb = h.astype(BF16)
    for j in range(d // MXU_DIM):
        cols = slice(MXU_DIM * j, MXU_DIM * (j + 1))
        part = [jnp.dot(hb, w_ref[:, s * d + MXU_DIM * j:s * d + MXU_DIM * (j + 1)], preferred_element_type=F32)
                for s in range(3)]
        z = part[1] * part[2]
        zbuf_ref[SUBLANES:SUBLANES + tm, cols] = z
        z1 = zbuf_ref[SUBLANES - 1:SUBLANES - 1 + tm, cols]
        z2 = zbuf_ref[SUBLANES - 2:SUBLANES - 2 + tm, cols]
        zc = cw_ref[0:1, cols] * z2 + cw_ref[1:2, cols] * z1 + cw_ref[2:3, cols] * z
        o_ref[:, cols] = (part[0] * zc).astype(BF16)
        zbuf_ref[0:SUBLANES, cols] = z[tm - SUBLANES:tm, :]


def _conv_pre(x2d, mod, g1, w_in, conv_w, t_len, tm):
    n, d = x2d.shape
    tpb = t_len // tm
    row = lambda i: (i, 0)
    return pl.pallas_call(
        functools.partial(_conv_pre_kernel, tpb=tpb),
        grid=(n // tm,),
        in_specs=[pl.BlockSpec((tm, d), row),
                  pl.BlockSpec((1, 6, d), lambda i: (i // tpb, 0, 0)),
                  _const_spec((1, d)),
                  _const_spec((d, 3 * d)),
                  _const_spec((CONV_WIDTH, d))],
        out_specs=pl.BlockSpec((tm, d), row),
        out_shape=jax.ShapeDtypeStruct((n, d), BF16),
        scratch_shapes=[pltpu.VMEM((tm + SUBLANES, d), F32)],
        compiler_params=_params(1),
        name="conv_pre",
    )(x2d, mod, g1.reshape(1, d), w_in.astype(BF16), conv_w.astype(F32))


def _post_kernel(x_ref, mix_ref, mod_ref, g2_ref, wo_ref, wg_ref, wu_ref, wd_ref, o_ref, *, th):
    gate1 = mod_ref[0, 2:3, :]
    sh2 = mod_ref[0, 3:4, :]
    sc2 = mod_ref[0, 4:5, :]
    gate2 = mod_ref[0, 5:6, :]
    y = jnp.dot(mix_ref[...], wo_ref[...], preferred_element_type=F32)
    x1 = x_ref[...] + gate1 * y
    ms = jnp.mean(x1 * x1, axis=-1, keepdims=True)
    h = ((x1 * lax.rsqrt(ms + NORM_EPS) * g2_ref[...]) * (1.0 + sc2) + sh2).astype(BF16)
    hidden = wg_ref.shape[1]
    acc = jnp.zeros(x1.shape, F32)
    for j in range(hidden // th):
        gt = jnp.dot(h, wg_ref[:, th * j:th * (j + 1)], preferred_element_type=F32)
        up = jnp.dot(h, wu_ref[:, th * j:th * (j + 1)], preferred_element_type=F32)
        a = (gt * jax.nn.sigmoid(gt)) * up
        acc = acc + jnp.dot(a.astype(BF16), wd_ref[th * j:th * (j + 1), :], preferred_element_type=F32)
    o_ref[...] = x1 + gate2 * acc


def _post(x2d, mix, mod, g2, w_out, w_gate, w_up, w_down, layer, t_len, tm):
    n, d = x2d.shape
    hidden = w_gate.shape[2]
    tpb = t_len // tm
    row = lambda i: (i, 0)

    def layer_spec(shape):
        return pl.BlockSpec((None,) + shape, lambda i: (layer, 0, 0), pipeline_mode=pl.Buffered(1))
    return pl.pallas_call(
        functools.partial(_post_kernel, th=FFN_CHUNK),
        grid=(n // tm,),
        in_specs=[pl.BlockSpec((tm, d), row),
                  pl.BlockSpec((tm, d), row),
                  pl.BlockSpec((1, 6, d), lambda i: (i // tpb, 0, 0)),
                  _const_spec((1, d)),
                  _const_spec((d, d)),
                  layer_spec((d, hidden)),
                  layer_spec((d, hidden)),
                  layer_spec((hidden, d))],
        out_specs=pl.BlockSpec((tm, d), row),
        out_shape=jax.ShapeDtypeStruct((n, d), F32),
        compiler_params=_params(1),
        name="mixer_out_ffn",
    )(x2d, mix, mod, g2.reshape(1, d), w_out.astype(BF16), w_gate, w_up, w_down)


def kernel(x, c, positions, ada_w, ada_b, norm1_g, norm2_g, attn_w_in, attn_q_norm_g, attn_k_norm_g,
           idx_k_ln_g, idx_k_ln_b, attn_w_out, conv_w_in, conv_w, conv_w_out, ffn_w_gate, ffn_w_up,
           ffn_w_down):
    bn, t_len, d = x.shape
    depth = ada_w.shape[0]
    topk = min(TOPK_MAX, t_len // 4)
    assert topk == TOPK_MAX == MXU_DIM and d == Q_COLS
    assert t_len % TM_PRE == 0 and t_len % TM_ROWS == 0 and TM_PRE % topk == 0 and COUNT_ROWS % topk == 0
    n = bn * t_len
    mod = _modulation(c, ada_w, ada_b)
    wg_all, wu_all, wd_all = (w.astype(BF16) for w in (ffn_w_gate, ffn_w_up, ffn_w_down))
    x2d = x.reshape(n, d)
    for i in range(depth):
        j = i // 2
        if i % 2 == 0:
            qT, qiT, wiT, vT, k4, ki = _attn_pre(
                x2d, mod[i], norm1_g[i], attn_w_in[j], attn_q_norm_g[j], attn_k_norm_g[j],
                idx_k_ln_g[j], idx_k_ln_b[j], positions, bn, t_len, TM_PRE, topk)
            mix = _dsa_attention(qT, qiT, wiT, k4, ki, vT, bn, t_len, topk).reshape(n, d)
            w_mix_out = attn_w_out[j]
        else:
            mix = _conv_pre(x2d, mod[i], norm1_g[i], conv_w_in[j], conv_w[j], t_len, TM_ROWS)
            w_mix_out = conv_w_out[j]
        x2d = _post(x2d, mix, mod[i], norm2_g[i], w_mix_out, wg_all, wu_all, wd_all, i, t_len, TM_ROWS)
    return x2d.reshape(bn, t_len, d)
```

```python
import functools
import math

import numpy as np
import jax
import jax.numpy as jnp
from jax import lax
from jax.experimental import pallas as pl
from jax.experimental.pallas import tpu as pltpu

F32 = jnp.float32
BF16 = jnp.bfloat16
I32 = jnp.int32

N_HEADS = 16
HEAD_DIM = 64
N_KV_HEADS = 4
GROUP = N_HEADS // N_KV_HEADS
IDX_HEADS = 8
IDX_DIM = 64
TOPK_MAX = 256
ROPE_THETA = 10000.0
CONV_WIDTH = 3
NORM_EPS = 1e-6
LOG2E = math.log2(math.e)

LANES = 128
SUBLANES = 8
BF16_ROWS = 2 * SUBLANES
MXU_DIM = 256
VMEM_LIMIT = 56 * 1024 * 1024

Q_COLS = N_HEADS * HEAD_DIM
KV_COLS = N_KV_HEADS * HEAD_DIM
QI_COLS = IDX_HEADS * IDX_DIM
V_AUG = HEAD_DIM + BF16_ROWS
SCORE_GROUP = 4
ATTN_GROUP = 4
COUNT_ROWS = 1024
TM_PRE = 512
TM_ROWS = 1024
FFN_CHUNK = MXU_DIM

INT_MIN = np.int32(-2**31)
KEY_NEG_INF = np.int32(np.uint32(0x807FFFFF).astype(np.int64) - 2**32)
KEY_HI_NEG_INF = np.int32(np.uint32(0x807F0000).astype(np.int64) - 2**32)


def _const_spec(shape):
    nd = len(shape)
    return pl.BlockSpec(shape, lambda *_: (0,) * nd, pipeline_mode=pl.Buffered(1))


def _params(n_axes):
    return pltpu.CompilerParams(dimension_semantics=("arbitrary",) * n_axes,
                                vmem_limit_bytes=VMEM_LIMIT)


def _mod_kernel(c_ref, w_ref, b_ref, o_ref):
    c = c_ref[...]
    ca = c * jax.nn.sigmoid(c)
    o_ref[0] = jnp.dot(ca.astype(BF16), w_ref[0].astype(BF16), preferred_element_type=F32) + b_ref[0]


def _modulation(c, ada_w, ada_b):
    depth, d, six_d = ada_w.shape
    bn = c.shape[0]
    rows = BF16_ROWS
    assert bn <= rows
    cp = jnp.zeros((rows, d), F32).at[:bn].set(c)
    tn = six_d // 4
    out = pl.pallas_call(
        _mod_kernel,
        grid=(depth, six_d // tn),
        in_specs=[pl.BlockSpec((rows, d), lambda i, j: (0, 0)),
                  pl.BlockSpec((1, d, tn), lambda i, j: (i, 0, j)),
                  pl.BlockSpec((1, 1, tn), lambda i, j: (i, 0, j))],
        out_specs=pl.BlockSpec((1, rows, tn), lambda i, j: (i, 0, j)),
        out_shape=jax.ShapeDtypeStruct((depth, rows, six_d), F32),
        compiler_params=_params(2),
        name="adaln_mod",
    )(cp, ada_w, ada_b.reshape(depth, 1, six_d))
    return out[:, :bn].reshape(depth, bn, 6, d)


def _attn_pre_kernel(x_ref, mod_ref, g1_ref, wt_ref, posr_ref, invft_ref, gq_ref, gk_ref, lng_ref, lnb_ref,
                     qT_ref, qiT_ref, wiT_ref, vT_ref, k4_ref, ki_ref):
    tm = x_ref.shape[0]
    x = x_ref[...]
    sh = mod_ref[0, 0:1, :]
    sc = mod_ref[0, 1:2, :]
    ms = jnp.mean(x * x, axis=-1, keepdims=True)
    h = (x * lax.rsqrt(ms + NORM_EPS) * g1_ref[...]) * (1.0 + sc) + sh
    hb = h.astype(BF16)

    def proj(first, rows):
        return lax.dot_general(wt_ref[first:first + rows, :], hb, (((1,), (1,)), ((), ())),
                               preferred_element_type=F32)

    angT = invft_ref[...] * posr_ref[0].astype(F32)
    cT = jnp.cos(angT)
    sT = jnp.sin(angT)
    half = HEAD_DIM // 2

    def rope_t(y):
        x1 = y[:half]
        x2 = y[half:]
        return jnp.concatenate([x1 * cT - x2 * sT, x2 * cT + x1 * sT], axis=0)

    gq = gq_ref[...]
    per_sec = MXU_DIM // HEAD_DIM
    for sec in range(N_HEADS // per_sec):
        pq = proj(MXU_DIM * sec, MXU_DIM)
        for hl in range(per_sec):
            xq = pq[HEAD_DIM * hl:HEAD_DIM * (hl + 1)]
            inv = lax.rsqrt(jnp.mean(xq * xq, axis=0, keepdims=True) + NORM_EPS)
            y = xq * inv * gq
            qT_ref[0, per_sec * sec + hl] = (rope_t(y) * (HEAD_DIM ** -0.5 * LOG2E)).astype(BF16)
    o0 = Q_COLS
    gk = gk_ref[...]
    pk = proj(o0, KV_COLS)
    for j in range(N_KV_HEADS // 2):
        pair = []
        for g in (2 * j, 2 * j + 1):
            xk = pk[HEAD_DIM * g:HEAD_DIM * (g + 1)]
            inv = lax.rsqrt(jnp.mean(xk * xk, axis=0, keepdims=True) + NORM_EPS)
            pair.append(rope_t(xk * inv * gk))
        kr = jnp.concatenate(pair, axis=0).T
        k4_ref[0, 2 * j] = kr[:, 0:HEAD_DIM].astype(BF16)
        k4_ref[0, 2 * j + 1] = kr[:, HEAD_DIM:LANES].astype(BF16)
    o0 += KV_COLS
    vc = vT_ref.shape[-1]
    ones = jnp.ones((V_AUG - HEAD_DIM, vc), BF16)
    pv = proj(o0, KV_COLS).astype(BF16)
    for g in range(N_KV_HEADS):
        for u in range(tm // vc):
            vT_ref[0, g, u, 0:HEAD_DIM, :] = pv[HEAD_DIM * g:HEAD_DIM * (g + 1), vc * u:vc * (u + 1)]
            vT_ref[0, g, u, HEAD_DIM:V_AUG, :] = ones
    o0 += KV_COLS
    for sec in range(QI_COLS // MXU_DIM):
        pqi = proj(o0 + MXU_DIM * sec, MXU_DIM)
        for hl in range(per_sec):
            qiT_ref[0, per_sec * sec + hl] = rope_t(pqi[IDX_DIM * hl:IDX_DIM * (hl + 1)]).astype(BF16)
    o0 += QI_COLS
    tail = proj(o0, wt_ref.shape[0] - o0)
    t = tail[0:IDX_DIM]
    mu = jnp.mean(t, axis=0, keepdims=True)
    dlt = t - mu
    var = jnp.mean(dlt * dlt, axis=0, keepdims=True)
    kin = rope_t(dlt * lax.rsqrt(var + NORM_EPS) * lng_ref[...] + lnb_ref[...])
    kin = jnp.concatenate([kin, jnp.zeros((LANES - IDX_DIM, tm), F32)], axis=0).T
    ki_ref[...] = kin[:, 0:IDX_DIM].astype(BF16)
    wiT_ref[0] = tail[IDX_DIM:IDX_DIM + IDX_HEADS] * (IDX_HEADS ** -0.5 * IDX_DIM ** -0.5)


def _attn_pre(x2d, mod, g1, w_in, gq, gk, lng, lnb, positions, bn, t_len, tm, vc):
    n, d = x2d.shape
    tpb = t_len // tm
    half = HEAD_DIM // 2
    ma = w_in.shape[1]
    ma_pad = -(-ma // BF16_ROWS) * BF16_ROWS
    wt = jnp.concatenate([w_in, jnp.zeros((d, ma_pad - ma), F32)], axis=1).T.astype(BF16)
    inv_freq = ROPE_THETA ** (-(jnp.arange(half, dtype=F32) * 2.0 / HEAD_DIM))
    invft = jnp.broadcast_to(inv_freq[:, None], (half, tm))
    col = lambda v: jnp.broadcast_to(v.astype(F32)[:, None], (v.shape[0], tm))
    posr = positions.reshape(bn, 1, t_len)

    row = lambda i: (i, 0)
    outs = pl.pallas_call(
        _attn_pre_kernel,
        grid=(n // tm,),
        in_specs=[pl.BlockSpec((tm, d), row),
                  pl.BlockSpec((1, 6, d), lambda i: (i // tpb, 0, 0)),
                  _const_spec((1, d)),
                  _const_spec((ma_pad, d)),
                  pl.BlockSpec((1, 1, tm), lambda i: (i // tpb, 0, i % tpb)),
                  _const_spec((half, tm)),
                  _const_spec((HEAD_DIM, tm)),
                  _const_spec((HEAD_DIM, tm)),
                  _const_spec((IDX_DIM, tm)),
                  _const_spec((IDX_DIM, tm))],
        out_specs=[pl.BlockSpec((1, N_HEADS, HEAD_DIM, tm), lambda i: (i // tpb, 0, 0, i % tpb)),
                   pl.BlockSpec((1, IDX_HEADS, IDX_DIM, tm), lambda i: (i // tpb, 0, 0, i % tpb)),
                   pl.BlockSpec((1, IDX_HEADS, tm), lambda i: (i // tpb, 0, i % tpb)),
                   pl.BlockSpec((1, N_KV_HEADS, tm // vc, V_AUG, vc), lambda i: (i // tpb, 0, i % tpb, 0, 0)),
                   pl.BlockSpec((1, N_KV_HEADS, tm, HEAD_DIM), lambda i: (i // tpb, 0, i % tpb, 0)),
                   pl.BlockSpec((tm, IDX_DIM), row)],
        out_shape=[jax.ShapeDtypeStruct((bn, N_HEADS, HEAD_DIM, t_len), BF16),
                   jax.ShapeDtypeStruct((bn, IDX_HEADS, IDX_DIM, t_len), BF16),
                   jax.ShapeDtypeStruct((bn, IDX_HEADS, t_len), F32),
                   jax.ShapeDtypeStruct((bn, N_KV_HEADS, t_len // vc, V_AUG, vc), BF16),
                   jax.ShapeDtypeStruct((bn, N_KV_HEADS, t_len, HEAD_DIM), BF16),
                   jax.ShapeDtypeStruct((n, IDX_DIM), BF16)],
        compiler_params=_params(1),
        name="attn_pre",
    )(x2d, mod, g1.reshape(1, d), wt, posr, invft, col(gq), col(gk), col(lng), col(lnb))
    return outs


def _dsa_kernel(qT_ref, qiT_ref, wiT_ref, k4_ref, ki_ref, vT_ref, o_ref,
                st_ref, hi_ref, m_ref, al_ref, acc_ref, bias_ref, lg_ref, left_ref, ltri_ref, *, topk, rb):
    tq = o_ref.shape[1]
    kc = tq
    i = pl.program_id(1)
    n_c = i + 1
    q0 = i * tq

    tcol = q0 + lax.broadcasted_iota(I32, (kc, tq), 1)
    srow0 = lax.broadcasted_iota(I32, (kc, tq), 0)
    wi = wiT_ref[0]
    qi_all = jnp.concatenate([qiT_ref[0, hh] for hh in range(IDX_HEADS)], axis=1)

    def score_chunk(c):
        r0 = pl.multiple_of(c * kc, kc)
        ki = ki_ref[0, pl.ds(r0, kc), :]
        s_all = jnp.dot(ki, qi_all, preferred_element_type=F32)
        acc = jnp.zeros((kc, tq), F32)
        for hh in range(IDX_HEADS):
            acc = acc + jnp.maximum(s_all[:, tq * hh:tq * (hh + 1)], 0.0) * wi[hh:hh + 1, :]
        causal = (srow0 + r0) <= tcol
        sc = jnp.where(causal, acc, -jnp.inf)
        st_ref[pl.ds(r0, kc), :] = sc
        top = lax.bitcast_convert_type(sc, I32) & np.int32(-65536)
        hi_ref[pl.ds(r0, kc), :] = lax.bitcast_convert_type(top, F32).astype(BF16)

    def score_group(j, carry):
        for u in range(SCORE_GROUP):
            score_chunk(SCORE_GROUP * j + u)
        return carry

    def score_single(c, carry):
        score_chunk(c)
        return carry

    n_grp = n_c // SCORE_GROUP
    lax.fori_loop(0, n_grp, score_group, 0)
    lax.fori_loop(n_grp * SCORE_GROUP, n_c, score_single, 0)

    n_big = (n_c * kc) // rb
    n_small = (n_c * kc - n_big * rb) // kc
    n_par = 4
    pack = BF16_ROWS

    def all_sublanes(tot):
        for sh in (4, 2, 1):
            tot = tot + pltpu.roll(tot, sh, 0)
        return tot

    def two_level(block_fn, init):
        acc = lax.fori_loop(0, n_big, lambda r, a: block_fn(pl.multiple_of(r * rb, rb), rb, a), init)
        base = n_big * rb
        return lax.fori_loop(0, n_small, lambda r, a: block_fn(pl.multiple_of(base + r * kc, kc), kc, a), acc)

    def count(pred):
        def block(r0, rows, acc):
            blk = st_ref[pl.ds(r0, rows), :].reshape(rows // SUBLANES, SUBLANES, tq)
            hit = pred(blk).astype(I32).reshape(rows // (SUBLANES * n_par), n_par, SUBLANES, tq)
            return acc + jnp.sum(hit, axis=0)
        acc = two_level(block, jnp.zeros((n_par, SUBLANES, tq), I32))
        return all_sublanes(jnp.sum(acc, axis=0))

    def count_hi(cand16):
        def block(r0, rows, acc):
            blk = hi_ref[pl.ds(r0, rows), :].reshape(rows // pack, pack, tq)
            hit = jnp.where(blk >= cand16[None], jnp.ones((), BF16), jnp.zeros((), BF16))
            chains = [hit[k] for k in range(n_par)]
            for g in range(n_par, rows // pack):
                chains[g % n_par] = chains[g % n_par] + hit[g]
            return acc + ((chains[0] + chains[1]) + (chains[2] + chains[3])).astype(F32)
        acc = two_level(block, jnp.zeros((pack, tq), F32))
        return all_sublanes((acc[0:SUBLANES] + acc[SUBLANES:pack]).astype(I32))

    def key_to_float(u):
        ks = u ^ INT_MIN
        bits = jnp.where(ks < 0, ks ^ np.int32(0x7FFFFFFF), ks)
        return ks, bits

    def hi_body(it, carry):
        tau_u, cnt_tau = carry
        cand_u = tau_u | lax.shift_left(np.int32(1), np.int32(31) - it)
        ks, bits = key_to_float(cand_u)
        cand_top = lax.bitcast_convert_type(bits & np.int32(-65536), F32)
        cand16 = jnp.concatenate([cand_top, cand_top], axis=0).astype(BF16)
        cnt = count_hi(cand16)
        take = (cnt >= topk) | (ks < KEY_HI_NEG_INF)
        return jnp.where(take, cand_u, tau_u), jnp.where(take, cnt, cnt_tau)

    def lo_body(it, carry):
        tau_u, cnt_tau = carry
        cand_u = tau_u | lax.shift_left(np.int32(1), np.int32(15) - it)
        ks, bits = key_to_float(cand_u)
        cand_f = lax.bitcast_convert_type(bits, F32)
        cnt = count(lambda blk: blk >= cand_f[None])
        take = (cnt >= topk) | (ks < KEY_NEG_INF)
        return jnp.where(take, cand_u, tau_u), jnp.where(take, cnt, cnt_tau)

    def lo_stage(state, first, n):
        tau_u, cnt_tau, _ = state
        tau_u, cnt_tau = lax.fori_loop(first, first + n, lo_body, (tau_u, cnt_tau))
        tau8 = lax.bitcast_convert_type(key_to_float(tau_u)[1], F32)
        return tau_u, cnt_tau, count(lambda blk: blk > tau8[None])

    def pending(state):
        tau_u, cnt_tau, cnt_gt = state
        open_lane = ((cnt_tau != topk) & (cnt_gt >= topk)) | ((tau_u ^ INT_MIN) < KEY_NEG_INF)
        return jnp.max(open_lane.astype(I32)) > 0

    zero8 = jnp.zeros((SUBLANES, tq), I32)
    tau_u, cnt_tau = lax.fori_loop(0, 16, hi_body, (zero8, zero8))
    state = lo_stage((tau_u, cnt_tau, zero8), 0, 10)
    state = lax.cond(pending(state), lambda s: lo_stage(s, 10, 2), lambda s: s, state)
    state = lax.cond(pending(state), lambda s: lo_stage(s, 12, 4), lambda s: s, state)
    tau_u, _, cnt_gt = state
    tau8 = lax.bitcast_convert_type(key_to_float(tau_u)[1], F32)
    left_ref[...] = jnp.where(tau8 == -jnp.inf, 0.0, (topk - cnt_gt).astype(F32))
    tau1 = tau8[0:1]
    ri = lax.broadcasted_iota(I32, (kc, kc), 0)
    ci = lax.broadcasted_iota(I32, (kc, kc), 1)
    ltri_ref[...] = jnp.where(ci <= ri, 1.0, 0.0).astype(BF16)

    m_ref[...] = jnp.full(m_ref.shape, -(2.0 ** 100), F32)
    acc_ref[...] = jnp.zeros(acc_ref.shape, F32)

    def set_bias(c):
        r0 = pl.multiple_of(c * kc, kc)
        sc = st_ref[pl.ds(r0, kc), :]
        tie = sc == tau1
        seen = jnp.dot(ltri_ref[...], jnp.where(tie, 1.0, 0.0).astype(BF16), preferred_element_type=F32)
        left = left_ref[0:1, :]
        mask = (sc > tau1) | (tie & (seen <= left))
        bias_ref[...] = jnp.where(mask, 0.0, -jnp.inf).astype(BF16)
        left_ref[...] = jnp.broadcast_to(left - seen[kc - 1:kc, :], left_ref.shape)

    def logits_stage(c, hh):
        r0 = pl.multiple_of(c * kc, kc)
        kg = k4_ref[0, hh // GROUP, pl.ds(r0, kc), :]
        lg = jnp.dot(kg, qT_ref[0, hh], preferred_element_type=F32).astype(BF16) + bias_ref[...]
        lg_ref[hh] = lg
        parts = [lg[pack * g:pack * (g + 1)] for g in range(kc // pack)]
        while len(parts) > 1:
            parts = [jnp.maximum(parts[g], parts[g + 1]) for g in range(0, len(parts), 2)]
        mx = parts[0].astype(F32)
        m_old = m_ref[hh:hh + 1, :]
        m_new = jnp.maximum(m_old, jnp.max(mx, axis=0, keepdims=True))
        m_ref[hh:hh + 1, :] = m_new
        al_ref[hh:hh + 1, :] = jnp.exp2(m_old - m_new)

    def value_stage(c, hh):
        vt = vT_ref[0, hh // GROUP, c]
        m_b = jnp.broadcast_to(m_ref[hh:hh + 1, :], (pack, tq)).astype(BF16)
        p = jnp.exp2(lg_ref[hh].reshape(kc // pack, pack, tq) - m_b[None]).reshape(kc, tq)
        acc_ref[hh] = acc_ref[hh] * al_ref[hh:hh + 1, :] + jnp.dot(vt, p, preferred_element_type=F32)

    set_bias(0)
    for hh in range(N_HEADS):
        logits_stage(0, hh)

    def attn_step(c):
        set_bias(c)
        for hh in range(N_HEADS):
            value_stage(c - 1, hh)
            logits_stage(c, hh)

    def attn_group(j, carry):
        for u in range(ATTN_GROUP):
            attn_step(ATTN_GROUP * j + 1 + u)
        return carry

    def attn_pair(j, carry):
        attn_step(done + 2 * j)
        attn_step(done + 2 * j + 1)
        return carry

    def attn_single(c, carry):
        attn_step(c)
        return carry

    n_grp = (n_c - 1) // ATTN_GROUP
    lax.fori_loop(0, n_grp, attn_group, 0)
    done = ATTN_GROUP * n_grp + 1
    n_pair = (n_c - done) // 2
    lax.fori_loop(0, n_pair, attn_pair, 0)
    lax.fori_loop(done + 2 * n_pair, n_c, attn_single, 0)
    for hh in range(N_HEADS):
        value_stage(n_c - 1, hh)

    for hp in range(N_HEADS // 2):
        parts = []
        for hh in (2 * hp, 2 * hp + 1):
            a = acc_ref[hh]
            parts.append(a[0:HEAD_DIM] / a[HEAD_DIM:HEAD_DIM + 1])
        pair = jnp.concatenate(parts, axis=0)
        o_ref[0, :, LANES * hp:LANES * (hp + 1)] = pair.T.astype(BF16)


def _dsa_attention(qT, qiT, wiT, k4, ki, vT, bn, t_len, topk):
    tq = topk
    kern = functools.partial(_dsa_kernel, topk=topk, rb=COUNT_ROWS)
    return pl.pallas_call(
        kern,
        grid=(bn, t_len // tq),
        in_specs=[pl.BlockSpec((1, N_HEADS, HEAD_DIM, tq), lambda b, i: (b, 0, 0, i)),
                  pl.BlockSpec((1, IDX_HEADS, IDX_DIM, tq), lambda b, i: (b, 0, 0, i)),
                  pl.BlockSpec((1, IDX_HEADS, tq), lambda b, i: (b, 0, i)),
                  pl.BlockSpec((1, N_KV_HEADS, t_len, HEAD_DIM), lambda b, i: (b, 0, 0, 0),
                               pipeline_mode=pl.Buffered(1)),
                  pl.BlockSpec((1, t_len, IDX_DIM), lambda b, i: (b, 0, 0), pipeline_mode=pl.Buffered(1)),
                  pl.BlockSpec((1, N_KV_HEADS, t_len // tq, V_AUG, tq), lambda b, i: (b, 0, 0, 0, 0),
                               pipeline_mode=pl.Buffered(1))],
        out_specs=pl.BlockSpec((1, tq, Q_COLS), lambda b, i: (b, i, 0)),
        out_shape=jax.ShapeDtypeStruct((bn, t_len, Q_COLS), BF16),
        scratch_shapes=[pltpu.VMEM((t_len, tq), F32),
                        pltpu.VMEM((t_len, tq), BF16),
                        pltpu.VMEM((N_HEADS, tq), F32),
                        pltpu.VMEM((N_HEADS, tq), F32),
                        pltpu.VMEM((N_HEADS, V_AUG, tq), F32),
                        pltpu.VMEM((tq, tq), BF16),
                        pltpu.VMEM((N_HEADS, tq, tq), BF16),
                        pltpu.VMEM((SUBLANES, tq), F32),
                        pltpu.VMEM((tq, tq), BF16)],
        compiler_params=_params(2),
        name="dsa_attention",
    )(qT, qiT, wiT, k4, ki.reshape(bn, t_len, IDX_DIM), vT)


def _conv_pre_kernel(x_ref, mod_ref, g1_ref, w_ref, cw_ref, o_ref, zbuf_ref, *, tpb):
    tm, d = x_ref.shape
    i = pl.program_id(0)

    @pl.when(i % tpb == 0)
    def _():
        zbuf_ref[0:SUBLANES, :] = jnp.zeros((SUBLANES, d), F32)

    x = x_ref[...]
    sh = mod_ref[0, 0:1, :]
    sc = mod_ref[0, 1:2, :]
    ms = jnp.mean(x * x, axis=-1, keepdims=True)
    h = (x * lax.rsqrt(ms + NORM_EPS) * g1_ref[...]) * (1.0 + sc) + sh
    hb = h.astype(BF16)
    for j in range(d // MXU_DIM):
        cols = slice(MXU_DIM * j, MXU_DIM * (j + 1))
        part = [jnp.dot(hb, w_ref[:, s * d + MXU_DIM * j:s * d + MXU_DIM * (j + 1)], preferred_element_type=F32)
                for s in range(3)]
        z = part[1] * part[2]
        zbuf_ref[SUBLANES:SUBLANES + tm, cols] = z
        z1 = zbuf_ref[SUBLANES - 1:SUBLANES - 1 + tm, cols]
        z2 = zbuf_ref[SUBLANES - 2:SUBLANES - 2 + tm, cols]
        zc = cw_ref[0:1, cols] * z2 + cw_ref[1:2, cols] * z1 + cw_ref[2:3, cols] * z
        o_ref[:, cols] = (part[0] * zc).astype(BF16)
        zbuf_ref[0:SUBLANES, cols] = z[tm - SUBLANES:tm, :]


def _conv_pre(x2d, mod, g1, w_in, conv_w, t_len, tm):
    n, d = x2d.shape
    tpb = t_len // tm
    row = lambda i: (i, 0)
    return pl.pallas_call(
        functools.partial(_conv_pre_kernel, tpb=tpb),
        grid=(n // tm,),
        in_specs=[pl.BlockSpec((tm, d), row),
                  pl.BlockSpec((1, 6, d), lambda i: (i // tpb, 0, 0)),
                  _const_spec((1, d)),
                  _const_spec((d, 3 * d)),
                  _const_spec((CONV_WIDTH, d))],
        out_specs=pl.BlockSpec((tm, d), row),
        out_shape=jax.ShapeDtypeStruct((n, d), BF16),
        scratch_shapes=[pltpu.VMEM((tm + SUBLANES, d), F32)],
        compiler_params=_params(1),
        name="conv_pre",
    )(x2d, mod, g1.reshape(1, d), w_in.astype(BF16), conv_w.astype(F32))


def _post_kernel(x_ref, mix_ref, mod_ref, g2_ref, wo_ref, wg_ref, wu_ref, wd_ref, o_ref, *, th):
    gate1 = mod_ref[0, 2:3, :]
    sh2 = mod_ref[0, 3:4, :]
    sc2 = mod_ref[0, 4:5, :]
    gate2 = mod_ref[0, 5:6, :]
    y = jnp.dot(mix_ref[...], wo_ref[...], preferred_element_type=F32)
    x1 = x_ref[...] + gate1 * y
    ms = jnp.mean(x1 * x1, axis=-1, keepdims=True)
    h = ((x1 * lax.rsqrt(ms + NORM_EPS) * g2_ref[...]) * (1.0 + sc2) + sh2).astype(BF16)
    hidden = wg_ref.shape[1]
    acc = jnp.zeros(x1.shape, F32)
    for j in range(hidden // th):
        gt = jnp.dot(h, wg_ref[:, th * j:th * (j + 1)], preferred_element_type=F32)
        up = jnp.dot(h, wu_ref[:, th * j:th * (j + 1)], preferred_element_type=F32)
        a = (gt * jax.nn.sigmoid(gt)) * up
        acc = acc + jnp.dot(a.astype(BF16), wd_ref[th * j:th * (j + 1), :], preferred_element_type=F32)
    o_ref[...] = x1 + gate2 * acc


def _post(x2d, mix, mod, g2, w_out, w_gate, w_up, w_down, layer, t_len, tm):
    n, d = x2d.shape
    hidden = w_gate.shape[2]
    tpb = t_len // tm
    row = lambda i: (i, 0)

    def layer_spec(shape):
        return pl.BlockSpec((None,) + shape, lambda i: (layer, 0, 0), pipeline_mode=pl.Buffered(1))
    return pl.pallas_call(
        functools.partial(_post_kernel, th=FFN_CHUNK),
        grid=(n // tm,),
        in_specs=[pl.BlockSpec((tm, d), row),
                  pl.BlockSpec((tm, d), row),
                  pl.BlockSpec((1, 6, d), lambda i: (i // tpb, 0, 0)),
                  _const_spec((1, d)),
                  _const_spec((d, d)),
                  layer_spec((d, hidden)),
                  layer_spec((d, hidden)),
                  layer_spec((hidden, d))],
        out_specs=pl.BlockSpec((tm, d), row),
        out_shape=jax.ShapeDtypeStruct((n, d), F32),
        compiler_params=_params(1),
        name="mixer_out_ffn",
    )(x2d, mix, mod, g2.reshape(1, d), w_out.astype(BF16), w_gate, w_up, w_down)


def kernel(x, c, positions, ada_w, ada_b, norm1_g, norm2_g, attn_w_in, attn_q_norm_g, attn_k_norm_g,
           idx_k_ln_g, idx_k_ln_b, attn_w_out, conv_w_in, conv_w, conv_w_out, ffn_w_gate, ffn_w_up,
           ffn_w_down):
    bn, t_len, d = x.shape
    depth = ada_w.shape[0]
    topk = min(TOPK_MAX, t_len // 4)
    assert topk == TOPK_MAX == MXU_DIM and d == Q_COLS
    assert t_len % TM_PRE == 0 and t_len % TM_ROWS == 0 and TM_PRE % topk == 0 and COUNT_ROWS % topk == 0
    n = bn * t_len
    mod = _modulation(c, ada_w, ada_b)
    wg_all, wu_all, wd_all = (w.astype(BF16) for w in (ffn_w_gate, ffn_w_up, ffn_w_down))
    x2d = x.reshape(n, d)
    for i in range(depth):
        j = i // 2
        if i % 2 == 0:
            qT, qiT, wiT, vT, k4, ki = _attn_pre(
                x2d, mod[i], norm1_g[i], attn_w_in[j], attn_q_norm_g[j], attn_k_norm_g[j],
                idx_k_ln_g[j], idx_k_ln_b[j], positions, bn, t_len, TM_PRE, topk)
            mix = _dsa_attention(qT, qiT, wiT, k4, ki, vT, bn, t_len, topk).reshape(n, d)
            w_mix_out = attn_w_out[j]
        else:
            mix = _conv_pre(x2d, mod[i], norm1_g[i], conv_w_in[j], conv_w[j], t_len, TM_ROWS)
            w_mix_out = conv_w_out[j]
        x2d = _post(x2d, mix, mod[i], norm2_g[i], w_mix_out, wg_all, wu_all, wd_all, i, t_len, TM_ROWS)
    return x2d.reshape(bn, t_len, d)
```

```python
import functools
import math

import numpy as np
import jax
import jax.numpy as jnp
from jax import lax
from jax.experimental import pallas as pl
from jax.experimental.pallas import tpu as pltpu

F32 = jnp.float32
BF16 = jnp.bfloat16
I32 = jnp.int32

N_HEADS = 16
HEAD_DIM = 64
N_KV_HEADS = 4
GROUP = N_HEADS // N_KV_HEADS
IDX_HEADS = 8
IDX_DIM = 64
TOPK_MAX = 256
ROPE_THETA = 10000.0
CONV_WIDTH = 3
NORM_EPS = 1e-6
LOG2E = math.log2(math.e)

LANES = 128
SUBLANES = 8
BF16_ROWS = 2 * SUBLANES
MXU_DIM = 256
VMEM_LIMIT = 56 * 1024 * 1024

Q_COLS = N_HEADS * HEAD_DIM
KV_COLS = N_KV_HEADS * HEAD_DIM
QI_COLS = IDX_HEADS * IDX_DIM
V_AUG = HEAD_DIM + BF16_ROWS
SCORE_GROUP = 4
ATTN_GROUP = 4
COUNT_ROWS = 1024
TM_PRE = 1024
TM_ROWS = 1024
FFN_CHUNK = MXU_DIM

INT_MIN = np.int32(-2**31)
KEY_NEG_INF = np.int32(np.uint32(0x807FFFFF).astype(np.int64) - 2**32)
KEY_HI_NEG_INF = np.int32(np.uint32(0x807F0000).astype(np.int64) - 2**32)


def _const_spec(shape):
    nd = len(shape)
    return pl.BlockSpec(shape, lambda *_: (0,) * nd, pipeline_mode=pl.Buffered(1))


def _params(n_axes):
    return pltpu.CompilerParams(dimension_semantics=("arbitrary",) * n_axes,
                                vmem_limit_bytes=VMEM_LIMIT)


def _mod_kernel(c_ref, w_ref, b_ref, o_ref):
    c = c_ref[...]
    ca = c * jax.nn.sigmoid(c)
    o_ref[0] = jnp.dot(ca.astype(BF16), w_ref[0].astype(BF16), preferred_element_type=F32) + b_ref[0]


def _modulation(c, ada_w, ada_b):
    depth, d, six_d = ada_w.shape
    bn = c.shape[0]
    rows = BF16_ROWS
    assert bn <= rows
    cp = jnp.zeros((rows, d), F32).at[:bn].set(c)
    tn = six_d // 4
    out = pl.pallas_call(
        _mod_kernel,
        grid=(depth, six_d // tn),
        in_specs=[pl.BlockSpec((rows, d), lambda i, j: (0, 0)),
                  pl.BlockSpec((1, d, tn), lambda i, j: (i, 0, j)),
                  pl.BlockSpec((1, 1, tn), lambda i, j: (i, 0, j))],
        out_specs=pl.BlockSpec((1, rows, tn), lambda i, j: (i, 0, j)),
        out_shape=jax.ShapeDtypeStruct((depth, rows, six_d), F32),
        compiler_params=_params(2),
        name="adaln_mod",
    )(cp, ada_w, ada_b.reshape(depth, 1, six_d))
    return out[:, :bn].reshape(depth, bn, 6, d)


def _attn_pre_kernel(x_ref, mod_ref, g1_ref, wt_ref, posr_ref, invft_ref, gq_ref, gk_ref, lng_ref, lnb_ref,
                     qT_ref, qiT_ref, wiT_ref, vT_ref, k4_ref, ki_ref):
    tm = x_ref.shape[0]
    x = x_ref[...]
    sh = mod_ref[0, 0:1, :]
    sc = mod_ref[0, 1:2, :]
    ms = jnp.mean(x * x, axis=-1, keepdims=True)
    h = (x * lax.rsqrt(ms + NORM_EPS) * g1_ref[...]) * (1.0 + sc) + sh
    hb = h.astype(BF16)

    def proj(first, rows):
        return lax.dot_general(wt_ref[first:first + rows, :], hb, (((1,), (1,)), ((), ())),
                               preferred_element_type=F32)

    angT = invft_ref[...] * posr_ref[0].astype(F32)
    cT = jnp.cos(angT)
    sT = jnp.sin(angT)
    half = HEAD_DIM // 2

    def rope_t(y):
        x1 = y[:half]
        x2 = y[half:]
        return jnp.concatenate([x1 * cT - x2 * sT, x2 * cT + x1 * sT], axis=0)

    gq = gq_ref[...]
    per_sec = MXU_DIM // HEAD_DIM
    for sec in range(N_HEADS // per_sec):
        pq = proj(MXU_DIM * sec, MXU_DIM)
        for hl in range(per_sec):
            xq = pq[HEAD_DIM * hl:HEAD_DIM * (hl + 1)]
            inv = lax.rsqrt(jnp.mean(xq * xq, axis=0, keepdims=True) + NORM_EPS)
            y = xq * inv * gq
            qT_ref[0, per_sec * sec + hl] = (rope_t(y) * (HEAD_DIM ** -0.5 * LOG2E)).astype(BF16)
    o0 = Q_COLS
    gk = gk_ref[...]
    pk = proj(o0, KV_COLS)
    for j in range(N_KV_HEADS // 2):
        pair = []
        for g in (2 * j, 2 * j + 1):
            xk = pk[HEAD_DIM * g:HEAD_DIM * (g + 1)]
            inv = lax.rsqrt(jnp.mean(xk * xk, axis=0, keepdims=True) + NORM_EPS)
            pair.append(rope_t(xk * inv * gk))
        kr = jnp.concatenate(pair, axis=0).T
        k4_ref[0, 2 * j] = kr[:, 0:HEAD_DIM].astype(BF16)
        k4_ref[0, 2 * j + 1] = kr[:, HEAD_DIM:LANES].astype(BF16)
    o0 += KV_COLS
    vc = vT_ref.shape[-1]
    ones = jnp.ones((V_AUG - HEAD_DIM, vc), BF16)
    pv = proj(o0, KV_COLS).astype(BF16)
    for g in range(N_KV_HEADS):
        for u in range(tm // vc):
            vT_ref[0, g, u, 0:HEAD_DIM, :] = pv[HEAD_DIM * g:HEAD_DIM * (g + 1), vc * u:vc * (u + 1)]
            vT_ref[0, g, u, HEAD_DIM:V_AUG, :] = ones
    o0 += KV_COLS
    for sec in range(QI_COLS // MXU_DIM):
        pqi = proj(o0 + MXU_DIM * sec, MXU_DIM)
        for hl in range(per_sec):
            qiT_ref[0, per_sec * sec + hl] = rope_t(pqi[IDX_DIM * hl:IDX_DIM * (hl + 1)]).astype(BF16)
    o0 += QI_COLS
    tail = proj(o0, wt_ref.shape[0] - o0)
    t = tail[0:IDX_DIM]
    mu = jnp.mean(t, axis=0, keepdims=True)
    dlt = t - mu
    var = jnp.mean(dlt * dlt, axis=0, keepdims=True)
    kin = rope_t(dlt * lax.rsqrt(var + NORM_EPS) * lng_ref[...] + lnb_ref[...])
    kin = jnp.concatenate([kin, jnp.zeros((LANES - IDX_DIM, tm), F32)], axis=0).T
    ki_ref[...] = kin[:, 0:IDX_DIM].astype(BF16)
    wiT_ref[0] = tail[IDX_DIM:IDX_DIM + IDX_HEADS] * (IDX_HEADS ** -0.5 * IDX_DIM ** -0.5)


def _attn_pre(x2d, mod, g1, w_in, gq, gk, lng, lnb, positions, bn, t_len, tm, vc):
    n, d = x2d.shape
    tpb = t_len // tm
    half = HEAD_DIM // 2
    ma = w_in.shape[1]
    ma_pad = -(-ma // BF16_ROWS) * BF16_ROWS
    wt = jnp.concatenate([w_in, jnp.zeros((d, ma_pad - ma), F32)], axis=1).T.astype(BF16)
    inv_freq = ROPE_THETA ** (-(jnp.arange(half, dtype=F32) * 2.0 / HEAD_DIM))
    invft = jnp.broadcast_to(inv_freq[:, None], (half, tm))
    col = lambda v: jnp.broadcast_to(v.astype(F32)[:, None], (v.shape[0], tm))
    posr = positions.reshape(bn, 1, t_len)

    row = lambda i: (i, 0)
    outs = pl.pallas_call(
        _attn_pre_kernel,
        grid=(n // tm,),
        in_specs=[pl.BlockSpec((tm, d), row),
                  pl.BlockSpec((1, 6, d), lambda i: (i // tpb, 0, 0)),
                  _const_spec((1, d)),
                  _const_spec((ma_pad, d)),
                  pl.BlockSpec((1, 1, tm), lambda i: (i // tpb, 0, i % tpb)),
                  _const_spec((half, tm)),
                  _const_spec((HEAD_DIM, tm)),
                  _const_spec((HEAD_DIM, tm)),
                  _const_spec((IDX_DIM, tm)),
                  _const_spec((IDX_DIM, tm))],
        out_specs=[pl.BlockSpec((1, N_HEADS, HEAD_DIM, tm), lambda i: (i // tpb, 0, 0, i % tpb)),
                   pl.BlockSpec((1, IDX_HEADS, IDX_DIM, tm), lambda i: (i // tpb, 0, 0, i % tpb)),
                   pl.BlockSpec((1, IDX_HEADS, tm), lambda i: (i // tpb, 0, i % tpb)),
                   pl.BlockSpec((1, N_KV_HEADS, tm // vc, V_AUG, vc), lambda i: (i // tpb, 0, i % tpb, 0, 0)),
                   pl.BlockSpec((1, N_KV_HEADS, tm, HEAD_DIM), lambda i: (i // tpb, 0, i % tpb, 0)),
                   pl.BlockSpec((tm, IDX_DIM), row)],
        out_shape=[jax.ShapeDtypeStruct((bn, N_HEADS, HEAD_DIM, t_len), BF16),
                   jax.ShapeDtypeStruct((bn, IDX_HEADS, IDX_DIM, t_len), BF16),
                   jax.ShapeDtypeStruct((bn, IDX_HEADS, t_len), F32),
                   jax.ShapeDtypeStruct((bn, N_KV_HEADS, t_len // vc, V_AUG, vc), BF16),
                   jax.ShapeDtypeStruct((bn, N_KV_HEADS, t_len, HEAD_DIM), BF16),
                   jax.ShapeDtypeStruct((n, IDX_DIM), BF16)],
        compiler_params=_params(1),
        name="attn_pre",
    )(x2d, mod, g1.reshape(1, d), wt, posr, invft, col(gq), col(gk), col(lng), col(lnb))
    return outs


def _dsa_kernel(qT_ref, qiT_ref, wiT_ref, k4_ref, ki_ref, vT_ref, o_ref,
                st_ref, hi_ref, m_ref, al_ref, acc_ref, bias_ref, lg_ref, left_ref, ltri_ref, *, topk, rb):
    tq = o_ref.shape[1]
    kc = tq
    i = pl.program_id(1)
    n_c = i + 1
    q0 = i * tq

    tcol = q0 + lax.broadcasted_iota(I32, (kc, tq), 1)
    srow0 = lax.broadcasted_iota(I32, (kc, tq), 0)
    wi = wiT_ref[0]
    qi_all = jnp.concatenate([qiT_ref[0, hh] for hh in range(IDX_HEADS)], axis=1)

    def score_chunk(c):
        r0 = pl.multiple_of(c * kc, kc)
        ki = ki_ref[0, pl.ds(r0, kc), :]
        s_all = jnp.dot(ki, qi_all, preferred_element_type=F32)
        acc = jnp.zeros((kc, tq), F32)
        for hh in range(IDX_HEADS):
            acc = acc + jnp.maximum(s_all[:, tq * hh:tq * (hh + 1)], 0.0) * wi[hh:hh + 1, :]
        causal = (srow0 + r0) <= tcol
        sc = jnp.where(causal, acc, -jnp.inf)
        st_ref[pl.ds(r0, kc), :] = sc
        top = lax.bitcast_convert_type(sc, I32) & np.int32(-65536)
        hi_ref[pl.ds(r0, kc), :] = lax.bitcast_convert_type(top, F32).astype(BF16)

    def score_group(j, carry):
        for u in range(SCORE_GROUP):
            score_chunk(SCORE_GROUP * j + u)
        return carry

    def score_single(c, carry):
        score_chunk(c)
        return carry

    n_grp = n_c // SCORE_GROUP
    lax.fori_loop(0, n_grp, score_group, 0)
    lax.fori_loop(n_grp * SCORE_GROUP, n_c, score_single, 0)

    n_big = (n_c * kc) // rb
    n_small = (n_c * kc - n_big * rb) // kc
    n_par = 4
    pack = BF16_ROWS

    def all_sublanes(tot):
        for sh in (4, 2, 1):
            tot = tot + pltpu.roll(tot, sh, 0)
        return tot

    def two_level(block_fn, init):
        acc = lax.fori_loop(0, n_big, lambda r, a: block_fn(pl.multiple_of(r * rb, rb), rb, a), init)
        base = n_big * rb
        return lax.fori_loop(0, n_small, lambda r, a: block_fn(pl.multiple_of(base + r * kc, kc), kc, a), acc)

    def count(pred):
        def block(r0, rows, acc):
            blk = st_ref[pl.ds(r0, rows), :].reshape(rows // SUBLANES, SUBLANES, tq)
            hit = pred(blk).astype(I32).reshape(rows // (SUBLANES * n_par), n_par, SUBLANES, tq)
            return acc + jnp.sum(hit, axis=0)
        acc = two_level(block, jnp.zeros((n_par, SUBLANES, tq), I32))
        return all_sublanes(jnp.sum(acc, axis=0))

    def count_hi(cand16):
        def block(r0, rows, acc):
            blk = hi_ref[pl.ds(r0, rows), :].reshape(rows // pack, pack, tq)
            hit = jnp.where(blk >= cand16[None], jnp.ones((), BF16), jnp.zeros((), BF16))
            chains = [hit[k] for k in range(n_par)]
            for g in range(n_par, rows // pack):
                chains[g % n_par] = chains[g % n_par] + hit[g]
            return acc + ((chains[0] + chains[1]) + (chains[2] + chains[3])).astype(F32)
        acc = two_level(block, jnp.zeros((pack, tq), F32))
        return all_sublanes((acc[0:SUBLANES] + acc[SUBLANES:pack]).astype(I32))

    def key_to_float(u):
        ks = u ^ INT_MIN
        bits = jnp.where(ks < 0, ks ^ np.int32(0x7FFFFFFF), ks)
        return ks, bits

    def hi_body(it, carry):
        tau_u, cnt_tau = carry
        cand_u = tau_u | lax.shift_left(np.int32(1), np.int32(31) - it)
        ks, bits = key_to_float(cand_u)
        cand_top = lax.bitcast_convert_type(bits & np.int32(-65536), F32)
        cand16 = jnp.concatenate([cand_top, cand_top], axis=0).astype(BF16)
        cnt = count_hi(cand16)
        take = (cnt >= topk) | (ks < KEY_HI_NEG_INF)
        return jnp.where(take, cand_u, tau_u), jnp.where(take, cnt, cnt_tau)

    def lo_body(it, carry):
        tau_u, cnt_tau = carry
        cand_u = tau_u | lax.shift_left(np.int32(1), np.int32(15) - it)
        ks, bits = key_to_float(cand_u)
        cand_f = lax.bitcast_convert_type(bits, F32)
        cnt = count(lambda blk: blk >= cand_f[None])
        take = (cnt >= topk) | (ks < KEY_NEG_INF)
        return jnp.where(take, cand_u, tau_u), jnp.where(take, cnt, cnt_tau)

    def lo_stage(state, first, n):
        tau_u, cnt_tau, _ = state
        tau_u, cnt_tau = lax.fori_loop(first, first + n, lo_body, (tau_u, cnt_tau))
        tau8 = lax.bitcast_convert_type(key_to_float(tau_u)[1], F32)
        return tau_u, cnt_tau, count(lambda blk: blk > tau8[None])

    def pending(state):
        tau_u, cnt_tau, cnt_gt = state
        open_lane = ((cnt_tau != topk) & (cnt_gt >= topk)) | ((tau_u ^ INT_MIN) < KEY_NEG_INF)
        return jnp.max(open_lane.astype(I32)) > 0

    zero8 = jnp.zeros((SUBLANES, tq), I32)
    tau_u, cnt_tau = lax.fori_loop(0, 16, hi_body, (zero8, zero8))
    state = lo_stage((tau_u, cnt_tau, zero8), 0, 10)
    state = lax.cond(pending(state), lambda s: lo_stage(s, 10, 2), lambda s: s, state)
    state = lax.cond(pending(state), lambda s: lo_stage(s, 12, 4), lambda s: s, state)
    tau_u, _, cnt_gt = state
    tau8 = lax.bitcast_convert_type(key_to_float(tau_u)[1], F32)
    left_ref[...] = jnp.where(tau8 == -jnp.inf, 0.0, (topk - cnt_gt).astype(F32))
    tau1 = tau8[0:1]
    ri = lax.broadcasted_iota(I32, (kc, kc), 0)
    ci = lax.broadcasted_iota(I32, (kc, kc), 1)
    ltri_ref[...] = jnp.where(ci <= ri, 1.0, 0.0).astype(BF16)

    m_ref[...] = jnp.full(m_ref.shape, -(2.0 ** 100), F32)
    acc_ref[...] = jnp.zeros(acc_ref.shape, F32)

    def set_bias(c):
        r0 = pl.multiple_of(c * kc, kc)
        sc = st_ref[pl.ds(r0, kc), :]
        tie = sc == tau1
        seen = jnp.dot(ltri_ref[...], jnp.where(tie, 1.0, 0.0).astype(BF16), preferred_element_type=F32)
        left = left_ref[0:1, :]
        mask = (sc > tau1) | (tie & (seen <= left))
        bias_ref[...] = jnp.where(mask, 0.0, -jnp.inf).astype(BF16)
        left_ref[...] = jnp.broadcast_to(left - seen[kc - 1:kc, :], left_ref.shape)

    def logits_stage(c, hh):
        r0 = pl.multiple_of(c * kc, kc)
        kg = k4_ref[0, hh // GROUP, pl.ds(r0, kc), :]
        lg = jnp.dot(kg, qT_ref[0, hh], preferred_element_type=F32).astype(BF16) + bias_ref[...]
        lg_ref[hh] = lg
        parts = [lg[pack * g:pack * (g + 1)] for g in range(kc // pack)]
        while len(parts) > 1:
            parts = [jnp.maximum(parts[g], parts[g + 1]) for g in range(0, len(parts), 2)]
        mx = parts[0].astype(F32)
        m_old = m_ref[hh:hh + 1, :]
        m_new = jnp.maximum(m_old, jnp.max(mx, axis=0, keepdims=True))
        m_ref[hh:hh + 1, :] = m_new
        al_ref[hh:hh + 1, :] = jnp.exp2(m_old - m_new)

    def value_stage(c, hh):
        vt = vT_ref[0, hh // GROUP, c]
        m_b = jnp.broadcast_to(m_ref[hh:hh + 1, :], (pack, tq)).astype(BF16)
        p = jnp.exp2(lg_ref[hh].reshape(kc // pack, pack, tq) - m_b[None]).reshape(kc, tq)
        acc_ref[hh] = acc_ref[hh] * al_ref[hh:hh + 1, :] + jnp.dot(vt, p, preferred_element_type=F32)

    set_bias(0)
    for hh in range(N_HEADS):
        logits_stage(0, hh)

    def attn_step(c):
        set_bias(c)
        for hh in range(N_HEADS):
            value_stage(c - 1, hh)
            logits_stage(c, hh)

    def attn_group(j, carry):
        for u in range(ATTN_GROUP):
            attn_step(ATTN_GROUP * j + 1 + u)
        return carry

    def attn_single(c, carry):
        attn_step(c)
        return carry

    n_grp = (n_c - 1) // ATTN_GROUP
    lax.fori_loop(0, n_grp, attn_group, 0)
    lax.fori_loop(ATTN_GROUP * n_grp + 1, n_c, attn_single, 0)
    for hh in range(N_HEADS):
        value_stage(n_c - 1, hh)

    for hp in range(N_HEADS // 2):
        parts = []
        for hh in (2 * hp, 2 * hp + 1):
            a = acc_ref[hh]
            parts.append(a[0:HEAD_DIM] / a[HEAD_DIM:HEAD_DIM + 1])
        pair = jnp.concatenate(parts, axis=0)
        o_ref[0, :, LANES * hp:LANES * (hp + 1)] = pair.T.astype(BF16)


def _dsa_attention(qT, qiT, wiT, k4, ki, vT, bn, t_len, topk):
    tq = topk
    kern = functools.partial(_dsa_kernel, topk=topk, rb=COUNT_ROWS)
    return pl.pallas_call(
        kern,
        grid=(bn, t_len // tq),
        in_specs=[pl.BlockSpec((1, N_HEADS, HEAD_DIM, tq), lambda b, i: (b, 0, 0, i)),
                  pl.BlockSpec((1, IDX_HEADS, IDX_DIM, tq), lambda b, i: (b, 0, 0, i)),
                  pl.BlockSpec((1, IDX_HEADS, tq), lambda b, i: (b, 0, i)),
                  pl.BlockSpec((1, N_KV_HEADS, t_len, HEAD_DIM), lambda b, i: (b, 0, 0, 0),
                               pipeline_mode=pl.Buffered(1)),
                  pl.BlockSpec((1, t_len, IDX_DIM), lambda b, i: (b, 0, 0), pipeline_mode=pl.Buffered(1)),
                  pl.BlockSpec((1, N_KV_HEADS, t_len // tq, V_AUG, tq), lambda b, i: (b, 0, 0, 0, 0),
                               pipeline_mode=pl.Buffered(1))],
        out_specs=pl.BlockSpec((1, tq, Q_COLS), lambda b, i: (b, i, 0)),
        out_shape=jax.ShapeDtypeStruct((bn, t_len, Q_COLS), BF16),
        scratch_shapes=[pltpu.VMEM((t_len, tq), F32),
                        pltpu.VMEM((t_len, tq), BF16),
                        pltpu.VMEM((N_HEADS, tq), F32),
                        pltpu.VMEM((N_HEADS, tq), F32),
                        pltpu.VMEM((N_HEADS, V_AUG, tq), F32),
                        pltpu.VMEM((tq, tq), BF16),
                        pltpu.VMEM((N_HEADS, tq, tq), BF16),
                        pltpu.VMEM((SUBLANES, tq), F32),
                        pltpu.VMEM((tq, tq), BF16)],
        compiler_params=_params(2),
        name="dsa_attention",
    )(qT, qiT, wiT, k4, ki.reshape(bn, t_len, IDX_DIM), vT)


def _conv_pre_kernel(x_ref, mod_ref, g1_ref, w_ref, cw_ref, o_ref, zbuf_ref, *, tpb):
    tm, d = x_ref.shape
    i = pl.program_id(0)

    @pl.when(i % tpb == 0)
    def _():
        zbuf_ref[0:SUBLANES, :] = jnp.zeros((SUBLANES, d), F32)

    x = x_ref[...]
    sh = mod_ref[0, 0:1, :]
    sc = mod_ref[0, 1:2, :]
    ms = jnp.mean(x * x, axis=-1, keepdims=True)
    h = (x * lax.rsqrt(ms + NORM_EPS) * g1_ref[...]) * (1.0 + sc) + sh
    hb = h.astype(BF16)
    for j in range(d // MXU_DIM):
        cols = slice(MXU_DIM * j, MXU_DIM * (j + 1))
        part = [jnp.dot(hb, w_ref[:, s * d + MXU_DIM * j:s * d + MXU_DIM * (j + 1)], preferred_element_type=F32)
                for s in range(3)]
        z = part[1] * part[2]
        zbuf_ref[SUBLANES:SUBLANES + tm, cols] = z
        z1 = zbuf_ref[SUBLANES - 1:SUBLANES - 1 + tm, cols]
        z2 = zbuf_ref[SUBLANES - 2:SUBLANES - 2 + tm, cols]
        zc = cw_ref[0:1, cols] * z2 + cw_ref[1:2, cols] * z1 + cw_ref[2:3, cols] * z
        o_ref[:, cols] = (part[0] * zc).astype(BF16)
        zbuf_ref[0:SUBLANES, cols] = z[tm - SUBLANES:tm, :]


def _conv_pre(x2d, mod, g1, w_in, conv_w, t_len, tm):
    n, d = x2d.shape
    tpb = t_len // tm
    row = lambda i: (i, 0)
    return pl.pallas_call(
        functools.partial(_conv_pre_kernel, tpb=tpb),
        grid=(n // tm,),
        in_specs=[pl.BlockSpec((tm, d), row),
                  pl.BlockSpec((1, 6, d), lambda i: (i // tpb, 0, 0)),
                  _const_spec((1, d)),
                  _const_spec((d, 3 * d)),
                  _const_spec((CONV_WIDTH, d))],
        out_specs=pl.BlockSpec((tm, d), row),
        out_shape=jax.ShapeDtypeStruct((n, d), BF16),
        scratch_shapes=[pltpu.VMEM((tm + SUBLANES, d), F32)],
        compiler_params=_params(1),
        name="conv_pre",
    )(x2d, mod, g1.reshape(1, d), w_in.astype(BF16), conv_w.astype(F32))


def _post_kernel(x_ref, mix_ref, mod_ref, g2_ref, wo_ref, wg_ref, wu_ref, wd_ref, o_ref, *, th):
    gate1 = mod_ref[0, 2:3, :]
    sh2 = mod_ref[0, 3:4, :]
    sc2 = mod_ref[0, 4:5, :]
    gate2 = mod_ref[0, 5:6, :]
    y = jnp.dot(mix_ref[...], wo_ref[...], preferred_element_type=F32)
    x1 = x_ref[...] + gate1 * y
    ms = jnp.mean(x1 * x1, axis=-1, keepdims=True)
    h = ((x1 * lax.rsqrt(ms + NORM_EPS) * g2_ref[...]) * (1.0 + sc2) + sh2).astype(BF16)
    hidden = wg_ref.shape[1]
    acc = jnp.zeros(x1.shape, F32)
    for j in range(hidden // th):
        gt = jnp.dot(h, wg_ref[:, th * j:th * (j + 1)], preferred_element_type=F32)
        up = jnp.dot(h, wu_ref[:, th * j:th * (j + 1)], preferred_element_type=F32)
        a = (gt * jax.nn.sigmoid(gt)) * up
        acc = acc + jnp.dot(a.astype(BF16), wd_ref[th * j:th * (j + 1), :], preferred_element_type=F32)
    o_ref[...] = x1 + gate2 * acc


def _post(x2d, mix, mod, g2, w_out, w_gate, w_up, w_down, layer, t_len, tm):
    n, d = x2d.shape
    hidden = w_gate.shape[2]
    tpb = t_len // tm
    row = lambda i: (i, 0)

    def layer_spec(shape):
        return pl.BlockSpec((None,) + shape, lambda i: (layer, 0, 0), pipeline_mode=pl.Buffered(1))
    return pl.pallas_call(
        functools.partial(_post_kernel, th=FFN_CHUNK),
        grid=(n // tm,),
        in_specs=[pl.BlockSpec((tm, d), row),
                  pl.BlockSpec((tm, d), row),
                  pl.BlockSpec((1, 6, d), lambda i: (i // tpb, 0, 0)),
                  _const_spec((1, d)),
                  _const_spec((d, d)),
                  layer_spec((d, hidden)),
                  layer_spec((d, hidden)),
                  layer_spec((hidden, d))],
        out_specs=pl.BlockSpec((tm, d), row),
        out_shape=jax.ShapeDtypeStruct((n, d), F32),
        compiler_params=_params(1),
        name="mixer_out_ffn",
    )(x2d, mix, mod, g2.reshape(1, d), w_out.astype(BF16), w_gate, w_up, w_down)


def kernel(x, c, positions, ada_w, ada_b, norm1_g, norm2_g, attn_w_in, attn_q_norm_g, attn_k_norm_g,
           idx_k_ln_g, idx_k_ln_b, attn_w_out, conv_w_in, conv_w, conv_w_out, ffn_w_gate, ffn_w_up,
           ffn_w_down):
    bn, t_len, d = x.shape
    depth = ada_w.shape[0]
    topk = min(TOPK_MAX, t_len // 4)
    assert topk == TOPK_MAX == MXU_DIM and d == Q_COLS
    assert t_len % TM_PRE == 0 and t_len % TM_ROWS == 0 and TM_PRE % topk == 0 and COUNT_ROWS % topk == 0
    n = bn * t_len
    mod = _modulation(c, ada_w, ada_b)
    wg_all, wu_all, wd_all = (w.astype(BF16) for w in (ffn_w_gate, ffn_w_up, ffn_w_down))
    x2d = x.reshape(n, d)
    for i in range(depth):
        j = i // 2
        if i % 2 == 0:
            qT, qiT, wiT, vT, k4, ki = _attn_pre(
                x2d, mod[i], norm1_g[i], attn_w_in[j], attn_q_norm_g[j], attn_k_norm_g[j],
                idx_k_ln_g[j], idx_k_ln_b[j], positions, bn, t_len, TM_PRE, topk)
            mix = _dsa_attention(qT, qiT, wiT, k4, ki, vT, bn, t_len, topk).reshape(n, d)
            w_mix_out = attn_w_out[j]
        else:
            mix = _conv_pre(x2d, mod[i], norm1_g[i], conv_w_in[j], conv_w[j], t_len, TM_ROWS)
            w_mix_out = conv_w_out[j]
        x2d = _post(x2d, mix, mod[i], norm2_g[i], w_mix_out, wg_all, wu_all, wd_all, i, t_len, TM_ROWS)
    return x2d.reshape(bn, t_len, d)
```
